```python
import jax, jax.numpy as jnp
from jax import lax
import numpy as np

D_MODEL = 1024
BATCH = 8
SEQ = 4096
DEPTH = 4

HEAD_DIM = 64
ROPE_THETA = 10000.0
NORM_EPS = 1e-6
NEG_INF = -1e30
D_FF = 2816

DIL_HEADS = 8
DIL_CONFIGS = ((128, 1), (512, 4), (2048, 16))
NA_HEADS = 8
GRID_W = 64
NA_ROWS = 8
NA_COLS = 16
NA_QCOLS = 16
NA_KCOLS = 32

SWA_Q_HEADS = 8
SWA_KV_HEADS = 2
SWA_HALF = 128
SWA_BLOCK = 128
MLA_HEADS = 8
MLA_Q_RANK = 384
MLA_KV_RANK = 256
MLA_NOPE = 64
MLA_ROPE = 32
MLA_V = 64
MLA_Q_BLOCK = 128

EVEN_SPLITS = (DIL_HEADS * HEAD_DIM,) * 3 + (NA_HEADS * HEAD_DIM,) * 3
EVEN_IN = sum(EVEN_SPLITS)
EVEN_MIX = (DIL_HEADS + NA_HEADS) * HEAD_DIM
ODD_SPLITS = (SWA_Q_HEADS * HEAD_DIM, SWA_KV_HEADS * HEAD_DIM, SWA_KV_HEADS * HEAD_DIM,
              MLA_Q_RANK, MLA_KV_RANK, MLA_ROPE)
ODD_IN = sum(ODD_SPLITS)
ODD_MIX = SWA_Q_HEADS * HEAD_DIM + MLA_HEADS * MLA_V

kernel_name = 'hybrid_bidir_encoder'


def rms_norm(x, g):
    xf = x.astype(jnp.float32)
    y = xf * lax.rsqrt(jnp.mean(xf * xf, axis=-1, keepdims=True) + NORM_EPS)
    return (y * g.astype(jnp.float32)).astype(x.dtype)


def rope_tables(seq, dim):
    pos = jnp.arange(seq, dtype=jnp.float32)
    inv_freq = ROPE_THETA ** (-jnp.arange(0, dim, 2, dtype=jnp.float32) / dim)
    ang = pos[:, None] * inv_freq[None, :]
    return jnp.cos(ang), jnp.sin(ang)


def apply_rope(x, cos, sin):
    half = x.shape[-1] // 2
    bshape = (cos.shape[0],) + (1,) * (x.ndim - 3) + (half,)
    c, s = cos.reshape(bshape), sin.reshape(bshape)
    xf = x.astype(jnp.float32)
    x1, x2 = xf[..., :half], xf[..., half:]
    return jnp.concatenate([x1 * c - x2 * s, x2 * c + x1 * s], axis=-1).astype(x.dtype)


def swiglu(x, w1, w3, w2):
    return (jax.nn.silu(x @ w1) * (x @ w3)) @ w2


def banded_window_attn(q, k, v, half, block, sink=None):
    n, L, hk, g, dh = q.shape
    nb = -(-L // block)
    lp = nb * block
    scale = dh ** -0.5
    qb = jnp.pad(q, ((0, 0), (0, lp - L), (0, 0), (0, 0), (0, 0))).reshape(n, nb, block, hk, g, dh)
    pad_kv = ((0, 0), (block, lp - L + block), (0, 0), (0, 0))
    kp = jnp.pad(k, pad_kv).reshape(n, nb + 2, block, hk, dh)
    vp = jnp.pad(v, pad_kv).reshape(n, nb + 2, block, hk, dh)
    kb = jnp.concatenate([kp[:, :-2], kp[:, 1:-1], kp[:, 2:]], axis=2)
    vb = jnp.concatenate([vp[:, :-2], vp[:, 1:-1], vp[:, 2:]], axis=2)
    sc = jnp.einsum('nbqhgd,nbkhd->nbhgqk', qb, kb, preferred_element_type=jnp.float32) * scale
    qpos = np.arange(nb)[:, None] * block + np.arange(block)[None, :]
    kpos = (np.arange(nb)[:, None] - 1) * block + np.arange(3 * block)[None, :]
    valid = ((np.abs(qpos[:, :, None] - kpos[:, None, :]) <= half)
             & (kpos[:, None, :] >= 0) & (kpos[:, None, :] < L))
    sc = jnp.where(valid[None, :, None, None], sc, NEG_INF)
    m = jnp.max(sc, axis=-1)
    if sink is not None:
        sk = sink.astype(jnp.float32).reshape(1, 1, hk, g, 1)
        m = jnp.maximum(m, sk)
    p = jnp.exp(sc - m[..., None])
    denom = jnp.sum(p, axis=-1)
    if sink is not None:
        denom = denom + jnp.exp(sk - m)
    o = jnp.einsum('nbhgqk,nbkhd->nbqhgd', p, vb.astype(jnp.float32))
    o = o / jnp.moveaxis(denom, -1, 2)[..., None]
    lse = jnp.moveaxis(m + jnp.log(denom), -1, 2)
    o = o.reshape(n, lp, hk, g, dh)[:, :L].astype(q.dtype)
    lse = lse.reshape(n, lp, hk, g)[:, :L]
    return o, lse


def dilated_attention(q, k, v):
    b, s, h, dh = q.shape
    outs, lses = [], []
    for window, dil in DIL_CONFIGS:
        half = window // 2 // dil
        sd = s // dil

        def to_res(t):
            return t.reshape(b, sd, dil, h, dh).transpose(0, 2, 1, 3, 4).reshape(b * dil, sd, h, dh)

        o, lse = banded_window_attn(to_res(q)[:, :, :, None], to_res(k), to_res(v), half, half)
        outs.append(o[:, :, :, 0].reshape(b, dil, sd, h, dh).transpose(0, 2, 1, 3, 4).reshape(b, s, h, dh))
        lses.append(lse[..., 0].reshape(b, dil, sd, h).transpose(0, 2, 1, 3).reshape(b, s, h))
    w = jax.nn.softmax(jnp.stack(lses), axis=0)
    out = jnp.einsum('cbsh,cbshd->bshd', w, jnp.stack(outs).astype(jnp.float32))
    return out.astype(q.dtype)


def neighbourhood_attention(q, k, v, rpb):
    b, s, h, dh = q.shape
    rows = s // GRID_W
    kr = min(NA_ROWS, rows)
    ncb = GRID_W // NA_QCOLS
    scale = dh ** -0.5
    qg = q.reshape(b, rows, GRID_W, h, dh)
    kg = k.reshape(b, rows, GRID_W, h, dh)
    vg = v.reshape(b, rows, GRID_W, h, dh)
    qcol = np.arange(GRID_W).reshape(ncb, NA_QCOLS)
    kstart = np.clip(np.arange(ncb) * NA_QCOLS - NA_COLS // 2, 0, GRID_W - NA_KCOLS)
    kcol = kstart[:, None] + np.arange(NA_KCOLS)[None, :]
    wstart = np.clip(qcol - NA_COLS // 2, 0, GRID_W - NA_COLS)
    col_valid = ((kcol[:, None, :] >= wstart[..., None])
                 & (kcol[:, None, :] < wstart[..., None] + NA_COLS))
    dc_idx = np.clip(kcol[:, None, :] - qcol[..., None] + NA_COLS - 1, 0, 2 * NA_COLS - 2)
    rpb_c = rpb[:, :, dc_idx]

    def row_step(r):
        rs = jnp.clip(r - kr // 2, 0, rows - kr)
        k_rows = lax.dynamic_slice_in_dim(kg, rs, kr, axis=1)[:, :, kcol]
        v_rows = lax.dynamic_slice_in_dim(vg, rs, kr, axis=1)[:, :, kcol]
        q_row = lax.dynamic_index_in_dim(qg, r, axis=1, keepdims=False).reshape(b, ncb, NA_QCOLS, h, dh)
        sc = jnp.einsum('bmqhd,brmkhd->bhmqrk', q_row, k_rows, preferred_element_type=jnp.float32) * scale
        dr_idx = rs - r + jnp.arange(kr) + NA_ROWS - 1
        bias = jnp.take(rpb_c, dr_idx, axis=1).transpose(0, 2, 3, 1, 4)
        sc = sc + bias[None].astype(jnp.float32)
        sc = jnp.where(col_valid[None, None, :, :, None, :], sc, NEG_INF)
        p = jax.nn.softmax(sc, axis=(-2, -1))
        o = jnp.einsum('bhmqrk,brmkhd->bmqhd', p, v_rows.astype(jnp.float32))
        return o.reshape(b, GRID_W, h, dh).astype(q.dtype)

    out = lax.map(row_step, jnp.arange(rows))
    return out.transpose(1, 0, 2, 3, 4).reshape(b, s, h, dh)


def mla_attention(q_a, kv_a, k_pe, q_norm, w_qb, kv_norm, w_kvb, cos_r, sin_r):
    b, s, _ = q_a.shape
    q = (rms_norm(q_a, q_norm) @ w_qb).reshape(b, s, MLA_HEADS, MLA_NOPE + MLA_ROPE)
    q_nope = q[..., :MLA_NOPE]
    q_pe = apply_rope(q[..., MLA_NOPE:], cos_r, sin_r)
    kv = (rms_norm(kv_a, kv_norm) @ w_kvb).reshape(b, s, MLA_HEADS, MLA_NOPE + MLA_V)
    k_nope = kv[..., :MLA_NOPE]
    v_f = kv[..., MLA_NOPE:].astype(jnp.float32)
    k_pe = apply_rope(k_pe, cos_r, sin_r)
    scale = (MLA_NOPE + MLA_ROPE) ** -0.5
    nqb = s // MLA_Q_BLOCK

    def blocks(t):
        return t.reshape((b, nqb, MLA_Q_BLOCK) + t.shape[2:]).swapaxes(0, 1)

    def step(args):
        qn, qp = args
        sc = (jnp.einsum('bqhd,bkhd->bhqk', qn, k_nope, preferred_element_type=jnp.float32)
              + jnp.einsum('bqhr,bkr->bhqk', qp, k_pe, preferred_element_type=jnp.float32)) * scale
        p = jax.nn.softmax(sc, axis=-1)
        return jnp.einsum('bhqk,bkhd->bqhd', p, v_f)

    o = lax.map(step, (blocks(q_nope), blocks(q_pe)))
    return o.swapaxes(0, 1).reshape(b, s, MLA_HEADS * MLA_V).astype(q_a.dtype)


def even_mixer(h, w_in, w_out, rpb, cos, sin):
    b, s, _ = h.shape
    proj = h @ w_in
    qa, ka, va, qn, kn, vn = jnp.split(proj, np.cumsum(EVEN_SPLITS[:-1]).tolist(), axis=-1)
    qa = apply_rope(qa.reshape(b, s, DIL_HEADS, HEAD_DIM), cos, sin)
    ka = apply_rope(ka.reshape(b, s, DIL_HEADS, HEAD_DIM), cos, sin)
    oa = dilated_attention(qa, ka, va.reshape(b, s, DIL_HEADS, HEAD_DIM))
    on = neighbourhood_attention(qn.reshape(b, s, NA_HEADS, HEAD_DIM), kn.reshape(b, s, NA_HEADS, HEAD_DIM),
                                 vn.reshape(b, s, NA_HEADS, HEAD_DIM), rpb)
    mixed = jnp.concatenate([oa.reshape(b, s, DIL_HEADS * HEAD_DIM), on.reshape(b, s, NA_HEADS * HEAD_DIM)], axis=-1)
    return mixed @ w_out


def odd_mixer(h, w_in, w_out, sink, q_norm, w_qb, kv_norm, w_kvb, cos, sin, cos_r, sin_r):
    b, s, _ = h.shape
    proj = h @ w_in
    q_c, k_c, v_c, q_a, kv_a, k_pe = jnp.split(proj, np.cumsum(ODD_SPLITS[:-1]).tolist(), axis=-1)
    grp = SWA_Q_HEADS // SWA_KV_HEADS
    qc = apply_rope(q_c.reshape(b, s, SWA_Q_HEADS, HEAD_DIM), cos, sin).reshape(b, s, SWA_KV_HEADS, grp, HEAD_DIM)
    kc = apply_rope(k_c.reshape(b, s, SWA_KV_HEADS, HEAD_DIM), cos, sin)
    vc = v_c.reshape(b, s, SWA_KV_HEADS, HEAD_DIM)
    oc, _ = banded_window_attn(qc, kc, vc, SWA_HALF, SWA_BLOCK, sink.reshape(SWA_KV_HEADS, grp))
    od = mla_attention(q_a, kv_a, k_pe, q_norm, w_qb, kv_norm, w_kvb, cos_r, sin_r)
    mixed = jnp.concatenate([oc.reshape(b, s, SWA_Q_HEADS * HEAD_DIM), od], axis=-1)
    return mixed @ w_out


def setup_inputs(seed: int = 0) -> dict:
    key = jax.random.key(seed)
    ks = jax.random.split(key, 21)
    n_even, n_odd = (DEPTH + 1) // 2, DEPTH // 2

    def dense(k, shape, fan_in):
        return jax.random.normal(k, shape, jnp.float32) * fan_in ** -0.5

    def gain(k, shape):
        return 1.0 + 0.02 * jax.random.normal(k, shape, jnp.float32)

    return {
        'x': jax.random.normal(ks[0], (BATCH, SEQ, D_MODEL), jnp.float32),
        'ffn1_norm': gain(ks[1], (DEPTH, D_MODEL)),
        'ffn1_w1': dense(ks[2], (DEPTH, D_MODEL, D_FF), D_MODEL),
        'ffn1_w3': dense(ks[3], (DEPTH, D_MODEL, D_FF), D_MODEL),
        'ffn1_w2': dense(ks[4], (DEPTH, D_FF, D_MODEL), D_FF),
        'mix_norm': gain(ks[5], (DEPTH, D_MODEL)),
        'ffn2_norm': gain(ks[6], (DEPTH, D_MODEL)),
        'ffn2_w1': dense(ks[7], (DEPTH, D_MODEL, D_FF), D_MODEL),
        'ffn2_w3': dense(ks[8], (DEPTH, D_MODEL, D_FF), D_MODEL),
        'ffn2_w2': dense(ks[9], (DEPTH, D_FF, D_MODEL), D_FF),
        'even_w_in': dense(ks[10], (n_even, D_MODEL, EVEN_IN), D_MODEL),
        'even_w_out': dense(ks[11], (n_even, EVEN_MIX, D_MODEL), EVEN_MIX),
        'na_rel_bias': 0.1 * jax.random.normal(ks[12], (n_even, NA_HEADS, 2 * NA_ROWS - 1, 2 * NA_COLS - 1), jnp.float32),
        'odd_w_in': dense(ks[13], (n_odd, D_MODEL, ODD_IN), D_MODEL),
        'odd_w_out': dense(ks[14], (n_odd, ODD_MIX, D_MODEL), ODD_MIX),
        'swa_sink': jax.random.normal(ks[15], (n_odd, SWA_Q_HEADS), jnp.float32),
        'mla_q_norm': gain(ks[16], (n_odd, MLA_Q_RANK)),
        'mla_w_qb': dense(ks[17], (n_odd, MLA_Q_RANK, MLA_HEADS * (MLA_NOPE + MLA_ROPE)), MLA_Q_RANK),
        'mla_kv_norm': gain(ks[18], (n_odd, MLA_KV_RANK)),
        'mla_w_kvb': dense(ks[19], (n_odd, MLA_KV_RANK, MLA_HEADS * (MLA_NOPE + MLA_V)), MLA_KV_RANK),
        'final_norm': gain(ks[20], (D_MODEL,)),
    }


def reference(x, ffn1_norm, ffn1_w1, ffn1_w3, ffn1_w2, mix_norm, ffn2_norm, ffn2_w1, ffn2_w3, ffn2_w2,
              even_w_in, even_w_out, na_rel_bias, odd_w_in, odd_w_out, swa_sink,
              mla_q_norm, mla_w_qb, mla_kv_norm, mla_w_kvb, final_norm):
    s = x.shape[1]
    cos, sin = rope_tables(s, HEAD_DIM)
    cos_r, sin_r = rope_tables(s, MLA_ROPE)
    h = x
    for i in range(DEPTH):
        j = i // 2
        h = h + 0.5 * swiglu(rms_norm(h, ffn1_norm[i]), ffn1_w1[i], ffn1_w3[i], ffn1_w2[i])
        hn = rms_norm(h, mix_norm[i])
        if i % 2 == 0:
            h = h + even_mixer(hn, even_w_in[j], even_w_out[j], na_rel_bias[j], cos, sin)
        else:
            h = h + odd_mixer(hn, odd_w_in[j], odd_w_out[j], swa_sink[j], mla_q_norm[j], mla_w_qb[j],
                              mla_kv_norm[j], mla_w_kvb[j], cos, sin, cos_r, sin_r)
        h = h + 0.5 * swiglu(rms_norm(h, ffn2_norm[i]), ffn2_w1[i], ffn2_w3[i], ffn2_w2[i])
    return rms_norm(h, final_norm)
```

```python
import functools

import jax
import jax.numpy as jnp
import numpy as np
from jax import lax
from jax.experimental import pallas as pl
from jax.experimental.pallas import tpu as pltpu

D_MODEL = 1024
D_FF = 2816
HEAD_DIM = 64
ROPE_THETA = 10000.0
NORM_EPS = 1e-6
NEG_INF = -1e30

DIL_HEADS = 8
DIL_CONFIGS = ((128, 1), (512, 4), (2048, 16))
NA_HEADS = 8
GRID_W = 64
NA_ROWS = 8
NA_COLS = 16

SWA_Q_HEADS = 8
SWA_KV_HEADS = 2
SWA_HALF = 128
MLA_HEADS = 8
MLA_Q_RANK = 384
MLA_KV_RANK = 256
MLA_NOPE = 64
MLA_ROPE = 32
MLA_V = 64

LANES = 128
VMEM_LIMIT = 56 * 1024 * 1024

FFN_TM = 512
FFN_TF = 256
MLA_TQ = 256
MLA_TK = 512

BF16 = jnp.bfloat16
F32 = jnp.float32


def _params(*sem):
    return pltpu.CompilerParams(dimension_semantics=sem, vmem_limit_bytes=VMEM_LIMIT)


def _rms(x, g):
    ms = jnp.mean(x * x, axis=-1, keepdims=True)
    return x * lax.rsqrt(ms + NORM_EPS) * g


def _dot(a, b):
    return jnp.dot(a, b, preferred_element_type=F32)


def _dot_nt(a, b):
    return lax.dot_general(a, b, (((1,), (1,)), ((), ())), preferred_element_type=F32)


def _lane_is_first_head(shape):
    return lax.broadcasted_iota(jnp.int32, shape, len(shape) - 1) < HEAD_DIM


def _rope_block(x, cos, sin_signed, first_half, up, down):
    partner = jnp.where(first_half, pltpu.roll(x, up, 1), pltpu.roll(x, down, 1))
    return x * cos + partner * sin_signed


def _ffn_kernel(x_ref, g_ref, w1_ref, w3_ref, w2_ref, fg_ref, o_ref, acc_ref, *, n_chunks, final):
    x = x_ref[...]
    hn = _rms(x, g_ref[...]).astype(BF16)
    acc_ref[...] = jnp.zeros_like(acc_ref)

    def body(c, carry):
        h1 = _dot(hn, w1_ref[c])
        h3 = _dot(hn, w3_ref[c])
        a = (h1 * jax.nn.sigmoid(h1) * h3).astype(BF16)
        acc_ref[...] += _dot(a, w2_ref[c])
        return carry

    lax.fori_loop(0, n_chunks, body, 0)
    y = x + 0.5 * acc_ref[...]
    if final:
        y = _rms(y, fg_ref[...])
    o_ref[...] = y


def _ffn(x2, g, w1c, w3c, w2c, fg, final):
    m, d = x2.shape
    n_chunks, _, tf = w1c.shape
    tm = FFN_TM
    const3 = lambda i: (0, 0, 0)
    return pl.pallas_call(
        functools.partial(_ffn_kernel, n_chunks=n_chunks, final=final),
        grid=(m // tm,),
        in_specs=[
            pl.BlockSpec((tm, d), lambda i: (i, 0)),
            pl.BlockSpec((1, d), lambda i: (0, 0)),
            pl.BlockSpec((n_chunks, d, tf), const3),
            pl.BlockSpec((n_chunks, d, tf), const3),
            pl.BlockSpec((n_chunks, tf, d), const3),
            pl.BlockSpec((1, d), lambda i: (0, 0)),
        ],
        out_specs=pl.BlockSpec((tm, d), lambda i: (i, 0)),
        out_shape=jax.ShapeDtypeStruct((m, d), F32),
        scratch_shapes=[pltpu.VMEM((tm, d), F32)],
        compiler_params=_params("parallel"),
        name="ffn",
    )(x2, g, w1c, w3c, w2c, fg)


def _ffn_weights(w1, w3, w2):
    d, f = w1.shape
    n = f // FFN_TF
    w1c = w1.astype(BF16).reshape(d, n, FFN_TF).transpose(1, 0, 2)
    w3c = w3.astype(BF16).reshape(d, n, FFN_TF).transpose(1, 0, 2)
    w2c = w2.astype(BF16).reshape(n, FFN_TF, d)
    return w1c, w3c, w2c


def _even_proj_kernel(x_ref, g_ref, w_ref, cos_ref, sin_ref, o_ref, *, n_rope_blocks, n_blocks):
    hn = _rms(x_ref[...], g_ref[...]).astype(BF16)
    cos = cos_ref[...]
    sin = sin_ref[...]
    lane = lax.broadcasted_iota(jnp.int32, cos.shape, 1)
    first_half = (lane % HEAD_DIM) < HEAD_DIM // 2
    scale = HEAD_DIM ** -0.5
    for j in range(n_blocks):
        blk = _dot(hn, w_ref[:, j * LANES:(j + 1) * LANES])
        if j < n_rope_blocks:
            blk = _rope_block(blk, cos, sin, first_half, LANES - HEAD_DIM // 2, HEAD_DIM // 2)
        if j in _EVEN_Q_BLOCKS:
            blk = blk * scale
        o_ref[:, j * LANES:(j + 1) * LANES] = blk.astype(BF16)


_EVEN_Q_BLOCKS = tuple(range(0, 4)) + tuple(range(12, 16))


def _even_proj(h2, g, w_in, cos, sin, seq):
    m, d = h2.shape
    n_out = w_in.shape[1]
    tm = FFN_TM
    per_seq = seq // tm
    return pl.pallas_call(
        functools.partial(_even_proj_kernel, n_rope_blocks=8, n_blocks=n_out // LANES),
        grid=(m // tm,),
        in_specs=[
            pl.BlockSpec((tm, d), lambda i: (i, 0)),
            pl.BlockSpec((1, d), lambda i: (0, 0)),
            pl.BlockSpec((d, n_out), lambda i: (0, 0)),
            pl.BlockSpec((tm, LANES), lambda i: (i % per_seq, 0)),
            pl.BlockSpec((tm, LANES), lambda i: (i % per_seq, 0)),
        ],
        out_specs=pl.BlockSpec((tm, n_out), lambda i: (i, 0)),
        out_shape=jax.ShapeDtypeStruct((m, n_out), BF16),
        compiler_params=_params("parallel"),
        name="even_proj",
    )(h2, g, w_in, cos, sin)


def _pair_scores(q, k):
    first = _lane_is_first_head(q.shape)
    zero = jnp.zeros_like(q)
    q2 = jnp.concatenate([jnp.where(first, q, zero), jnp.where(first, zero, q)], axis=0)
    return _dot_nt(q2, k)


def _pair_merge(x2, tq):
    first = _lane_is_first_head((tq, LANES))
    return jnp.where(first, x2[:tq], x2[tq:])


def _band_kernel(q_ref, k_ref, v_ref, o_ref, lse_ref, *, seq, half, tq, win):
    def body(i, carry):
        q0 = pl.multiple_of(i * tq, tq)
        ks = pl.multiple_of(jnp.clip(q0 - half, 0, seq - win), half)
        q = q_ref[0, pl.ds(q0, tq), :]
        k = k_ref[0, pl.ds(ks, win), :]
        v = v_ref[0, pl.ds(ks, win), :]
        s = _pair_scores(q, k)
        row = lax.broadcasted_iota(jnp.int32, s.shape, 0) % tq
        col = lax.broadcasted_iota(jnp.int32, s.shape, 1)
        rel = col - row + (ks - q0)
        s = jnp.where(jnp.abs(rel) <= half, s, NEG_INF)
        m = jnp.max(s, axis=-1, keepdims=True)
        p = jnp.exp(s - m)
        l = jnp.sum(p, axis=-1, keepdims=True)
        o2 = _dot(p.astype(BF16), v) / l
        lse2 = jnp.broadcast_to(m + jnp.log(l), o2.shape)
        o_ref[0, pl.ds(q0, tq), :] = _pair_merge(o2, tq).astype(o_ref.dtype)
        lse_ref[0, pl.ds(q0, tq), :] = _pair_merge(lse2, tq)
        return carry

    lax.fori_loop(0, seq // tq, body, 0)


def _dilated_branch(proj, batch, seq, dil, half):
    n_cols = proj.shape[-1]
    blocks_per_tok = n_cols // LANES
    length = seq // dil
    view = proj.reshape(batch, length, dil * n_cols)
    tq = 128
    win = tq + 2 * half
    n_pairs = DIL_HEADS // 2

    def spec(base):
        return pl.BlockSpec((1, length, LANES), lambda b, r, p: (b, 0, r * blocks_per_tok + base + p))

    out_spec = pl.BlockSpec((1, length, LANES), lambda b, r, p: (b, 0, r * n_pairs + p))
    o, lse = pl.pallas_call(
        functools.partial(_band_kernel, seq=length, half=half, tq=tq, win=win),
        grid=(batch, dil, n_pairs),
        in_specs=[spec(0), spec(4), spec(8)],
        out_specs=[out_spec, out_spec],
        out_shape=[jax.ShapeDtypeStruct((batch, length, dil * n_pairs * LANES), BF16),
                   jax.ShapeDtypeStruct((batch, length, dil * n_pairs * LANES), F32)],
        compiler_params=_params("parallel", "parallel", "parallel"),
        name=f"dilated_{dil}",
    )(view, view, view)
    width = n_pairs * LANES
    return o.reshape(batch * seq, width), lse.reshape(batch * seq, width)


def _na_kernel(q_ref, k_ref, v_ref, bias_ref, o_ref, *, rows, kr):
    def body(r, carry):
        rs = jnp.clip(r - kr // 2, 0, rows - kr)
        off = rs - r + (NA_ROWS - 1)
        q0 = pl.multiple_of(r * GRID_W, GRID_W)
        k0 = pl.multiple_of(rs * GRID_W, GRID_W)
        q = q_ref[0, pl.ds(q0, GRID_W), :]
        k = k_ref[0, pl.ds(k0, kr * GRID_W), :]
        v = v_ref[0, pl.ds(k0, kr * GRID_W), :]
        bias = jnp.concatenate([bias_ref[0, off], bias_ref[1, off]], axis=0)
        s = _pair_scores(q, k) + bias
        m = jnp.max(s, axis=-1, keepdims=True)
        p = jnp.exp(s - m)
        l = jnp.sum(p, axis=-1, keepdims=True)
        o2 = _dot(p.astype(BF16), v) / l
        o_ref[0, pl.ds(q0, GRID_W), :] = _pair_merge(o2, GRID_W).astype(o_ref.dtype)
        return carry

    lax.fori_loop(0, rows, body, 0)


def _na_bias_table(rpb, kr):
    qcol = np.arange(GRID_W)
    kcol = np.arange(GRID_W)
    wstart = np.clip(qcol - NA_COLS // 2, 0, GRID_W - NA_COLS)
    valid = (kcol[None, :] >= wstart[:, None]) & (kcol[None, :] < wstart[:, None] + NA_COLS)
    dc = np.clip(kcol[None, :] - qcol[:, None] + NA_COLS - 1, 0, 2 * NA_COLS - 2)
    full = jnp.where(valid[None, None], rpb[:, :, dc], NEG_INF)
    n_off = 2 * NA_ROWS - kr
    slabs = [jnp.concatenate([full[:, off + j] for j in range(kr)], axis=-1) for off in range(n_off)]
    return jnp.stack(slabs, axis=1)


def _neighbourhood(proj, rpb, batch, seq):
    rows = seq // GRID_W
    kr = min(NA_ROWS, rows)
    table = _na_bias_table(rpb.astype(F32), kr)
    n_off = table.shape[1]
    n_pairs = NA_HEADS // 2

    def spec(base):
        return pl.BlockSpec((1, seq, LANES), lambda b, p: (b, 0, base + p))

    return pl.pallas_call(
        functools.partial(_na_kernel, rows=rows, kr=kr),
        grid=(batch, n_pairs),
        in_specs=[spec(12), spec(16), spec(20),
                  pl.BlockSpec((2, n_off, GRID_W, kr * GRID_W), lambda b, p: (p, 0, 0, 0))],
        out_specs=pl.BlockSpec((1, seq, LANES), lambda b, p: (b, 0, p)),
        out_shape=jax.ShapeDtypeStruct((batch, seq, n_pairs * LANES), BF16),
        compiler_params=_params("parallel", "parallel"),
        name="neighbourhood",
    )(proj, proj, proj, table)


def _even_out_kernel(h_ref, o1_ref, o2_ref, o3_ref, l1_ref, l2_ref, l3_ref, on_ref, wa_ref, wb_ref, out_ref):
    l1, l2, l3 = l1_ref[...], l2_ref[...], l3_ref[...]
    mx = jnp.maximum(jnp.maximum(l1, l2), l3)
    e1, e2, e3 = jnp.exp(l1 - mx), jnp.exp(l2 - mx), jnp.exp(l3 - mx)
    num = e1 * o1_ref[...].astype(F32) + e2 * o2_ref[...].astype(F32) + e3 * o3_ref[...].astype(F32)
    oa = (num / (e1 + e2 + e3)).astype(BF16)
    out_ref[...] = h_ref[...] + _dot(oa, wa_ref[...]) + _dot(on_ref[...], wb_ref[...])


def _even_out(h2, os, lses, on, w_out):
    m, d = h2.shape
    half_w = w_out.shape[0] // 2
    tm = FFN_TM
    tile = lambda w: pl.BlockSpec((tm, w), lambda i: (i, 0))
    full = lambda r, c: pl.BlockSpec((r, c), lambda i: (0, 0))
    return pl.pallas_call(
        _even_out_kernel,
        grid=(m // tm,),
        in_specs=[tile(d)] + [tile(half_w)] * 7 + [full(half_w, d), full(half_w, d)],
        out_specs=tile(d),
        out_shape=jax.ShapeDtypeStruct((m, d), F32),
        compiler_params=_params("parallel"),
        name="even_out",
    )(h2, *os, *lses, on, w_out[:half_w], w_out[half_w:])


_ODD_W_COLS = 14 * LANES
_ODD_OUT_COLS = 28 * LANES


def _odd_proj_kernel(x_ref, g_ref, w_ref, qn_ref, wqb_ref, kvn_ref, wkb_ref, wvb_ref,
                     cos_ref, sin_ref, cosr_ref, sinr_ref, o_ref):
    hn = _rms(x_ref[...], g_ref[...]).astype(BF16)
    cos, sin = cos_ref[...], sin_ref[...]
    cosr, sinr = cosr_ref[...], sinr_ref[...]
    lane = lax.broadcasted_iota(jnp.int32, cos.shape, 1)
    first_half = (lane % HEAD_DIM) < HEAD_DIM // 2
    first_half_r = lane < MLA_NOPE + MLA_ROPE // 2
    swa_scale = HEAD_DIM ** -0.5
    mla_scale = (MLA_NOPE + MLA_ROPE) ** -0.5

    def col(j, n=1):
        return slice(j * LANES, (j + n) * LANES)

    def rope(blk):
        return _rope_block(blk, cos, sin, first_half, LANES - HEAD_DIM // 2, HEAD_DIM // 2)

    def rope_r(blk):
        return _rope_block(blk, cosr, sinr, first_half_r, LANES - MLA_ROPE // 2, MLA_ROPE // 2)

    for j in range(4):
        o_ref[:, col(j)] = (rope(_dot(hn, w_ref[:, col(j)])) * swa_scale).astype(BF16)
    for j in range(4, 6):
        o_ref[:, col(j)] = rope(_dot(hn, w_ref[:, col(j)])).astype(BF16)
    for j in range(6, 8):
        o_ref[:, col(j)] = _dot(hn, w_ref[:, col(j)]).astype(BF16)

    q_a = _dot(hn, w_ref[:, col(8, 3)])
    q_an = _rms(q_a, qn_ref[...]).astype(BF16)
    for h in range(MLA_HEADS):
        qh = rope_r(_dot(q_an, wqb_ref[:, col(h)])) * mla_scale
        o_ref[:, col(8 + h)] = qh.astype(BF16)

    kv_a = _dot(hn, w_ref[:, col(11, 2)])
    kv_an = _rms(kv_a, kvn_ref[...]).astype(BF16)
    k_pe = rope_r(_dot(hn, w_ref[:, col(13)]))
    for h in range(MLA_HEADS):
        kh = _dot(kv_an, wkb_ref[:, col(h)]) + k_pe
        o_ref[:, col(16 + h)] = kh.astype(BF16)
    for j in range(MLA_HEADS // 2):
        o_ref[:, col(24 + j)] = _dot(kv_an, wvb_ref[:, col(j)]).astype(BF16)


def _odd_proj(h2, g, w, qn, wqb, kvn, wkb, wvb, cos, sin, cosr, sinr, seq):
    m, d = h2.shape
    tm = FFN_TM
    per_seq = seq // tm
    full = lambda a: pl.BlockSpec(a.shape, lambda i: (0, 0))
    tab = pl.BlockSpec((tm, LANES), lambda i: (i % per_seq, 0))
    return pl.pallas_call(
        _odd_proj_kernel,
        grid=(m // tm,),
        in_specs=[pl.BlockSpec((tm, d), lambda i: (i, 0)), full(g), full(w), full(qn), full(wqb),
                  full(kvn), full(wkb), full(wvb), tab, tab, tab, tab],
        out_specs=pl.BlockSpec((tm, _ODD_OUT_COLS), lambda i: (i, 0)),
        out_shape=jax.ShapeDtypeStruct((m, _ODD_OUT_COLS), BF16),
        compiler_params=_params("parallel"),
        name="odd_proj",
    )(h2, g, w, qn, wqb, kvn, wkb, wvb, cos, sin, cosr, sinr)


def _odd_weights(w_in, w_qb, w_kvb):
    d = w_in.shape[0]
    c = np.cumsum([0, SWA_Q_HEADS * HEAD_DIM, SWA_KV_HEADS * HEAD_DIM, SWA_KV_HEADS * HEAD_DIM,
                   MLA_Q_RANK, MLA_KV_RANK, MLA_ROPE])

    def dup(w):
        w = w.reshape(d, SWA_KV_HEADS, 1, HEAD_DIM)
        return jnp.broadcast_to(w, (d, SWA_KV_HEADS, 2, HEAD_DIM)).reshape(d, SWA_KV_HEADS * LANES)

    w_kpe = jnp.pad(w_in[:, c[5]:c[6]], ((0, 0), (MLA_NOPE, LANES - MLA_NOPE - MLA_ROPE)))
    w = jnp.concatenate([w_in[:, c[0]:c[1]], dup(w_in[:, c[1]:c[2]]), dup(w_in[:, c[2]:c[3]]),
                         w_in[:, c[3]:c[4]], w_in[:, c[4]:c[5]], w_kpe], axis=1).astype(BF16)
    qk = MLA_NOPE + MLA_ROPE
    wqb = jnp.pad(w_qb.reshape(MLA_Q_RANK, MLA_HEADS, qk), ((0, 0), (0, 0), (0, LANES - qk)))
    wqb = wqb.reshape(MLA_Q_RANK, MLA_HEADS * LANES).astype(BF16)
    kv = w_kvb.reshape(MLA_KV_RANK, MLA_HEADS, MLA_NOPE + MLA_V)
    wkb = jnp.pad(kv[:, :, :MLA_NOPE], ((0, 0), (0, 0), (0, LANES - MLA_NOPE)))
    wkb = wkb.reshape(MLA_KV_RANK, MLA_HEADS * LANES).astype(BF16)
    wvb = kv[:, :, MLA_NOPE:].reshape(MLA_KV_RANK, MLA_HEADS * MLA_V).astype(BF16)
    return w, wqb, wkb, wvb


def _swa_kernel(sink_ref, q_ref, k_ref, v_ref, o_ref, *, seq, half, tq, win):
    pair = pl.program_id(1)
    sink_a = sink_ref[2 * pair]
    sink_b = sink_ref[2 * pair + 1]

    def body(i, carry):
        q0 = pl.multiple_of(i * tq, tq)
        ks = pl.multiple_of(jnp.clip(q0 - half, 0, seq - win), half)
        q = q_ref[0, pl.ds(q0, tq), :]
        k = k_ref[0, pl.ds(ks, win), :]
        v = v_ref[0, pl.ds(ks, win), :]
        s = _pair_scores(q, k)
        row = lax.broadcasted_iota(jnp.int32, s.shape, 0)
        col = lax.broadcasted_iota(jnp.int32, s.shape, 1)
        rel = col - row % tq + (ks - q0)
        s = jnp.where(jnp.abs(rel) <= half, s, NEG_INF)
        sink = jnp.where(row[:, :1] < tq, sink_a, sink_b)
        m = jnp.maximum(jnp.max(s, axis=-1, keepdims=True), sink)
        p = jnp.exp(s - m)
        l = jnp.sum(p, axis=-1, keepdims=True) + jnp.exp(sink - m)
        o2 = _dot(p.astype(BF16), v) / l
        o_ref[0, pl.ds(q0, tq), :] = _pair_merge(o2, tq).astype(o_ref.dtype)
        return carry

    lax.fori_loop(0, seq // tq, body, 0)


def _swa(proj, sink, batch, seq):
    tq = 128
    win = tq + 2 * SWA_HALF
    n_pairs = SWA_Q_HEADS // 2
    pairs_per_kv = n_pairs // SWA_KV_HEADS
    return pl.pallas_call(
        functools.partial(_swa_kernel, seq=seq, half=SWA_HALF, tq=tq, win=win),
        grid=(batch, n_pairs),
        in_specs=[pl.BlockSpec(memory_space=pltpu.SMEM),
                  pl.BlockSpec((1, seq, LANES), lambda b, p: (b, 0, p)),
                  pl.BlockSpec((1, seq, LANES), lambda b, p: (b, 0, 4 + p // pairs_per_kv)),
                  pl.BlockSpec((1, seq, LANES), lambda b, p: (b, 0, 6 + p // pairs_per_kv))],
        out_specs=pl.BlockSpec((1, seq, LANES), lambda b, p: (b, 0, p)),
        out_shape=jax.ShapeDtypeStruct((batch, seq, n_pairs * LANES), BF16),
        compiler_params=_params("parallel", "parallel"),
        name="swa",
    )(sink, proj, proj, proj)


def _mla_kernel(q_ref, k_ref, v_ref, o_ref, *, seq, tk):
    tq = q_ref.shape[1]
    outs = []
    for h in range(2):
        q = q_ref[0, :, h * LANES:(h + 1) * LANES]

        def body(c, carry, h=h, q=q):
            m, l, acc = carry
            k0 = pl.multiple_of(c * tk, tk)
            k = k_ref[0, pl.ds(k0, tk), h * LANES:(h + 1) * LANES]
            v = v_ref[0, pl.ds(k0, tk), :]
            s = _dot_nt(q, k)
            m_new = jnp.maximum(m, jnp.max(s, axis=-1, keepdims=True))
            alpha = jnp.exp(m - m_new)
            p = jnp.exp(s - m_new)
            l = alpha * l + jnp.sum(p, axis=-1, keepdims=True)
            acc = alpha * acc + _dot(p.astype(BF16), v)
            return m_new, l, acc

        init = (jnp.full((tq, 1), NEG_INF, F32), jnp.zeros((tq, 1), F32), jnp.zeros((tq, LANES), F32))
        _, l, acc = lax.fori_loop(0, seq // tk, body, init)
        outs.append(acc / l)
    first = _lane_is_first_head((tq, LANES))
    o_ref[0] = jnp.where(first, outs[0], outs[1]).astype(o_ref.dtype)


def _mla(proj, batch, seq):
    tq = MLA_TQ
    n_pairs = MLA_HEADS // 2
    return pl.pallas_call(
        functools.partial(_mla_kernel, seq=seq, tk=MLA_TK),
        grid=(batch, n_pairs, seq // tq),
        in_specs=[pl.BlockSpec((1, tq, 2 * LANES), lambda b, p, i: (b, i, 4 + p)),
                  pl.BlockSpec((1, seq, 2 * LANES), lambda b, p, i: (b, 0, 8 + p)),
                  pl.BlockSpec((1, seq, LANES), lambda b, p, i: (b, 0, 24 + p))],
        out_specs=pl.BlockSpec((1, tq, LANES), lambda b, p, i: (b, i, p)),
        out_shape=jax.ShapeDtypeStruct((batch, seq, n_pairs * LANES), BF16),
        compiler_params=_params("parallel", "parallel", "parallel"),
        name="mla",
    )(proj, proj, proj)


def _odd_out_kernel(h_ref, oc_ref, od_ref, wa_ref, wb_ref, out_ref):
    out_ref[...] = h_ref[...] + _dot(oc_ref[...], wa_ref[...]) + _dot(od_ref[...], wb_ref[...])


def _odd_out(h2, oc, od, w_out):
    m, d = h2.shape
    half_w = w_out.shape[0] // 2
    tm = FFN_TM
    tile = lambda w: pl.BlockSpec((tm, w), lambda i: (i, 0))
    full = lambda r, c: pl.BlockSpec((r, c), lambda i: (0, 0))
    return pl.pallas_call(
        _odd_out_kernel,
        grid=(m // tm,),
        in_specs=[tile(d), tile(half_w), tile(half_w), full(half_w, d), full(half_w, d)],
        out_specs=tile(d),
        out_shape=jax.ShapeDtypeStruct((m, d), F32),
        compiler_params=_params("parallel"),
        name="odd_out",
    )(h2, oc, od, w_out[:half_w], w_out[half_w:])


def _rope_tables(seq):
    pos = jnp.arange(seq, dtype=F32)

    def tables(dim):
        inv_freq = ROPE_THETA ** (-jnp.arange(0, dim, 2, dtype=F32) / dim)
        ang = pos[:, None] * inv_freq[None, :]
        return jnp.cos(ang), jnp.sin(ang)

    cos, sin = tables(HEAD_DIM)
    reps = LANES // HEAD_DIM
    cos_t = jnp.tile(jnp.concatenate([cos, cos], axis=1), (1, reps))
    sin_t = jnp.tile(jnp.concatenate([-sin, sin], axis=1), (1, reps))
    cos_r, sin_r = tables(MLA_ROPE)
    pad_l, pad_r = MLA_NOPE, LANES - MLA_NOPE - MLA_ROPE
    cos_rt = jnp.pad(jnp.concatenate([cos_r, cos_r], axis=1), ((0, 0), (pad_l, pad_r)), constant_values=1.0)
    sin_rt = jnp.pad(jnp.concatenate([-sin_r, sin_r], axis=1), ((0, 0), (pad_l, pad_r)))
    return cos_t, sin_t, cos_rt, sin_rt


def kernel(x, ffn1_norm, ffn1_w1, ffn1_w3, ffn1_w2, mix_norm, ffn2_norm, ffn2_w1, ffn2_w3, ffn2_w2, even_w_in, even_w_out, na_rel_bias, odd_w_in, odd_w_out, swa_sink, mla_q_norm, mla_w_qb, mla_kv_norm, mla_w_kvb, final_norm):
    batch, seq, d = x.shape
    depth = ffn1_norm.shape[0]
    assert d == D_MODEL and seq % FFN_TM == 0 and seq % (GRID_W * NA_ROWS) == 0
    cos_t, sin_t, cos_rt, sin_rt = _rope_tables(seq)
    h = x.reshape(batch * seq, d)
    row = lambda v: v.reshape(1, -1).astype(F32)
    fg = row(final_norm)

    for i in range(depth):
        j = i // 2
        h = _ffn(h, row(ffn1_norm[i]), *_ffn_weights(ffn1_w1[i], ffn1_w3[i], ffn1_w2[i]), fg, False)
        if i % 2 == 0:
            proj = _even_proj(h, row(mix_norm[i]), even_w_in[j].astype(BF16), cos_t, sin_t, seq)
            proj = proj.reshape(batch, seq, -1)
            os, lses = [], []
            for window, dil in DIL_CONFIGS:
                o, lse = _dilated_branch(proj, batch, seq, dil, window // 2 // dil)
                os.append(o)
                lses.append(lse)
            on = _neighbourhood(proj, na_rel_bias[j], batch, seq).reshape(batch * seq, -1)
            h = _even_out(h, os, lses, on, even_w_out[j].astype(BF16))
        else:
            w, wqb, wkb, wvb = _odd_weights(odd_w_in[j], mla_w_qb[j], mla_w_kvb[j])
            proj = _odd_proj(h, row(mix_norm[i]), w, row(mla_q_norm[j]), wqb, row(mla_kv_norm[j]), wkb, wvb,
                             cos_t, sin_t, cos_rt, sin_rt, seq)
            proj = proj.reshape(batch, seq, -1)
            oc = _swa(proj, swa_sink[j].astype(F32), batch, seq).reshape(batch * seq, -1)
            od = _mla(proj, batch, seq).reshape(batch * seq, -1)
            h = _odd_out(h, oc, od, odd_w_out[j].astype(BF16))
        h = _ffn(h, row(ffn2_norm[i]), *_ffn_weights(ffn2_w1[i], ffn2_w3[i], ffn2_w2[i]), fg, i == depth - 1)
    return h.reshape(batch, seq, d)
```

```python
import functools

import jax
import jax.numpy as jnp
import numpy as np
from jax import lax
from jax.experimental import pallas as pl
from jax.experimental.pallas import tpu as pltpu

D_MODEL = 1024
D_FF = 2816
HEAD_DIM = 64
ROPE_THETA = 10000.0
NORM_EPS = 1e-6
NEG_INF = -1e30
LOG2_E = 1.4426950408889634

DIL_HEADS = 8
DIL_CONFIGS = ((128, 1), (512, 4), (2048, 16))
NA_HEADS = 8
GRID_W = 64
NA_ROWS = 8
NA_COLS = 16

SWA_Q_HEADS = 8
SWA_KV_HEADS = 2
SWA_HALF = 128
MLA_HEADS = 8
MLA_Q_RANK = 384
MLA_KV_RANK = 256
MLA_NOPE = 64
MLA_ROPE = 32
MLA_V = 64

LANES = 128
VMEM_LIMIT = 56 * 1024 * 1024

FFN_TM = 512
FFN_TF = 256
MLA_TQ = 256
MLA_TK = 512
DIL_TQ = 256
DIL_UNROLL = 2
SWA_TQ = 256
SWA_UNROLL = 2
NA_GROUP = 4
NA_UNROLL = 2

BF16 = jnp.bfloat16
F32 = jnp.float32


def _params(*sem):
    return pltpu.CompilerParams(dimension_semantics=sem, vmem_limit_bytes=VMEM_LIMIT)


def _rms(x, g):
    ms = jnp.mean(x * x, axis=-1, keepdims=True)
    return x * lax.rsqrt(ms + NORM_EPS) * g


def _dot(a, b):
    return jnp.dot(a, b, preferred_element_type=F32)


def _dot_nt(a, b):
    return lax.dot_general(a, b, (((1,), (1,)), ((), ())), preferred_element_type=F32)


def _lane_is_first_head(shape):
    return lax.broadcasted_iota(jnp.int32, shape, len(shape) - 1) < HEAD_DIM


def _rope_block(x, cos, sin_signed, first_half, up, down):
    partner = jnp.where(first_half, pltpu.roll(x, up, 1), pltpu.roll(x, down, 1))
    return x * cos + partner * sin_signed


def _ffn_kernel(x_ref, g_ref, w1_ref, w3_ref, w2_ref, fg_ref, o_ref, acc_ref, *, n_chunks, final):
    x = x_ref[...]
    hn = _rms(x, g_ref[...]).astype(BF16)
    acc_ref[...] = jnp.zeros_like(acc_ref)

    def body(c, carry):
        h1 = _dot(hn, w1_ref[c])
        h3 = _dot(hn, w3_ref[c])
        a = (h1 * jax.nn.sigmoid(h1) * h3).astype(BF16)
        acc_ref[...] += _dot(a, w2_ref[c])
        return carry

    lax.fori_loop(0, n_chunks, body, 0)
    y = x + 0.5 * acc_ref[...]
    if final:
        y = _rms(y, fg_ref[...])
    o_ref[...] = y


def _ffn(x2, g, w1c, w3c, w2c, fg, final):
    m, d = x2.shape
    n_chunks, _, tf = w1c.shape
    tm = FFN_TM
    const3 = lambda i: (0, 0, 0)
    return pl.pallas_call(
        functools.partial(_ffn_kernel, n_chunks=n_chunks, final=final),
        grid=(m // tm,),
        in_specs=[
            pl.BlockSpec((tm, d), lambda i: (i, 0)),
            pl.BlockSpec((1, d), lambda i: (0, 0)),
            pl.BlockSpec((n_chunks, d, tf), const3),
            pl.BlockSpec((n_chunks, d, tf), const3),
            pl.BlockSpec((n_chunks, tf, d), const3),
            pl.BlockSpec((1, d), lambda i: (0, 0)),
        ],
        out_specs=pl.BlockSpec((tm, d), lambda i: (i, 0)),
        out_shape=jax.ShapeDtypeStruct((m, d), F32),
        scratch_shapes=[pltpu.VMEM((tm, d), F32)],
        compiler_params=_params("parallel"),
        name="ffn",
    )(x2, g, w1c, w3c, w2c, fg)


def _ffn_weights(w1, w3, w2):
    d, f = w1.shape
    n = f // FFN_TF
    w1c = w1.astype(BF16).reshape(d, n, FFN_TF).transpose(1, 0, 2)
    w3c = w3.astype(BF16).reshape(d, n, FFN_TF).transpose(1, 0, 2)
    w2c = w2.astype(BF16).reshape(n, FFN_TF, d)
    return w1c, w3c, w2c


_EVEN_DIL_BLOCKS = 12
_EVEN_Q_BLOCKS = tuple(range(0, 4)) + tuple(range(12, 16))
_EVEN_ROPE_BLOCKS = 8


def _even_proj_kernel(x_ref, g_ref, w_ref, cos_ref, sin_ref, dil_ref, na_ref, *, n_blocks):
    hn = _rms(x_ref[...], g_ref[...]).astype(BF16)
    cos = cos_ref[...]
    sin = sin_ref[...]
    lane = lax.broadcasted_iota(jnp.int32, cos.shape, 1)
    first_half = (lane % HEAD_DIM) < HEAD_DIM // 2
    scale = HEAD_DIM ** -0.5 * LOG2_E
    for j2 in range(0, n_blocks, 2):
        wide = _dot(hn, w_ref[:, j2 * LANES:(j2 + 2) * LANES])
        for j in (j2, j2 + 1):
            blk = wide[:, (j - j2) * LANES:(j - j2 + 1) * LANES]
            if j < _EVEN_ROPE_BLOCKS:
                blk = _rope_block(blk, cos, sin, first_half, LANES - HEAD_DIM // 2, HEAD_DIM // 2)
            if j in _EVEN_Q_BLOCKS:
                blk = blk * scale
            if j < _EVEN_DIL_BLOCKS:
                dil_ref[:, j * LANES:(j + 1) * LANES] = blk
            else:
                jn = j - _EVEN_DIL_BLOCKS
                na_ref[:, jn * LANES:(jn + 1) * LANES] = blk.astype(BF16)


def _even_proj(h2, g, w_in, cos, sin, seq):
    m, d = h2.shape
    n_out = w_in.shape[1]
    n_dil = _EVEN_DIL_BLOCKS * LANES
    tm = FFN_TM
    per_seq = seq // tm
    return pl.pallas_call(
        functools.partial(_even_proj_kernel, n_blocks=n_out // LANES),
        grid=(m // tm,),
        in_specs=[
            pl.BlockSpec((tm, d), lambda i: (i, 0)),
            pl.BlockSpec((1, d), lambda i: (0, 0)),
            pl.BlockSpec((d, n_out), lambda i: (0, 0)),
            pl.BlockSpec((tm, LANES), lambda i: (i % per_seq, 0)),
            pl.BlockSpec((tm, LANES), lambda i: (i % per_seq, 0)),
        ],
        out_specs=[pl.BlockSpec((tm, n_dil), lambda i: (i, 0)),
                   pl.BlockSpec((tm, n_out - n_dil), lambda i: (i, 0))],
        out_shape=[jax.ShapeDtypeStruct((m, n_dil), F32),
                   jax.ShapeDtypeStruct((m, n_out - n_dil), BF16)],
        compiler_params=_params("parallel"),
        name="even_proj",
    )(h2, g, w_in, cos, sin)


def _pair_scores(q, k):
    first = _lane_is_first_head(q.shape)
    zero = jnp.zeros_like(q)
    q2 = jnp.concatenate([jnp.where(first, q, zero), jnp.where(first, zero, q)], axis=0)
    return _dot_nt(q2, k)


def _pair_merge(x2, tq):
    first = _lane_is_first_head((tq, LANES))
    return jnp.where(first, x2[:tq], x2[tq:])


def _band_bias(tq, half, win):
    row = np.arange(2 * tq)[:, None] % tq
    col = np.arange(win)[None, :]
    kinds = [np.where(np.abs(col - row - k * half) <= half, 0.0, NEG_INF) for k in range(3)]
    return jnp.asarray(np.stack(kinds), F32)


def _band_window(q0, tq, half, length, win):
    ks = pl.multiple_of(jnp.clip(q0 - half, 0, length - win), half)
    return ks, (q0 - ks) // half


def _pair_softmax(q, k, bias):
    s = _pair_scores(q, k) + bias
    m = jnp.max(s, axis=-1, keepdims=True)
    p = jnp.exp2(s - m)
    return p, m, jnp.sum(p, axis=-1, keepdims=True)


def _dilated_kernel(q_ref, k_ref, v_ref, *refs, seq, configs, tiles):
    n = len(configs)
    bias_refs, o_ref = refs[:n], refs[n]
    qs_ref, ks_ref, vs_ref = refs[n + 1:n + 4]
    ob_refs, lb_refs = refs[n + 4:2 * n + 4], refs[2 * n + 4:]
    for c, ((half, dil), (tq, win)) in enumerate(zip(configs, tiles)):
        length = seq // dil
        blocks = length // tq
        for src, dst in ((q_ref, qs_ref), (k_ref, ks_ref), (v_ref, vs_ref)):
            for r in range(dil):
                rows = pl.ds(r, length, stride=dil) if dil > 1 else pl.ds(0, length)
                dst[r * length:(r + 1) * length, :] = src[rows, :].astype(BF16)

        def body(i, carry, c=c, half=half, dil=dil, tq=tq, win=win, length=length, blocks=blocks):
            r = i // blocks
            q0 = pl.multiple_of((i % blocks) * tq, tq)
            ks, kind = _band_window(q0, tq, half, length, win)
            base = pl.multiple_of(r * length, tq)
            q = qs_ref[pl.ds(base + q0, tq), :]
            k = ks_ref[pl.ds(base + ks, win), :]
            v = vs_ref[pl.ds(base + ks, win), :]
            p, m, l = _pair_softmax(q, k, bias_refs[c][kind])
            o2 = _dot(p.astype(BF16), v) / l
            lse2 = jnp.broadcast_to(m + jnp.log2(l), o2.shape)
            rows = pl.ds(r + dil * q0, tq, stride=dil) if dil > 1 else pl.ds(q0, tq)
            ob_refs[c][rows, :] = _pair_merge(o2, tq)
            lb_refs[c][rows, :] = _pair_merge(lse2, tq)
            return carry

        lax.fori_loop(0, seq // tq, body, 0, unroll=DIL_UNROLL)

    chunk = 512

    def merge(i, carry):
        rows = pl.ds(pl.multiple_of(i * chunk, chunk), chunk)
        lses = [lb[rows, :] for lb in lb_refs]
        mx = functools.reduce(jnp.maximum, lses)
        es = [jnp.exp2(l - mx) for l in lses]
        num = sum(e * ob[rows, :] for e, ob in zip(es, ob_refs))
        o_ref[rows, :] = (num / sum(es)).astype(o_ref.dtype)
        return carry

    lax.fori_loop(0, seq // chunk, merge, 0)


def _dilated(dil_proj, batch, seq):
    n_pairs = DIL_HEADS // 2
    configs = tuple((window // 2 // dil, dil) for window, dil in DIL_CONFIGS)
    tiles = []
    for half, dil in configs:
        tq = min(DIL_TQ, seq // dil)
        tiles.append((tq, min(tq + 2 * half, seq // dil)))
        assert seq % (dil * tq) == 0 and tq % half == 0
    biases = [_band_bias(tq, half, win) for (half, _), (tq, win) in zip(configs, tiles)]

    def spec(base):
        return pl.BlockSpec((None, seq, LANES), lambda b, p: (b, 0, base + p))

    return pl.pallas_call(
        functools.partial(_dilated_kernel, seq=seq, configs=configs, tiles=tuple(tiles)),
        grid=(batch, n_pairs),
        in_specs=[spec(0), spec(4), spec(8)] + [pl.BlockSpec(b.shape, lambda b_, p: (0, 0, 0)) for b in biases],
        out_specs=pl.BlockSpec((None, seq, LANES), lambda b, p: (b, 0, p)),
        out_shape=jax.ShapeDtypeStruct((batch, seq, n_pairs * LANES), BF16),
        scratch_shapes=[pltpu.VMEM((seq, LANES), BF16)] * 3 + [pltpu.VMEM((seq, LANES), F32)] * (2 * len(configs)),
        compiler_params=_params("parallel", "parallel"),
        name="dilated",
    )(dil_proj, dil_proj, dil_proj, *biases)


def _na_group_geometry(rows, kr):
    n_groups = rows // NA_GROUP
    wr = kr + NA_GROUP
    g = np.arange(n_groups)
    start = np.minimum(np.clip(g * NA_GROUP - kr // 2, 0, rows - kr), rows - wr)
    return n_groups, wr, start


def _na_kernel(q_ref, k_ref, v_ref, bias_ref, o_ref, *, rows, kr):
    n_groups, wr, _ = _na_group_geometry(rows, kr)
    tq = NA_GROUP * GRID_W

    def body(g, carry):
        start = jnp.minimum(jnp.clip(g * NA_GROUP - kr // 2, 0, rows - kr), rows - wr)
        kind = jnp.where(g == 0, 0, jnp.where(g == n_groups - 1, 2, 1))
        q0 = pl.multiple_of(g * tq, tq)
        k0 = pl.multiple_of(start * GRID_W, GRID_W)
        q = q_ref[pl.ds(q0, tq), :]
        k = k_ref[pl.ds(k0, wr * GRID_W), :]
        v = v_ref[pl.ds(k0, wr * GRID_W), :]
        p, _, l = _pair_softmax(q, k, bias_ref[kind])
        o2 = _dot(p.astype(BF16), v) / l
        o_ref[pl.ds(q0, tq), :] = _pair_merge(o2, tq).astype(o_ref.dtype)
        return carry

    lax.fori_loop(0, n_groups, body, 0, unroll=NA_UNROLL)


def _na_bias_table(rpb, rows, kr):
    n_groups, wr, start = _na_group_geometry(rows, kr)
    qcol = np.arange(GRID_W)
    kcol = np.arange(GRID_W)
    wstart = np.clip(qcol - NA_COLS // 2, 0, GRID_W - NA_COLS)
    col_ok = (kcol[None, :] >= wstart[:, None]) & (kcol[None, :] < wstart[:, None] + NA_COLS)
    dc = np.clip(kcol[None, :] - qcol[:, None] + NA_COLS - 1, 0, 2 * NA_COLS - 2)

    def geometry(g):
        qrow = g * NA_GROUP + np.arange(NA_GROUP)
        krow = start[g] + np.arange(wr)
        rs = np.clip(qrow - kr // 2, 0, rows - kr)
        row_ok = (krow[None, :] >= rs[:, None]) & (krow[None, :] < rs[:, None] + kr)
        dr = np.clip(krow[None, :] - qrow[:, None] + NA_ROWS - 1, 0, 2 * NA_ROWS - 2)
        return row_ok, dr

    kinds = [0, min(1, n_groups - 1), n_groups - 1]
    for g in range(1, n_groups - 1):
        assert all(np.array_equal(a, b) for a, b in zip(geometry(g), geometry(kinds[1])))
    slabs = []
    for g in kinds:
        row_ok, dr = geometry(g)
        ok = row_ok[:, None, :, None] & col_ok[None, :, None, :]
        vals = rpb[:, dr[:, None, :, None], dc[None, :, None, :]] * LOG2_E
        slab = jnp.where(ok[None], vals, NEG_INF)
        slabs.append(slab.reshape(NA_HEADS // 2, 2 * NA_GROUP * GRID_W, wr * GRID_W))
    return jnp.stack(slabs)


def _neighbourhood(proj, rpb, batch, seq):
    rows = seq // GRID_W
    kr = min(NA_ROWS, rows)
    assert rows % NA_GROUP == 0 and rows >= kr + NA_GROUP
    table = _na_bias_table(rpb.astype(F32), rows, kr)
    n_pairs = NA_HEADS // 2

    def spec(base):
        return pl.BlockSpec((None, seq, LANES), lambda b, p: (b, 0, base + p))

    return pl.pallas_call(
        functools.partial(_na_kernel, rows=rows, kr=kr),
        grid=(batch, n_pairs),
        in_specs=[spec(0), spec(4), spec(8),
                  pl.BlockSpec((3, None) + table.shape[2:], lambda b, p: (0, p, 0, 0))],
        out_specs=pl.BlockSpec((None, seq, LANES), lambda b, p: (b, 0, p)),
        out_shape=jax.ShapeDtypeStruct((batch, seq, n_pairs * LANES), BF16),
        compiler_params=_params("parallel", "parallel"),
        name="neighbourhood",
    )(proj, proj, proj, table)


_ODD_W_COLS = 14 * LANES
_ODD_OUT_COLS = 28 * LANES


def _odd_proj_kernel(x_ref, g_ref, w_ref, qn_ref, wqb_ref, kvn_ref, wkb_ref, wvb_ref,
                     cos_ref, sin_ref, cosr_ref, sinr_ref, o_ref):
    hn = _rms(x_ref[...], g_ref[...]).astype(BF16)
    cos, sin = cos_ref[...], sin_ref[...]
    cosr, sinr = cosr_ref[...], sinr_ref[...]
    lane = lax.broadcasted_iota(jnp.int32, cos.shape, 1)
    first_half = (lane % HEAD_DIM) < HEAD_DIM // 2
    first_half_r = lane < MLA_NOPE + MLA_ROPE // 2
    swa_scale = HEAD_DIM ** -0.5 * LOG2_E
    mla_scale = (MLA_NOPE + MLA_ROPE) ** -0.5 * LOG2_E

    def col(j, n=1):
        return slice(j * LANES, (j + n) * LANES)

    def rope(blk):
        return _rope_block(blk, cos, sin, first_half, LANES - HEAD_DIM // 2, HEAD_DIM // 2)

    def rope_r(blk):
        return _rope_block(blk, cosr, sinr, first_half_r, LANES - MLA_ROPE // 2, MLA_ROPE // 2)

    def halves(wide):
        return wide[:, :LANES], wide[:, LANES:]

    for j in range(0, 8, 2):
        for jj, blk in zip((j, j + 1), halves(_dot(hn, w_ref[:, col(j, 2)]))):
            if jj < 4:
                blk = rope(blk) * swa_scale
            elif jj < 6:
                blk = rope(blk)
            o_ref[:, col(jj)] = blk.astype(BF16)

    latent = _dot(hn, w_ref[:, col(8, 6)])
    q_an = _rms(latent[:, :3 * LANES], qn_ref[...]).astype(BF16)
    kv_an = _rms(latent[:, 3 * LANES:5 * LANES], kvn_ref[...]).astype(BF16)
    k_pe = rope_r(latent[:, 5 * LANES:])
    for h in range(0, MLA_HEADS, 2):
        for hh, qh in zip((h, h + 1), halves(_dot(q_an, wqb_ref[:, col(h, 2)]))):
            o_ref[:, col(8 + hh)] = (rope_r(qh) * mla_scale).astype(BF16)
        for hh, kh in zip((h, h + 1), halves(_dot(kv_an, wkb_ref[:, col(h, 2)]))):
            o_ref[:, col(16 + hh)] = (kh + k_pe).astype(BF16)
    for j in range(0, MLA_HEADS // 2, 2):
        o_ref[:, col(24 + j, 2)] = _dot(kv_an, wvb_ref[:, col(j, 2)]).astype(BF16)


def _odd_proj(h2, g, w, qn, wqb, kvn, wkb, wvb, cos, sin, cosr, sinr, seq):
    m, d = h2.shape
    tm = FFN_TM
    per_seq = seq // tm
    full = lambda a: pl.BlockSpec(a.shape, lambda i: (0, 0))
    tab = pl.BlockSpec((tm, LANES), lambda i: (i % per_seq, 0))
    return pl.pallas_call(
        _odd_proj_kernel,
        grid=(m // tm,),
        in_specs=[pl.BlockSpec((tm, d), lambda i: (i, 0)), full(g), full(w), full(qn), full(wqb),
                  full(kvn), full(wkb), full(wvb), tab, tab, tab, tab],
        out_specs=pl.BlockSpec((tm, _ODD_OUT_COLS), lambda i: (i, 0)),
        out_shape=jax.ShapeDtypeStruct((m, _ODD_OUT_COLS), BF16),
        compiler_params=_params("parallel"),
        name="odd_proj",
    )(h2, g, w, qn, wqb, kvn, wkb, wvb, cos, sin, cosr, sinr)


def _odd_weights(w_in, w_qb, w_kvb):
    d = w_in.shape[0]
    c = np.cumsum([0, SWA_Q_HEADS * HEAD_DIM, SWA_KV_HEADS * HEAD_DIM, SWA_KV_HEADS * HEAD_DIM,
                   MLA_Q_RANK, MLA_KV_RANK, MLA_ROPE])

    def dup(w):
        w = w.reshape(d, SWA_KV_HEADS, 1, HEAD_DIM)
        return jnp.broadcast_to(w, (d, SWA_KV_HEADS, 2, HEAD_DIM)).reshape(d, SWA_KV_HEADS * LANES)

    w_kpe = jnp.pad(w_in[:, c[5]:c[6]], ((0, 0), (MLA_NOPE, LANES - MLA_NOPE - MLA_ROPE)))
    w = jnp.concatenate([w_in[:, c[0]:c[1]], dup(w_in[:, c[1]:c[2]]), dup(w_in[:, c[2]:c[3]]),
                         w_in[:, c[3]:c[4]], w_in[:, c[4]:c[5]], w_kpe], axis=1).astype(BF16)
    qk = MLA_NOPE + MLA_ROPE
    wqb = jnp.pad(w_qb.reshape(MLA_Q_RANK, MLA_HEADS, qk), ((0, 0), (0, 0), (0, LANES - qk)))
    wqb = wqb.reshape(MLA_Q_RANK, MLA_HEADS * LANES).astype(BF16)
    kv = w_kvb.reshape(MLA_KV_RANK, MLA_HEADS, MLA_NOPE + MLA_V)
    wkb = jnp.pad(kv[:, :, :MLA_NOPE], ((0, 0), (0, 0), (0, LANES - MLA_NOPE)))
    wkb = wkb.reshape(MLA_KV_RANK, MLA_HEADS * LANES).astype(BF16)
    wvb = kv[:, :, MLA_NOPE:].reshape(MLA_KV_RANK, MLA_HEADS * MLA_V).astype(BF16)
    return w, wqb, wkb, wvb


def _swa_kernel(sink_ref, q_ref, k_ref, v_ref, bias_ref, o_ref, *, seq, half, tq, win):
    pair = pl.program_id(1)
    first = lax.broadcasted_iota(jnp.int32, (2 * tq, 1), 0) < tq
    sink = jnp.where(first, sink_ref[2 * pair], sink_ref[2 * pair + 1]) * LOG2_E

    def body(i, carry):
        q0 = pl.multiple_of(i * tq, tq)
        ks, kind = _band_window(q0, tq, half, seq, win)
        q = q_ref[pl.ds(q0, tq), :]
        k = k_ref[pl.ds(ks, win), :]
        v = v_ref[pl.ds(ks, win), :]
        s = _pair_scores(q, k) + bias_ref[kind]
        m = jnp.maximum(jnp.max(s, axis=-1, keepdims=True), sink)
        p = jnp.exp2(s - m)
        l = jnp.sum(p, axis=-1, keepdims=True) + jnp.exp2(sink - m)
        o2 = _dot(p.astype(BF16), v) / l
        o_ref[pl.ds(q0, tq), :] = _pair_merge(o2, tq).astype(o_ref.dtype)
        return carry

    lax.fori_loop(0, seq // tq, body, 0, unroll=SWA_UNROLL)


def _swa(proj, sink, batch, seq):
    tq = SWA_TQ
    win = tq + 2 * SWA_HALF
    assert seq % tq == 0 and tq % SWA_HALF == 0 and seq >= win
    n_pairs = SWA_Q_HEADS // 2
    pairs_per_kv = n_pairs // SWA_KV_HEADS
    bias = _band_bias(tq, SWA_HALF, win)
    return pl.pallas_call(
        functools.partial(_swa_kernel, seq=seq, half=SWA_HALF, tq=tq, win=win),
        grid=(batch, n_pairs),
        in_specs=[pl.BlockSpec(memory_space=pltpu.SMEM),
                  pl.BlockSpec((None, seq, LANES), lambda b, p: (b, 0, p)),
                  pl.BlockSpec((None, seq, LANES), lambda b, p: (b, 0, 4 + p // pairs_per_kv)),
                  pl.BlockSpec((None, seq, LANES), lambda b, p: (b, 0, 6 + p // pairs_per_kv)),
                  pl.BlockSpec(bias.shape, lambda b, p: (0, 0, 0))],
        out_specs=pl.BlockSpec((None, seq, LANES), lambda b, p: (b, 0, p)),
        out_shape=jax.ShapeDtypeStruct((batch, seq, n_pairs * LANES), BF16),
        compiler_params=_params("parallel", "parallel"),
        name="swa",
    )(sink, proj, proj, proj, bias)


def _mla_kernel(q_ref, k_ref, v_ref, o_ref, s_ref, *, seq, tk):
    tq = q_ref.shape[1]
    groups = tk // LANES
    outs = []
    for h in range(2):
        q = q_ref[0, :, h * LANES:(h + 1) * LANES]
        m_l = jnp.full((tq, LANES), NEG_INF, F32)
        for c in range(seq // tk):
            s = _dot_nt(q, k_ref[0, c * tk:(c + 1) * tk, h * LANES:(h + 1) * LANES])
            s_ref[h, :, c * tk:(c + 1) * tk] = s
            for g in range(groups):
                m_l = jnp.maximum(m_l, s[:, g * LANES:(g + 1) * LANES])
        m = jnp.max(m_l, axis=-1, keepdims=True)
        l_l = jnp.zeros((tq, LANES), F32)
        acc = jnp.zeros((tq, LANES), F32)
        for c in range(seq // tk):
            p = jnp.exp2(s_ref[h, :, c * tk:(c + 1) * tk] - m)
            for g in range(groups):
                l_l = l_l + p[:, g * LANES:(g + 1) * LANES]
            acc = acc + _dot(p.astype(BF16), v_ref[0, c * tk:(c + 1) * tk, :])
        outs.append(acc / jnp.sum(l_l, axis=-1, keepdims=True))
    first = _lane_is_first_head((tq, LANES))
    o_ref[0] = jnp.where(first, outs[0], outs[1]).astype(o_ref.dtype)


def _mla(proj, batch, seq):
    tq = MLA_TQ
    n_pairs = MLA_HEADS // 2
    return pl.pallas_call(
        functools.partial(_mla_kernel, seq=seq, tk=MLA_TK),
        grid=(batch, n_pairs, seq // tq),
        in_specs=[pl.BlockSpec((1, tq, 2 * LANES), lambda b, p, i: (b, i, 4 + p)),
                  pl.BlockSpec((1, seq, 2 * LANES), lambda b, p, i: (b, 0, 8 + p)),
                  pl.BlockSpec((1, seq, LANES), lambda b, p, i: (b, 0, 24 + p))],
        out_specs=pl.BlockSpec((1, tq, LANES), lambda b, p, i: (b, i, p)),
        out_shape=jax.ShapeDtypeStruct((batch, seq, n_pairs * LANES), BF16),
        scratch_shapes=[pltpu.VMEM((2, tq, seq), F32)],
        compiler_params=_params("parallel", "parallel", "parallel"),
        name="mla",
    )(proj, proj, proj)


def _mix_out_kernel(h_ref, oc_ref, od_ref, wa_ref, wb_ref, out_ref):
    out_ref[...] = h_ref[...] + _dot(oc_ref[...], wa_ref[...]) + _dot(od_ref[...], wb_ref[...])


def _mix_out(h2, oc, od, w_out):
    m, d = h2.shape
    half_w = w_out.shape[0] // 2
    tm = FFN_TM
    tile = lambda w: pl.BlockSpec((tm, w), lambda i: (i, 0))
    full = lambda r, c: pl.BlockSpec((r, c), lambda i: (0, 0))
    return pl.pallas_call(
        _mix_out_kernel,
        grid=(m // tm,),
        in_specs=[tile(d), tile(half_w), tile(half_w), full(half_w, d), full(half_w, d)],
        out_specs=tile(d),
        out_shape=jax.ShapeDtypeStruct((m, d), F32),
        compiler_params=_params("parallel"),
        name="mix_out",
    )(h2, oc, od, w_out[:half_w], w_out[half_w:])


def _rope_tables(seq):
    pos = jnp.arange(seq, dtype=F32)

    def tables(dim):
        inv_freq = ROPE_THETA ** (-jnp.arange(0, dim, 2, dtype=F32) / dim)
        ang = pos[:, None] * inv_freq[None, :]
        return jnp.cos(ang), jnp.sin(ang)

    cos, sin = tables(HEAD_DIM)
    reps = LANES // HEAD_DIM
    cos_t = jnp.tile(jnp.concatenate([cos, cos], axis=1), (1, reps))
    sin_t = jnp.tile(jnp.concatenate([-sin, sin], axis=1), (1, reps))
    cos_r, sin_r = tables(MLA_ROPE)
    pad_l, pad_r = MLA_NOPE, LANES - MLA_NOPE - MLA_ROPE
    cos_rt = jnp.pad(jnp.concatenate([cos_r, cos_r], axis=1), ((0, 0), (pad_l, pad_r)), constant_values=1.0)
    sin_rt = jnp.pad(jnp.concatenate([-sin_r, sin_r], axis=1), ((0, 0), (pad_l, pad_r)))
    return cos_t, sin_t, cos_rt, sin_rt


def kernel(x, ffn1_norm, ffn1_w1, ffn1_w3, ffn1_w2, mix_norm, ffn2_norm, ffn2_w1, ffn2_w3, ffn2_w2, even_w_in, even_w_out, na_rel_bias, odd_w_in, odd_w_out, swa_sink, mla_q_norm, mla_w_qb, mla_kv_norm, mla_w_kvb, final_norm):
    batch, seq, d = x.shape
    depth = ffn1_norm.shape[0]
    assert d == D_MODEL and seq % FFN_TM == 0 and seq % (GRID_W * NA_ROWS) == 0
    cos_t, sin_t, cos_rt, sin_rt = _rope_tables(seq)
    h = x.reshape(batch * seq, d)
    row = lambda v: v.reshape(1, -1).astype(F32)
    fg = row(final_norm)

    for i in range(depth):
        j = i // 2
        h = _ffn(h, row(ffn1_norm[i]), *_ffn_weights(ffn1_w1[i], ffn1_w3[i], ffn1_w2[i]), fg, False)
        if i % 2 == 0:
            dil_proj, na_proj = _even_proj(h, row(mix_norm[i]), even_w_in[j].astype(BF16), cos_t, sin_t, seq)
            oa = _dilated(dil_proj.reshape(batch, seq, -1), batch, seq).reshape(batch * seq, -1)
            on = _neighbourhood(na_proj.reshape(batch, seq, -1), na_rel_bias[j], batch, seq).reshape(batch * seq, -1)
            h = _mix_out(h, oa, on, even_w_out[j].astype(BF16))
        else:
            w, wqb, wkb, wvb = _odd_weights(odd_w_in[j], mla_w_qb[j], mla_w_kvb[j])
            proj = _odd_proj(h, row(mix_norm[i]), w, row(mla_q_norm[j]), wqb, row(mla_kv_norm[j]), wkb, wvb,
                             cos_t, sin_t, cos_rt, sin_rt, seq)
            proj = proj.reshape(batch, seq, -1)
            oc = _swa(proj, swa_sink[j].astype(F32), batch, seq).reshape(batch * seq, -1)
            od = _mla(proj, batch, seq).reshape(batch * seq, -1)
            h = _mix_out(h, oc, od, odd_w_out[j].astype(BF16))
        h = _ffn(h, row(ffn2_norm[i]), *_ffn_weights(ffn2_w1[i], ffn2_w3[i], ffn2_w2[i]), fg, i == depth - 1)
    return h.reshape(batch, seq, d)
```

```python
import functools

import jax
import jax.numpy as jnp
import numpy as np
from jax import lax
from jax.experimental import pallas as pl
from jax.experimental.pallas import tpu as pltpu

D_MODEL = 1024
D_FF = 2816
HEAD_DIM = 64
ROPE_THETA = 10000.0
NORM_EPS = 1e-6
NEG_INF = -1e30
LOG2_E = 1.4426950408889634

DIL_HEADS = 8
DIL_CONFIGS = ((128, 1), (512, 4), (2048, 16))
NA_HEADS = 8
GRID_W = 64
NA_ROWS = 8
NA_COLS = 16

SWA_Q_HEADS = 8
SWA_KV_HEADS = 2
SWA_HALF = 128
MLA_HEADS = 8
MLA_Q_RANK = 384
MLA_KV_RANK = 256
MLA_NOPE = 64
MLA_ROPE = 32
MLA_V = 64

LANES = 128
VMEM_LIMIT = 56 * 1024 * 1024

FFN_TM = 512
FFN_TF = 256
MLA_TQ = 256
MLA_SUB = 4
MLA_TK = 512
DIL_TQ = 128
DIL_UNROLL = 8
SWA_TQ = 128
SWA_UNROLL = 4
NA_GROUP = 4
NA_UNROLL = 4

BF16 = jnp.bfloat16
F32 = jnp.float32


def _params(*sem):
    return pltpu.CompilerParams(dimension_semantics=sem, vmem_limit_bytes=VMEM_LIMIT)


def _rms(x, g):
    ms = jnp.mean(x * x, axis=-1, keepdims=True)
    return x * lax.rsqrt(ms + NORM_EPS) * g


def _dot(a, b):
    return jnp.dot(a, b, preferred_element_type=F32)


def _dot_nt(a, b):
    return lax.dot_general(a, b, (((1,), (1,)), ((), ())), preferred_element_type=F32)


def _lane_is_first_head(shape):
    return lax.broadcasted_iota(jnp.int32, shape, len(shape) - 1) < HEAD_DIM


def _rope_block(x, cos, sin_signed, first_half, up, down):
    partner = jnp.where(first_half, pltpu.roll(x, up, 1), pltpu.roll(x, down, 1))
    return x * cos + partner * sin_signed


def _ffn_kernel(x_ref, g_ref, w1_ref, w3_ref, w2_ref, fg_ref, o_ref, a_ref, *, final):
    x = x_ref[...]
    hn = _rms(x, g_ref[...]).astype(BF16)
    for c in range(a_ref.shape[1] // FFN_TF):
        cols = slice(c * FFN_TF, (c + 1) * FFN_TF)
        h1 = _dot(hn, w1_ref[:, cols])
        h3 = _dot(hn, w3_ref[:, cols])
        a_ref[:, cols] = (h1 * jax.nn.sigmoid(h1) * h3).astype(BF16)
    y = x + 0.5 * _dot(a_ref[...], w2_ref[...])
    if final:
        y = _rms(y, fg_ref[...])
    o_ref[...] = y


def _ffn(x2, g, w1, w3, w2, fg, final):
    m, d = x2.shape
    f = w1.shape[1]
    assert f % FFN_TF == 0
    tm = FFN_TM
    const = lambda i: (0, 0)
    resident = dict(pipeline_mode=pl.Buffered(1))
    return pl.pallas_call(
        functools.partial(_ffn_kernel, final=final),
        grid=(m // tm,),
        in_specs=[
            pl.BlockSpec((tm, d), lambda i: (i, 0)),
            pl.BlockSpec((1, d), const),
            pl.BlockSpec((d, f), const, **resident),
            pl.BlockSpec((d, f), const, **resident),
            pl.BlockSpec((f, d), const, **resident),
            pl.BlockSpec((1, d), const),
        ],
        out_specs=pl.BlockSpec((tm, d), lambda i: (i, 0)),
        out_shape=jax.ShapeDtypeStruct((m, d), F32),
        scratch_shapes=[pltpu.VMEM((tm, f), BF16)],
        compiler_params=_params("parallel"),
        name="ffn",
    )(x2, g, w1, w3, w2, fg)


def _ffn_weights(w1, w3, w2):
    return w1.astype(BF16), w3.astype(BF16), w2.astype(BF16)


_EVEN_DIL_BLOCKS = 12
_EVEN_Q_BLOCKS = tuple(range(0, 4)) + tuple(range(12, 16))
_EVEN_ROPE_BLOCKS = 8


def _even_proj_kernel(x_ref, g_ref, w_ref, cos_ref, sin_ref, dil_ref, na_ref, *, n_blocks):
    hn = _rms(x_ref[...], g_ref[...]).astype(BF16)
    cos = cos_ref[...]
    sin = sin_ref[...]
    lane = lax.broadcasted_iota(jnp.int32, cos.shape, 1)
    first_half = (lane % HEAD_DIM) < HEAD_DIM // 2
    scale = HEAD_DIM ** -0.5 * LOG2_E
    for j2 in range(0, n_blocks, 2):
        wide = _dot(hn, w_ref[:, j2 * LANES:(j2 + 2) * LANES])
        for j in (j2, j2 + 1):
            blk = wide[:, (j - j2) * LANES:(j - j2 + 1) * LANES]
            if j < _EVEN_ROPE_BLOCKS:
                blk = _rope_block(blk, cos, sin, first_half, LANES - HEAD_DIM // 2, HEAD_DIM // 2)
            if j in _EVEN_Q_BLOCKS:
                blk = blk * scale
            if j < _EVEN_DIL_BLOCKS:
                dil_ref[:, j * LANES:(j + 1) * LANES] = blk
            else:
                jn = j - _EVEN_DIL_BLOCKS
                na_ref[:, jn * LANES:(jn + 1) * LANES] = blk.astype(BF16)


def _even_proj(h2, g, w_in, cos, sin, seq):
    m, d = h2.shape
    n_out = w_in.shape[1]
    n_dil = _EVEN_DIL_BLOCKS * LANES
    tm = FFN_TM
    per_seq = seq // tm
    return pl.pallas_call(
        functools.partial(_even_proj_kernel, n_blocks=n_out // LANES),
        grid=(m // tm,),
        in_specs=[
            pl.BlockSpec((tm, d), lambda i: (i, 0)),
            pl.BlockSpec((1, d), lambda i: (0, 0)),
            pl.BlockSpec((d, n_out), lambda i: (0, 0)),
            pl.BlockSpec((tm, LANES), lambda i: (i % per_seq, 0)),
            pl.BlockSpec((tm, LANES), lambda i: (i % per_seq, 0)),
        ],
        out_specs=[pl.BlockSpec((tm, n_dil), lambda i: (i, 0)),
                   pl.BlockSpec((tm, n_out - n_dil), lambda i: (i, 0))],
        out_shape=[jax.ShapeDtypeStruct((m, n_dil), F32),
                   jax.ShapeDtypeStruct((m, n_out - n_dil), BF16)],
        compiler_params=_params("parallel"),
        name="even_proj",
    )(h2, g, w_in, cos, sin)


def _pair_scores(q, k):
    first = _lane_is_first_head(q.shape)
    zero = jnp.zeros_like(q)
    q2 = jnp.concatenate([jnp.where(first, q, zero), jnp.where(first, zero, q)], axis=0)
    return _dot_nt(q2, k)


def _pair_merge(x2, tq):
    first = _lane_is_first_head((tq, LANES))
    return jnp.where(first, x2[:tq], x2[tq:])


def _band_bias(tq, half, win):
    row = np.arange(2 * tq)[:, None] % tq
    col = np.arange(win)[None, :]
    kinds = [np.where(np.abs(col - row - k * half) <= half, 0.0, NEG_INF) for k in range(3)]
    return jnp.asarray(np.stack(kinds), F32)


def _band_window(q0, tq, half, length, win):
    ks = pl.multiple_of(jnp.clip(q0 - half, 0, length - win), half)
    return ks, (q0 - ks) // half


def _pair_softmax(q, k, bias):
    s = _pair_scores(q, k) + bias
    m = jnp.max(s, axis=-1, keepdims=True)
    p = jnp.exp2(s - m)
    return p, m, jnp.sum(p, axis=-1, keepdims=True)


def _dilated_kernel(q_ref, k_ref, v_ref, *refs, seq, configs, tiles):
    n = len(configs)
    bias_refs, o_ref = refs[:n], refs[n]
    qs_ref, ks_ref, vs_ref = refs[n + 1:n + 4]
    ob_refs, lb_refs = refs[n + 4:2 * n + 4], refs[2 * n + 4:3 * n + 4]
    stage_refs = refs[3 * n + 4:]
    copies = {1: (q_ref, k_ref, v_ref)}
    for c, ((half, dil), (tq, win)) in enumerate(zip(configs, tiles)):
        length = seq // dil
        blocks = length // tq
        base_dil = max(d for d in copies if dil % d == 0)
        step = dil // base_dil
        keep = dil > 1 and any(d2 > dil and d2 % dil == 0 for _, d2 in configs)
        if keep:
            copies[dil] = stage_refs
        for a, (src, dst) in enumerate(zip(copies[base_dil], (qs_ref, ks_ref, vs_ref))):
            for r in range(dil):
                start = (r % base_dil) * (seq // base_dil) + r // base_dil
                rows = pl.ds(start, length, stride=step) if step > 1 else pl.ds(start, length)
                x = src[rows, :]
                dst[r * length:(r + 1) * length, :] = x.astype(BF16)
                if keep:
                    stage_refs[a][r * length:(r + 1) * length, :] = x

        def body(i, carry, c=c, half=half, dil=dil, tq=tq, win=win, length=length, blocks=blocks):
            geo = []
            for g in range(DIL_UNROLL):
                n = i * DIL_UNROLL + g
                r = n // blocks
                q0 = pl.multiple_of((n % blocks) * tq, tq)
                ks, kind = _band_window(q0, tq, half, length, win)
                geo.append((r, q0, pl.multiple_of(r * length, tq) + ks, kind))
            ss = [_pair_scores(qs_ref[pl.ds(pl.multiple_of(r * length, tq) + q0, tq), :], ks_ref[pl.ds(kb, win), :])
                  + bias_refs[c][kind] for r, q0, kb, kind in geo]
            ms = [jnp.max(s, axis=-1, keepdims=True) for s in ss]
            ps = [jnp.exp2(s - m) for s, m in zip(ss, ms)]
            ls = [jnp.sum(p, axis=-1, keepdims=True) for p in ps]
            os_ = [_dot(p.astype(BF16), vs_ref[pl.ds(kb, win), :]) for p, (_, _, kb, _) in zip(ps, geo)]
            for (r, q0, _, _), o2, m, l in zip(geo, os_, ms, ls):
                o2 = o2 / l
                lse2 = jnp.broadcast_to(m + jnp.log2(l), o2.shape)
                rows = pl.ds(r + dil * q0, tq, stride=dil) if dil > 1 else pl.ds(q0, tq)
                ob_refs[c][rows, :] = _pair_merge(o2, tq)
                lb_refs[c][rows, :] = _pair_merge(lse2, tq)
            return carry

        lax.fori_loop(0, seq // (tq * DIL_UNROLL), body, 0)

    chunk = 512

    def merge(i, carry):
        rows = pl.ds(pl.multiple_of(i * chunk, chunk), chunk)
        lses = [lb[rows, :] for lb in lb_refs]
        mx = functools.reduce(jnp.maximum, lses)
        es = [jnp.exp2(l - mx) for l in lses]
        num = sum(e * ob[rows, :] for e, ob in zip(es, ob_refs))
        o_ref[rows, :] = (num / sum(es)).astype(o_ref.dtype)
        return carry

    lax.fori_loop(0, seq // chunk, merge, 0)


def _dilated(dil_proj, batch, seq):
    n_pairs = DIL_HEADS // 2
    configs = tuple((window // 2 // dil, dil) for window, dil in DIL_CONFIGS)
    tiles = []
    for half, dil in configs:
        tq = min(DIL_TQ, seq // dil)
        tiles.append((tq, min(tq + 2 * half, seq // dil)))
        assert seq % (dil * tq) == 0 and tq % half == 0
    biases = [_band_bias(tq, half, win) for (half, _), (tq, win) in zip(configs, tiles)]

    def spec(base):
        return pl.BlockSpec((None, seq, LANES), lambda b, p: (b, 0, base + p))

    return pl.pallas_call(
        functools.partial(_dilated_kernel, seq=seq, configs=configs, tiles=tuple(tiles)),
        grid=(batch, n_pairs),
        in_specs=[spec(0), spec(4), spec(8)] + [pl.BlockSpec(b.shape, lambda b_, p: (0, 0, 0)) for b in biases],
        out_specs=pl.BlockSpec((None, seq, LANES), lambda b, p: (b, 0, p)),
        out_shape=jax.ShapeDtypeStruct((batch, seq, n_pairs * LANES), BF16),
        scratch_shapes=[pltpu.VMEM((seq, LANES), BF16)] * 3 + [pltpu.VMEM((seq, LANES), F32)] * (2 * len(configs) + 3),
        compiler_params=_params("parallel", "parallel"),
        name="dilated",
    )(dil_proj, dil_proj, dil_proj, *biases)


def _na_group_geometry(rows, kr):
    n_groups = rows // NA_GROUP
    wr = kr + NA_GROUP
    g = np.arange(n_groups)
    start = np.minimum(np.clip(g * NA_GROUP - kr // 2, 0, rows - kr), rows - wr)
    return n_groups, wr, start


def _na_kernel(q_ref, k_ref, v_ref, bias_ref, o_ref, *, rows, kr):
    n_groups, wr, _ = _na_group_geometry(rows, kr)
    tq = NA_GROUP * GRID_W

    def body(i, carry):
        geo = []
        for u in range(NA_UNROLL):
            g = i * NA_UNROLL + u
            start = jnp.minimum(jnp.clip(g * NA_GROUP - kr // 2, 0, rows - kr), rows - wr)
            kind = jnp.where(g == 0, 0, jnp.where(g == n_groups - 1, 2, 1))
            geo.append((pl.multiple_of(g * tq, tq), pl.multiple_of(start * GRID_W, GRID_W), kind))
        ss = [_pair_scores(q_ref[pl.ds(q0, tq), :], k_ref[pl.ds(k0, wr * GRID_W), :]) + bias_ref[kind]
              for q0, k0, kind in geo]
        ms = [jnp.max(s, axis=-1, keepdims=True) for s in ss]
        ps = [jnp.exp2(s - m) for s, m in zip(ss, ms)]
        ls = [jnp.sum(p, axis=-1, keepdims=True) for p in ps]
        os_ = [_dot(p.astype(BF16), v_ref[pl.ds(k0, wr * GRID_W), :]) for p, (_, k0, _) in zip(ps, geo)]
        for (q0, _, _), o2, l in zip(geo, os_, ls):
            o_ref[pl.ds(q0, tq), :] = _pair_merge(o2 / l, tq).astype(o_ref.dtype)
        return carry

    lax.fori_loop(0, n_groups // NA_UNROLL, body, 0)


def _na_bias_table(rpb, rows, kr):
    n_groups, wr, start = _na_group_geometry(rows, kr)
    qcol = np.arange(GRID_W)
    kcol = np.arange(GRID_W)
    wstart = np.clip(qcol - NA_COLS // 2, 0, GRID_W - NA_COLS)
    col_ok = (kcol[None, :] >= wstart[:, None]) & (kcol[None, :] < wstart[:, None] + NA_COLS)
    dc = np.clip(kcol[None, :] - qcol[:, None] + NA_COLS - 1, 0, 2 * NA_COLS - 2)

    def geometry(g):
        qrow = g * NA_GROUP + np.arange(NA_GROUP)
        krow = start[g] + np.arange(wr)
        rs = np.clip(qrow - kr // 2, 0, rows - kr)
        row_ok = (krow[None, :] >= rs[:, None]) & (krow[None, :] < rs[:, None] + kr)
        dr = np.clip(krow[None, :] - qrow[:, None] + NA_ROWS - 1, 0, 2 * NA_ROWS - 2)
        return row_ok, dr

    kinds = [0, min(1, n_groups - 1), n_groups - 1]
    for g in range(1, n_groups - 1):
        assert all(np.array_equal(a, b) for a, b in zip(geometry(g), geometry(kinds[1])))
    onehot = (dc[None] == np.arange(2 * NA_COLS - 1)[:, None, None]) & col_ok[None]
    planes = jnp.einsum("hrd,dqk->hrqk", rpb * LOG2_E, jnp.asarray(onehot, F32), precision=lax.Precision.HIGHEST)
    planes = planes + jnp.asarray(np.where(col_ok, 0.0, NEG_INF), F32)
    masked = jnp.full(planes.shape[:1] + planes.shape[2:], NEG_INF, F32)
    slabs = []
    for g in kinds:
        row_ok, dr = geometry(g)
        rows_i = []
        for i in range(NA_GROUP):
            cols_j = [planes[:, int(dr[i, j])] if row_ok[i, j] else masked for j in range(wr)]
            rows_i.append(jnp.concatenate(cols_j, axis=-1))
        slab = jnp.stack(rows_i, axis=1)
        slabs.append(slab.reshape(NA_HEADS // 2, 2 * NA_GROUP * GRID_W, wr * GRID_W))
    return jnp.stack(slabs)


def _neighbourhood(proj, rpb, batch, seq):
    rows = seq // GRID_W
    kr = min(NA_ROWS, rows)
    assert rows % (NA_GROUP * NA_UNROLL) == 0 and rows >= kr + NA_GROUP
    table = _na_bias_table(rpb.astype(F32), rows, kr)
    n_pairs = NA_HEADS // 2

    def spec(base):
        return pl.BlockSpec((None, seq, LANES), lambda b, p: (b, 0, base + p))

    return pl.pallas_call(
        functools.partial(_na_kernel, rows=rows, kr=kr),
        grid=(batch, n_pairs),
        in_specs=[spec(0), spec(4), spec(8),
                  pl.BlockSpec((3, None) + table.shape[2:], lambda b, p: (0, p, 0, 0))],
        out_specs=pl.BlockSpec((None, seq, LANES), lambda b, p: (b, 0, p)),
        out_shape=jax.ShapeDtypeStruct((batch, seq, n_pairs * LANES), BF16),
        compiler_params=_params("parallel", "parallel"),
        name="neighbourhood",
    )(proj, proj, proj, table)


_ODD_W_COLS = 14 * LANES
_ODD_OUT_COLS = 28 * LANES


def _odd_proj_kernel(x_ref, g_ref, w_ref, qn_ref, wqb_ref, kvn_ref, wkb_ref, wvb_ref,
                     cos_ref, sin_ref, cosr_ref, sinr_ref, o_ref):
    hn = _rms(x_ref[...], g_ref[...]).astype(BF16)
    cos, sin = cos_ref[...], sin_ref[...]
    cosr, sinr = cosr_ref[...], sinr_ref[...]
    lane = lax.broadcasted_iota(jnp.int32, cos.shape, 1)
    first_half = (lane % HEAD_DIM) < HEAD_DIM // 2
    first_half_r = lane < MLA_NOPE + MLA_ROPE // 2
    swa_scale = HEAD_DIM ** -0.5 * LOG2_E
    mla_scale = (MLA_NOPE + MLA_ROPE) ** -0.5 * LOG2_E

    def col(j, n=1):
        return slice(j * LANES, (j + n) * LANES)

    def rope(blk):
        return _rope_block(blk, cos, sin, first_half, LANES - HEAD_DIM // 2, HEAD_DIM // 2)

    def rope_r(blk):
        return _rope_block(blk, cosr, sinr, first_half_r, LANES - MLA_ROPE // 2, MLA_ROPE // 2)

    def halves(wide):
        return wide[:, :LANES], wide[:, LANES:]

    for j in range(0, 8, 2):
        for jj, blk in zip((j, j + 1), halves(_dot(hn, w_ref[:, col(j, 2)]))):
            if jj < 4:
                blk = rope(blk) * swa_scale
            elif jj < 6:
                blk = rope(blk)
            o_ref[:, col(jj)] = blk.astype(BF16)

    latent = _dot(hn, w_ref[:, col(8, 6)])
    q_an = _rms(latent[:, :3 * LANES], qn_ref[...]).astype(BF16)
    kv_an = _rms(latent[:, 3 * LANES:5 * LANES], kvn_ref[...]).astype(BF16)
    k_pe = rope_r(latent[:, 5 * LANES:])
    for h in range(0, MLA_HEADS, 2):
        for hh, qh in zip((h, h + 1), halves(_dot(q_an, wqb_ref[:, col(h, 2)]))):
            o_ref[:, col(8 + hh)] = (rope_r(qh) * mla_scale).astype(BF16)
        for hh, kh in zip((h, h + 1), halves(_dot(kv_an, wkb_ref[:, col(h, 2)]))):
            o_ref[:, col(16 + hh)] = (kh + k_pe).astype(BF16)
    for j in range(0, MLA_HEADS // 2, 2):
        o_ref[:, col(24 + j, 2)] = _dot(kv_an, wvb_ref[:, col(j, 2)]).astype(BF16)


def _odd_proj(h2, g, w, qn, wqb, kvn, wkb, wvb, cos, sin, cosr, sinr, seq):
    m, d = h2.shape
    tm = FFN_TM
    per_seq = seq // tm
    full = lambda a: pl.BlockSpec(a.shape, lambda i: (0, 0))
    tab = pl.BlockSpec((tm, LANES), lambda i: (i % per_seq, 0))
    return pl.pallas_call(
        _odd_proj_kernel,
        grid=(m // tm,),
        in_specs=[pl.BlockSpec((tm, d), lambda i: (i, 0)), full(g), full(w), full(qn), full(wqb),
                  full(kvn), full(wkb), full(wvb), tab, tab, tab, tab],
        out_specs=pl.BlockSpec((tm, _ODD_OUT_COLS), lambda i: (i, 0)),
        out_shape=jax.ShapeDtypeStruct((m, _ODD_OUT_COLS), BF16),
        compiler_params=_params("parallel"),
        name="odd_proj",
    )(h2, g, w, qn, wqb, kvn, wkb, wvb, cos, sin, cosr, sinr)


def _odd_weights(w_in, w_qb, w_kvb):
    d = w_in.shape[0]
    c = np.cumsum([0, SWA_Q_HEADS * HEAD_DIM, SWA_KV_HEADS * HEAD_DIM, SWA_KV_HEADS * HEAD_DIM,
                   MLA_Q_RANK, MLA_KV_RANK, MLA_ROPE])

    def dup(w):
        w = w.reshape(d, SWA_KV_HEADS, 1, HEAD_DIM)
        return jnp.broadcast_to(w, (d, SWA_KV_HEADS, 2, HEAD_DIM)).reshape(d, SWA_KV_HEADS * LANES)

    w_kpe = jnp.pad(w_in[:, c[5]:c[6]], ((0, 0), (MLA_NOPE, LANES - MLA_NOPE - MLA_ROPE)))
    w = jnp.concatenate([w_in[:, c[0]:c[1]], dup(w_in[:, c[1]:c[2]]), dup(w_in[:, c[2]:c[3]]),
                         w_in[:, c[3]:c[4]], w_in[:, c[4]:c[5]], w_kpe], axis=1).astype(BF16)
    qk = MLA_NOPE + MLA_ROPE
    wqb = jnp.pad(w_qb.reshape(MLA_Q_RANK, MLA_HEADS, qk), ((0, 0), (0, 0), (0, LANES - qk)))
    wqb = wqb.reshape(MLA_Q_RANK, MLA_HEADS * LANES).astype(BF16)
    kv = w_kvb.reshape(MLA_KV_RANK, MLA_HEADS, MLA_NOPE + MLA_V)
    wkb = jnp.pad(kv[:, :, :MLA_NOPE], ((0, 0), (0, 0), (0, LANES - MLA_NOPE)))
    wkb = wkb.reshape(MLA_KV_RANK, MLA_HEADS * LANES).astype(BF16)
    wvb = kv[:, :, MLA_NOPE:].reshape(MLA_KV_RANK, MLA_HEADS * MLA_V).astype(BF16)
    return w, wqb, wkb, wvb


def _swa_kernel(sink_ref, q_ref, k_ref, v_ref, bias_ref, o_ref, *, seq, half, tq, win):
    pair = pl.program_id(1)
    first = lax.broadcasted_iota(jnp.int32, (2 * tq, 1), 0) < tq
    sink = jnp.where(first, sink_ref[2 * pair], sink_ref[2 * pair + 1]) * LOG2_E

    def body(i, carry):
        geo = []
        for g in range(SWA_UNROLL):
            q0 = pl.multiple_of((i * SWA_UNROLL + g) * tq, tq)
            geo.append((q0,) + _band_window(q0, tq, half, seq, win))
        ss = [_pair_scores(q_ref[pl.ds(q0, tq), :], k_ref[pl.ds(ks, win), :]) + bias_ref[kind]
              for q0, ks, kind in geo]
        ms = [jnp.maximum(jnp.max(s, axis=-1, keepdims=True), sink) for s in ss]
        ps = [jnp.exp2(s - m) for s, m in zip(ss, ms)]
        ls = [jnp.sum(p, axis=-1, keepdims=True) + jnp.exp2(sink - m) for p, m in zip(ps, ms)]
        os_ = [_dot(p.astype(BF16), v_ref[pl.ds(ks, win), :]) for p, (_, ks, _) in zip(ps, geo)]
        for (q0, _, _), o2, l in zip(geo, os_, ls):
            o_ref[pl.ds(q0, tq), :] = _pair_merge(o2 / l, tq).astype(o_ref.dtype)
        return carry

    lax.fori_loop(0, seq // (tq * SWA_UNROLL), body, 0)


def _swa(proj, sink, batch, seq):
    tq = SWA_TQ
    win = tq + 2 * SWA_HALF
    assert seq % (tq * SWA_UNROLL) == 0 and tq % SWA_HALF == 0 and seq >= win
    n_pairs = SWA_Q_HEADS // 2
    pairs_per_kv = n_pairs // SWA_KV_HEADS
    bias = _band_bias(tq, SWA_HALF, win)
    return pl.pallas_call(
        functools.partial(_swa_kernel, seq=seq, half=SWA_HALF, tq=tq, win=win),
        grid=(batch, n_pairs),
        in_specs=[pl.BlockSpec(memory_space=pltpu.SMEM),
                  pl.BlockSpec((None, seq, LANES), lambda b, p: (b, 0, p)),
                  pl.BlockSpec((None, seq, LANES), lambda b, p: (b, 0, 4 + p // pairs_per_kv)),
                  pl.BlockSpec((None, seq, LANES), lambda b, p: (b, 0, 6 + p // pairs_per_kv)),
                  pl.BlockSpec(bias.shape, lambda b, p: (0, 0, 0))],
        out_specs=pl.BlockSpec((None, seq, LANES), lambda b, p: (b, 0, p)),
        out_shape=jax.ShapeDtypeStruct((batch, seq, n_pairs * LANES), BF16),
        compiler_params=_params("parallel", "parallel"),
        name="swa",
    )(sink, proj, proj, proj, bias)


def _mla_kernel(q_ref, k_ref, v_ref, o_ref, s_ref, p_ref, *, seq, tk):
    tq = MLA_TQ
    groups = tk // LANES
    first = _lane_is_first_head((tq, LANES))
    for t in range(q_ref.shape[1] // tq):
        rows = slice(t * tq, (t + 1) * tq)
        outs = []
        for h in range(2):
            q = q_ref[0, rows, h * LANES:(h + 1) * LANES]
            m_l = jnp.full((tq, LANES), NEG_INF, F32)
            for c in range(seq // tk):
                s = _dot_nt(q, k_ref[0, c * tk:(c + 1) * tk, h * LANES:(h + 1) * LANES])
                s_ref[h, :, c * tk:(c + 1) * tk] = s
                for g in range(groups):
                    m_l = jnp.maximum(m_l, s[:, g * LANES:(g + 1) * LANES])
            m = jnp.max(m_l, axis=-1, keepdims=True)
            l_l = jnp.zeros((tq, LANES), F32)
            for c in range(seq // tk):
                p = jnp.exp2(s_ref[h, :, c * tk:(c + 1) * tk] - m)
                for g in range(groups):
                    l_l = l_l + p[:, g * LANES:(g + 1) * LANES]
                p_ref[h, :, c * tk:(c + 1) * tk] = p.astype(BF16)
            outs.append(_dot(p_ref[h], v_ref[0]) / jnp.sum(l_l, axis=-1, keepdims=True))
        o_ref[0, rows, :] = jnp.where(first, outs[0], outs[1]).astype(o_ref.dtype)


def _mla(proj, batch, seq):
    tq = MLA_TQ * MLA_SUB
    assert seq % tq == 0 and seq % MLA_TK == 0
    n_pairs = MLA_HEADS // 2
    return pl.pallas_call(
        functools.partial(_mla_kernel, seq=seq, tk=MLA_TK),
        grid=(batch, n_pairs, seq // tq),
        in_specs=[pl.BlockSpec((1, tq, 2 * LANES), lambda b, p, i: (b, i, 4 + p)),
                  pl.BlockSpec((1, seq, 2 * LANES), lambda b, p, i: (b, 0, 8 + p)),
                  pl.BlockSpec((1, seq, LANES), lambda b, p, i: (b, 0, 24 + p))],
        out_specs=pl.BlockSpec((1, tq, LANES), lambda b, p, i: (b, i, p)),
        out_shape=jax.ShapeDtypeStruct((batch, seq, n_pairs * LANES), BF16),
        scratch_shapes=[pltpu.VMEM((2, MLA_TQ, seq), F32), pltpu.VMEM((2, MLA_TQ, seq), BF16)],
        compiler_params=_params("parallel", "parallel", "parallel"),
        name="mla",
    )(proj, proj, proj)


def _mix_out_kernel(h_ref, oc_ref, od_ref, wa_ref, wb_ref, out_ref):
    out_ref[...] = h_ref[...] + _dot(oc_ref[...], wa_ref[...]) + _dot(od_ref[...], wb_ref[...])


def _mix_out(h2, oc, od, w_out):
    m, d = h2.shape
    half_w = w_out.shape[0] // 2
    tm = FFN_TM
    tile = lambda w: pl.BlockSpec((tm, w), lambda i: (i, 0))
    full = lambda r, c: pl.BlockSpec((r, c), lambda i: (0, 0))
    return pl.pallas_call(
        _mix_out_kernel,
        grid=(m // tm,),
        in_specs=[tile(d), tile(half_w), tile(half_w), full(half_w, d), full(half_w, d)],
        out_specs=tile(d),
        out_shape=jax.ShapeDtypeStruct((m, d), F32),
        compiler_params=_params("parallel"),
        name="mix_out",
    )(h2, oc, od, w_out[:half_w], w_out[half_w:])


def _rope_tables(seq):
    pos = jnp.arange(seq, dtype=F32)

    def tables(dim):
        inv_freq = ROPE_THETA ** (-jnp.arange(0, dim, 2, dtype=F32) / dim)
        ang = pos[:, None] * inv_freq[None, :]
        return jnp.cos(ang), jnp.sin(ang)

    cos, sin = tables(HEAD_DIM)
    reps = LANES // HEAD_DIM
    cos_t = jnp.tile(jnp.concatenate([cos, cos], axis=1), (1, reps))
    sin_t = jnp.tile(jnp.concatenate([-sin, sin], axis=1), (1, reps))
    cos_r, sin_r = tables(MLA_ROPE)
    pad_l, pad_r = MLA_NOPE, LANES - MLA_NOPE - MLA_ROPE
    cos_rt = jnp.pad(jnp.concatenate([cos_r, cos_r], axis=1), ((0, 0), (pad_l, pad_r)), constant_values=1.0)
    sin_rt = jnp.pad(jnp.concatenate([-sin_r, sin_r], axis=1), ((0, 0), (pad_l, pad_r)))
    return cos_t, sin_t, cos_rt, sin_rt


def kernel(x, ffn1_norm, ffn1_w1, ffn1_w3, ffn1_w2, mix_norm, ffn2_norm, ffn2_w1, ffn2_w3, ffn2_w2, even_w_in, even_w_out, na_rel_bias, odd_w_in, odd_w_out, swa_sink, mla_q_norm, mla_w_qb, mla_kv_norm, mla_w_kvb, final_norm):
    batch, seq, d = x.shape
    depth = ffn1_norm.shape[0]
    assert d == D_MODEL and seq % FFN_TM == 0 and seq % (GRID_W * NA_ROWS) == 0
    cos_t, sin_t, cos_rt, sin_rt = _rope_tables(seq)
    h = x.reshape(batch * seq, d)
    row = lambda v: v.reshape(1, -1).astype(F32)
    fg = row(final_norm)

    for i in range(depth):
        j = i // 2
        h = _ffn(h, row(ffn1_norm[i]), *_ffn_weights(ffn1_w1[i], ffn1_w3[i], ffn1_w2[i]), fg, False)
        if i % 2 == 0:
            dil_proj, na_proj = _even_proj(h, row(mix_norm[i]), even_w_in[j].astype(BF16), cos_t, sin_t, seq)
            oa = _dilated(dil_proj.reshape(batch, seq, -1), batch, seq).reshape(batch * seq, -1)
            on = _neighbourhood(na_proj.reshape(batch, seq, -1), na_rel_bias[j], batch, seq).reshape(batch * seq, -1)
            h = _mix_out(h, oa, on, even_w_out[j].astype(BF16))
        else:
            w, wqb, wkb, wvb = _odd_weights(odd_w_in[j], mla_w_qb[j], mla_w_kvb[j])
            proj = _odd_proj(h, row(mix_norm[i]), w, row(mla_q_norm[j]), wqb, row(mla_kv_norm[j]), wkb, wvb,
                             cos_t, sin_t, cos_rt, sin_rt, seq)
            proj = proj.reshape(batch, seq, -1)
            oc = _swa(proj, swa_sink[j].astype(F32), batch, seq).reshape(batch * seq, -1)
            od = _mla(proj, batch, seq).reshape(batch * seq, -1)
            h = _mix_out(h, oc, od, odd_w_out[j].astype(BF16))
        h = _ffn(h, row(ffn2_norm[i]), *_ffn_weights(ffn2_w1[i], ffn2_w3[i], ffn2_w2[i]), fg, i == depth - 1)
    return h.reshape(batch, seq, d)
```

```python
import functools

import jax
import jax.numpy as jnp
import numpy as np
from jax import lax
from jax.experimental import pallas as pl
from jax.experimental.pallas import tpu as pltpu

D_MODEL = 1024
D_FF = 2816
HEAD_DIM = 64
ROPE_THETA = 10000.0
NORM_EPS = 1e-6
NEG_INF = -1e30
LOG2_E = 1.4426950408889634

DIL_HEADS = 8
DIL_CONFIGS = ((128, 1), (512, 4), (2048, 16))
NA_HEADS = 8
GRID_W = 64
NA_ROWS = 8
NA_COLS = 16

SWA_Q_HEADS = 8
SWA_KV_HEADS = 2
SWA_HALF = 128
MLA_HEADS = 8
MLA_Q_RANK = 384
MLA_KV_RANK = 256
MLA_NOPE = 64
MLA_ROPE = 32
MLA_V = 64

LANES = 128
VMEM_LIMIT = 56 * 1024 * 1024

FFN_TM = 512
FFN_TF = 256
MLA_TQ = 256
MLA_SUB = 4
MLA_TK = 512
DIL_TQ = 128
DIL_UNROLL = 8
SWA_TQ = 128
SWA_UNROLL = 4
NA_GROUP = 4
NA_UNROLL = 4

BF16 = jnp.bfloat16
F32 = jnp.float32


def _params(*sem):
    return pltpu.CompilerParams(dimension_semantics=sem, vmem_limit_bytes=VMEM_LIMIT)


def _rms(x, g):
    ms = jnp.mean(x * x, axis=-1, keepdims=True)
    return x * lax.rsqrt(ms + NORM_EPS) * g


def _dot(a, b):
    return jnp.dot(a, b, preferred_element_type=F32)


def _dot_nt(a, b):
    return lax.dot_general(a, b, (((1,), (1,)), ((), ())), preferred_element_type=F32)


def _lane_is_first_head(shape):
    return lax.broadcasted_iota(jnp.int32, shape, len(shape) - 1) < HEAD_DIM


def _rope_block(x, cos, sin_signed, first_half, up, down):
    partner = jnp.where(first_half, pltpu.roll(x, up, 1), pltpu.roll(x, down, 1))
    return x * cos + partner * sin_signed


def _ffn_kernel(*refs, mix, final):
    if mix:
        x_ref, oc_ref, od_ref, wa_ref, wb_ref, g_ref, w1_ref, w3_ref, w2_ref, fg_ref, o_ref, a_ref = refs
        x = x_ref[...] + _dot(oc_ref[...], wa_ref[...]) + _dot(od_ref[...], wb_ref[...])
    else:
        x_ref, g_ref, w1_ref, w3_ref, w2_ref, fg_ref, o_ref, a_ref = refs
        x = x_ref[...]
    hn = _rms(x, g_ref[...]).astype(BF16)
    for c in range(a_ref.shape[1] // FFN_TF):
        cols = slice(c * FFN_TF, (c + 1) * FFN_TF)
        h1 = _dot(hn, w1_ref[:, cols])
        h3 = _dot(hn, w3_ref[:, cols])
        a_ref[:, cols] = (h1 * jax.nn.sigmoid(h1) * h3).astype(BF16)
    y = x + 0.5 * _dot(a_ref[...], w2_ref[...])
    if final:
        y = _rms(y, fg_ref[...])
    o_ref[...] = y


def _ffn(x2, g, w1, w3, w2, layer, fg, final, mix=None):
    m, d = x2.shape
    f = w1.shape[2]
    assert f % FFN_TF == 0
    tm = FFN_TM
    const = lambda i: (0, 0)
    of_layer = lambda i: (layer, 0, 0)
    resident = dict(pipeline_mode=pl.Buffered(1))
    mix_args, mix_specs = (), []
    if mix is not None:
        oc, od, w_out = mix
        half_w = w_out.shape[0] // 2
        mix_args = (oc, od, w_out[:half_w], w_out[half_w:])
        mix_specs = [pl.BlockSpec((tm, half_w), lambda i: (i, 0))] * 2 + [pl.BlockSpec((half_w, d), const, **resident)] * 2
    return pl.pallas_call(
        functools.partial(_ffn_kernel, mix=mix is not None, final=final),
        grid=(m // tm,),
        in_specs=[pl.BlockSpec((tm, d), lambda i: (i, 0))] + mix_specs + [
            pl.BlockSpec((1, d), const),
            pl.BlockSpec((None, d, f), of_layer, **resident),
            pl.BlockSpec((None, d, f), of_layer, **resident),
            pl.BlockSpec((None, f, d), of_layer, **resident),
            pl.BlockSpec((1, d), const),
        ],
        out_specs=pl.BlockSpec((tm, d), lambda i: (i, 0)),
        out_shape=jax.ShapeDtypeStruct((m, d), F32),
        scratch_shapes=[pltpu.VMEM((tm, f), BF16)],
        compiler_params=_params("parallel"),
        name="ffn",
    )(x2, *mix_args, g, w1, w3, w2, fg)


def _ffn_weights(w1, w3, w2):
    return w1.astype(BF16), w3.astype(BF16), w2.astype(BF16)


_EVEN_DIL_BLOCKS = 12
_EVEN_Q_BLOCKS = tuple(range(0, 4)) + tuple(range(12, 16))
_EVEN_ROPE_BLOCKS = 8


def _even_proj_kernel(x_ref, g_ref, w_ref, cos_ref, sin_ref, dil_ref, na_ref, *, n_blocks):
    hn = _rms(x_ref[...], g_ref[...]).astype(BF16)
    cos = cos_ref[...]
    sin = sin_ref[...]
    lane = lax.broadcasted_iota(jnp.int32, cos.shape, 1)
    first_half = (lane % HEAD_DIM) < HEAD_DIM // 2
    scale = HEAD_DIM ** -0.5 * LOG2_E
    for j2 in range(0, n_blocks, 2):
        wide = _dot(hn, w_ref[:, j2 * LANES:(j2 + 2) * LANES])
        for j in (j2, j2 + 1):
            blk = wide[:, (j - j2) * LANES:(j - j2 + 1) * LANES]
            if j < _EVEN_ROPE_BLOCKS:
                blk = _rope_block(blk, cos, sin, first_half, LANES - HEAD_DIM // 2, HEAD_DIM // 2)
            if j in _EVEN_Q_BLOCKS:
                blk = blk * scale
            if j < _EVEN_DIL_BLOCKS:
                dil_ref[:, j * LANES:(j + 1) * LANES] = blk
            else:
                jn = j - _EVEN_DIL_BLOCKS
                na_ref[:, jn * LANES:(jn + 1) * LANES] = blk.astype(BF16)


def _even_proj(h2, g, w_in, cos, sin, seq):
    m, d = h2.shape
    n_out = w_in.shape[1]
    n_dil = _EVEN_DIL_BLOCKS * LANES
    tm = FFN_TM
    per_seq = seq // tm
    return pl.pallas_call(
        functools.partial(_even_proj_kernel, n_blocks=n_out // LANES),
        grid=(m // tm,),
        in_specs=[
            pl.BlockSpec((tm, d), lambda i: (i, 0)),
            pl.BlockSpec((1, d), lambda i: (0, 0)),
            pl.BlockSpec((d, n_out), lambda i: (0, 0)),
            pl.BlockSpec((tm, LANES), lambda i: (i % per_seq, 0)),
            pl.BlockSpec((tm, LANES), lambda i: (i % per_seq, 0)),
        ],
        out_specs=[pl.BlockSpec((tm, n_dil), lambda i: (i, 0)),
                   pl.BlockSpec((tm, n_out - n_dil), lambda i: (i, 0))],
        out_shape=[jax.ShapeDtypeStruct((m, n_dil), F32),
                   jax.ShapeDtypeStruct((m, n_out - n_dil), BF16)],
        compiler_params=_params("parallel"),
        name="even_proj",
    )(h2, g, w_in, cos, sin)


def _pair_scores(q, k):
    first = _lane_is_first_head(q.shape)
    zero = jnp.zeros_like(q)
    q2 = jnp.concatenate([jnp.where(first, q, zero), jnp.where(first, zero, q)], axis=0)
    return _dot_nt(q2, k)


def _pair_merge(x2, tq):
    first = _lane_is_first_head((tq, LANES))
    return jnp.where(first, x2[:tq], x2[tq:])


def _band_bias(tq, half, win):
    row = np.arange(2 * tq)[:, None] % tq
    col = np.arange(win)[None, :]
    kinds = [np.where(np.abs(col - row - k * half) <= half, 0.0, NEG_INF) for k in range(3)]
    return jnp.asarray(np.stack(kinds), F32)


def _band_window(q0, tq, half, length, win):
    ks = pl.multiple_of(jnp.clip(q0 - half, 0, length - win), half)
    return ks, (q0 - ks) // half


def _pair_softmax(q, k, bias):
    s = _pair_scores(q, k) + bias
    m = jnp.max(s, axis=-1, keepdims=True)
    p = jnp.exp2(s - m)
    return p, m, jnp.sum(p, axis=-1, keepdims=True)


def _dilated_kernel(q_ref, k_ref, v_ref, *refs, seq, configs, tiles):
    n = len(configs)
    bias_refs, o_ref = refs[:n], refs[n]
    qs_ref, ks_ref, vs_ref = refs[n + 1:n + 4]
    ob_refs, lb_refs = refs[n + 4:2 * n + 4], refs[2 * n + 4:3 * n + 4]
    stage_refs = refs[3 * n + 4:]
    copies = {1: (q_ref, k_ref, v_ref)}
    for c, ((half, dil), (tq, win)) in enumerate(zip(configs, tiles)):
        length = seq // dil
        blocks = length // tq
        base_dil = max(d for d in copies if dil % d == 0)
        step = dil // base_dil
        keep = dil > 1 and any(d2 > dil and d2 % dil == 0 for _, d2 in configs)
        if keep:
            copies[dil] = stage_refs
        for a, (src, dst) in enumerate(zip(copies[base_dil], (qs_ref, ks_ref, vs_ref))):
            for r in range(dil):
                start = (r % base_dil) * (seq // base_dil) + r // base_dil
                rows = pl.ds(start, length, stride=step) if step > 1 else pl.ds(start, length)
                x = src[rows, :]
                dst[r * length:(r + 1) * length, :] = x.astype(BF16)
                if keep:
                    stage_refs[a][r * length:(r + 1) * length, :] = x

        def body(i, carry, c=c, half=half, dil=dil, tq=tq, win=win, length=length, blocks=blocks):
            geo = []
            for g in range(DIL_UNROLL):
                n = i * DIL_UNROLL + g
                r = n // blocks
                q0 = pl.multiple_of((n % blocks) * tq, tq)
                ks, kind = _band_window(q0, tq, half, length, win)
                geo.append((r, q0, pl.multiple_of(r * length, tq) + ks, kind))
            ss = [_pair_scores(qs_ref[pl.ds(pl.multiple_of(r * length, tq) + q0, tq), :], ks_ref[pl.ds(kb, win), :])
                  + bias_refs[c][kind] for r, q0, kb, kind in geo]
            ms = [jnp.max(s, axis=-1, keepdims=True) for s in ss]
            ps = [jnp.exp2(s - m) for s, m in zip(ss, ms)]
            ls = [jnp.sum(p, axis=-1, keepdims=True) for p in ps]
            os_ = [_dot(p.astype(BF16), vs_ref[pl.ds(kb, win), :]) for p, (_, _, kb, _) in zip(ps, geo)]
            for (r, q0, _, _), o2, m, l in zip(geo, os_, ms, ls):
                o2 = o2 / l
                lse2 = jnp.broadcast_to(m + jnp.log2(l), o2.shape)
                rows = pl.ds(r + dil * q0, tq, stride=dil) if dil > 1 else pl.ds(q0, tq)
                ob_refs[c][rows, :] = _pair_merge(o2, tq)
                lb_refs[c][rows, :] = _pair_merge(lse2, tq)
            return carry

        lax.fori_loop(0, seq // (tq * DIL_UNROLL), body, 0)

    chunk = 512

    def merge(i, carry):
        rows = pl.ds(pl.multiple_of(i * chunk, chunk), chunk)
        lses = [lb[rows, :] for lb in lb_refs]
        mx = functools.reduce(jnp.maximum, lses)
        es = [jnp.exp2(l - mx) for l in lses]
        num = sum(e * ob[rows, :] for e, ob in zip(es, ob_refs))
        o_ref[rows, :] = (num / sum(es)).astype(o_ref.dtype)
        return carry

    lax.fori_loop(0, seq // chunk, merge, 0)


def _dilated(dil_proj, batch, seq):
    n_pairs = DIL_HEADS // 2
    configs = tuple((window // 2 // dil, dil) for window, dil in DIL_CONFIGS)
    tiles = []
    for half, dil in configs:
        tq = min(DIL_TQ, seq // dil)
        tiles.append((tq, min(tq + 2 * half, seq // dil)))
        assert seq % (dil * tq) == 0 and tq % half == 0
    biases = [_band_bias(tq, half, win) for (half, _), (tq, win) in zip(configs, tiles)]

    def spec(base):
        return pl.BlockSpec((None, seq, LANES), lambda b, p: (b, 0, base + p))

    return pl.pallas_call(
        functools.partial(_dilated_kernel, seq=seq, configs=configs, tiles=tuple(tiles)),
        grid=(batch, n_pairs),
        in_specs=[spec(0), spec(4), spec(8)] + [pl.BlockSpec(b.shape, lambda b_, p: (0, 0, 0)) for b in biases],
        out_specs=pl.BlockSpec((None, seq, LANES), lambda b, p: (b, 0, p)),
        out_shape=jax.ShapeDtypeStruct((batch, seq, n_pairs * LANES), BF16),
        scratch_shapes=[pltpu.VMEM((seq, LANES), BF16)] * 3 + [pltpu.VMEM((seq, LANES), F32)] * (2 * len(configs) + 3),
        compiler_params=_params("parallel", "parallel"),
        name="dilated",
    )(dil_proj, dil_proj, dil_proj, *biases)


def _na_group_geometry(rows, kr):
    n_groups = rows // NA_GROUP
    wr = kr + NA_GROUP
    g = np.arange(n_groups)
    start = np.minimum(np.clip(g * NA_GROUP - kr // 2, 0, rows - kr), rows - wr)
    return n_groups, wr, start


def _na_kernel(q_ref, k_ref, v_ref, bias_ref, o_ref, *, rows, kr):
    n_groups, wr, _ = _na_group_geometry(rows, kr)
    tq = NA_GROUP * GRID_W

    def body(i, carry):
        geo = []
        for u in range(NA_UNROLL):
            g = i * NA_UNROLL + u
            start = jnp.minimum(jnp.clip(g * NA_GROUP - kr // 2, 0, rows - kr), rows - wr)
            kind = jnp.where(g == 0, 0, jnp.where(g == n_groups - 1, 2, 1))
            geo.append((pl.multiple_of(g * tq, tq), pl.multiple_of(start * GRID_W, GRID_W), kind))
        ss = [_pair_scores(q_ref[pl.ds(q0, tq), :], k_ref[pl.ds(k0, wr * GRID_W), :]) + bias_ref[kind]
              for q0, k0, kind in geo]
        ms = [jnp.max(s, axis=-1, keepdims=True) for s in ss]
        ps = [jnp.exp2(s - m) for s, m in zip(ss, ms)]
        ls = [jnp.sum(p, axis=-1, keepdims=True) for p in ps]
        os_ = [_dot(p.astype(BF16), v_ref[pl.ds(k0, wr * GRID_W), :]) for p, (_, k0, _) in zip(ps, geo)]
        for (q0, _, _), o2, l in zip(geo, os_, ls):
            o_ref[pl.ds(q0, tq), :] = _pair_merge(o2 / l, tq).astype(o_ref.dtype)
        return carry

    lax.fori_loop(0, n_groups // NA_UNROLL, body, 0)


def _na_bias_table(rpb, rows, kr):
    n_groups, wr, start = _na_group_geometry(rows, kr)
    qcol = np.arange(GRID_W)
    kcol = np.arange(GRID_W)
    wstart = np.clip(qcol - NA_COLS // 2, 0, GRID_W - NA_COLS)
    col_ok = (kcol[None, :] >= wstart[:, None]) & (kcol[None, :] < wstart[:, None] + NA_COLS)
    dc = np.clip(kcol[None, :] - qcol[:, None] + NA_COLS - 1, 0, 2 * NA_COLS - 2)

    def geometry(g):
        qrow = g * NA_GROUP + np.arange(NA_GROUP)
        krow = start[g] + np.arange(wr)
        rs = np.clip(qrow - kr // 2, 0, rows - kr)
        row_ok = (krow[None, :] >= rs[:, None]) & (krow[None, :] < rs[:, None] + kr)
        dr = np.clip(krow[None, :] - qrow[:, None] + NA_ROWS - 1, 0, 2 * NA_ROWS - 2)
        return row_ok, dr

    kinds = [0, min(1, n_groups - 1), n_groups - 1]
    for g in range(1, n_groups - 1):
        assert all(np.array_equal(a, b) for a, b in zip(geometry(g), geometry(kinds[1])))
    onehot = (dc[None] == np.arange(2 * NA_COLS - 1)[:, None, None]) & col_ok[None]
    planes = jnp.einsum("hrd,dqk->hrqk", rpb * LOG2_E, jnp.asarray(onehot, F32), precision=lax.Precision.HIGHEST)
    planes = planes + jnp.asarray(np.where(col_ok, 0.0, NEG_INF), F32)
    masked = jnp.full(planes.shape[:1] + planes.shape[2:], NEG_INF, F32)
    slabs = []
    for g in kinds:
        row_ok, dr = geometry(g)
        rows_i = []
        for i in range(NA_GROUP):
            cols_j = [planes[:, int(dr[i, j])] if row_ok[i, j] else masked for j in range(wr)]
            rows_i.append(jnp.concatenate(cols_j, axis=-1))
        slab = jnp.stack(rows_i, axis=1)
        slabs.append(slab.reshape(NA_HEADS // 2, 2 * NA_GROUP * GRID_W, wr * GRID_W))
    return jnp.stack(slabs)


def _neighbourhood(proj, rpb, batch, seq):
    rows = seq // GRID_W
    kr = min(NA_ROWS, rows)
    assert rows % (NA_GROUP * NA_UNROLL) == 0 and rows >= kr + NA_GROUP
    table = _na_bias_table(rpb.astype(F32), rows, kr)
    n_pairs = NA_HEADS // 2

    def spec(base):
        return pl.BlockSpec((None, seq, LANES), lambda b, p: (b, 0, base + p))

    return pl.pallas_call(
        functools.partial(_na_kernel, rows=rows, kr=kr),
        grid=(batch, n_pairs),
        in_specs=[spec(0), spec(4), spec(8),
                  pl.BlockSpec((3, None) + table.shape[2:], lambda b, p: (0, p, 0, 0))],
        out_specs=pl.BlockSpec((None, seq, LANES), lambda b, p: (b, 0, p)),
        out_shape=jax.ShapeDtypeStruct((batch, seq, n_pairs * LANES), BF16),
        compiler_params=_params("parallel", "parallel"),
        name="neighbourhood",
    )(proj, proj, proj, table)


_ODD_W_COLS = 14 * LANES
_ODD_OUT_COLS = 28 * LANES


def _odd_proj_kernel(x_ref, g_ref, w_ref, qn_ref, wqb_ref, kvn_ref, wkb_ref, wvb_ref,
                     cos_ref, sin_ref, cosr_ref, sinr_ref, o_ref):
    hn = _rms(x_ref[...], g_ref[...]).astype(BF16)
    cos, sin = cos_ref[...], sin_ref[...]
    cosr, sinr = cosr_ref[...], sinr_ref[...]
    lane = lax.broadcasted_iota(jnp.int32, cos.shape, 1)
    first_half = (lane % HEAD_DIM) < HEAD_DIM // 2
    first_half_r = lane < MLA_NOPE + MLA_ROPE // 2
    swa_scale = HEAD_DIM ** -0.5 * LOG2_E
    mla_scale = (MLA_NOPE + MLA_ROPE) ** -0.5 * LOG2_E

    def col(j, n=1):
        return slice(j * LANES, (j + n) * LANES)

    def rope(blk):
        return _rope_block(blk, cos, sin, first_half, LANES - HEAD_DIM // 2, HEAD_DIM // 2)

    def rope_r(blk):
        return _rope_block(blk, cosr, sinr, first_half_r, LANES - MLA_ROPE // 2, MLA_ROPE // 2)

    def halves(wide):
        return wide[:, :LANES], wide[:, LANES:]

    for j in range(0, 8, 2):
        for jj, blk in zip((j, j + 1), halves(_dot(hn, w_ref[:, col(j, 2)]))):
            if jj < 4:
                blk = rope(blk) * swa_scale
            elif jj < 6:
                blk = rope(blk)
            o_ref[:, col(jj)] = blk.astype(BF16)

    latent = _dot(hn, w_ref[:, col(8, 6)])
    q_an = _rms(latent[:, :3 * LANES], qn_ref[...]).astype(BF16)
    kv_an = _rms(latent[:, 3 * LANES:5 * LANES], kvn_ref[...]).astype(BF16)
    k_pe = rope_r(latent[:, 5 * LANES:])
    for h in range(0, MLA_HEADS, 2):
        for hh, qh in zip((h, h + 1), halves(_dot(q_an, wqb_ref[:, col(h, 2)]))):
            o_ref[:, col(8 + hh)] = (rope_r(qh) * mla_scale).astype(BF16)
        for hh, kh in zip((h, h + 1), halves(_dot(kv_an, wkb_ref[:, col(h, 2)]))):
            o_ref[:, col(16 + hh)] = (kh + k_pe).astype(BF16)
    for j in range(0, MLA_HEADS // 2, 2):
        o_ref[:, col(24 + j, 2)] = _dot(kv_an, wvb_ref[:, col(j, 2)]).astype(BF16)


def _odd_proj(h2, g, w, qn, wqb, kvn, wkb, wvb, cos, sin, cosr, sinr, seq):
    m, d = h2.shape
    tm = FFN_TM
    per_seq = seq // tm
    full = lambda a: pl.BlockSpec(a.shape, lambda i: (0, 0))
    tab = pl.BlockSpec((tm, LANES), lambda i: (i % per_seq, 0))
    return pl.pallas_call(
        _odd_proj_kernel,
        grid=(m // tm,),
        in_specs=[pl.BlockSpec((tm, d), lambda i: (i, 0)), full(g), full(w), full(qn), full(wqb),
                  full(kvn), full(wkb), full(wvb), tab, tab, tab, tab],
        out_specs=pl.BlockSpec((tm, _ODD_OUT_COLS), lambda i: (i, 0)),
        out_shape=jax.ShapeDtypeStruct((m, _ODD_OUT_COLS), BF16),
        compiler_params=_params("parallel"),
        name="odd_proj",
    )(h2, g, w, qn, wqb, kvn, wkb, wvb, cos, sin, cosr, sinr)


def _odd_weights(w_in, w_qb, w_kvb):
    d = w_in.shape[0]
    c = np.cumsum([0, SWA_Q_HEADS * HEAD_DIM, SWA_KV_HEADS * HEAD_DIM, SWA_KV_HEADS * HEAD_DIM,
                   MLA_Q_RANK, MLA_KV_RANK, MLA_ROPE])

    def dup(w):
        w = w.reshape(d, SWA_KV_HEADS, 1, HEAD_DIM)
        return jnp.broadcast_to(w, (d, SWA_KV_HEADS, 2, HEAD_DIM)).reshape(d, SWA_KV_HEADS * LANES)

    w_kpe = jnp.pad(w_in[:, c[5]:c[6]], ((0, 0), (MLA_NOPE, LANES - MLA_NOPE - MLA_ROPE)))
    w = jnp.concatenate([w_in[:, c[0]:c[1]], dup(w_in[:, c[1]:c[2]]), dup(w_in[:, c[2]:c[3]]),
                         w_in[:, c[3]:c[4]], w_in[:, c[4]:c[5]], w_kpe], axis=1).astype(BF16)
    qk = MLA_NOPE + MLA_ROPE
    wqb = jnp.pad(w_qb.reshape(MLA_Q_RANK, MLA_HEADS, qk), ((0, 0), (0, 0), (0, LANES - qk)))
    wqb = wqb.reshape(MLA_Q_RANK, MLA_HEADS * LANES).astype(BF16)
    kv = w_kvb.reshape(MLA_KV_RANK, MLA_HEADS, MLA_NOPE + MLA_V)
    wkb = jnp.pad(kv[:, :, :MLA_NOPE], ((0, 0), (0, 0), (0, LANES - MLA_NOPE)))
    wkb = wkb.reshape(MLA_KV_RANK, MLA_HEADS * LANES).astype(BF16)
    wvb = kv[:, :, MLA_NOPE:].reshape(MLA_KV_RANK, MLA_HEADS * MLA_V).astype(BF16)
    return w, wqb, wkb, wvb


def _swa_kernel(sink_ref, q_ref, k_ref, v_ref, bias_ref, o_ref, *, seq, half, tq, win):
    pair = pl.program_id(1)
    first = lax.broadcasted_iota(jnp.int32, (2 * tq, 1), 0) < tq
    sink = jnp.where(first, sink_ref[2 * pair], sink_ref[2 * pair + 1]) * LOG2_E

    def body(i, carry):
        geo = []
        for g in range(SWA_UNROLL):
            q0 = pl.multiple_of((i * SWA_UNROLL + g) * tq, tq)
            geo.append((q0,) + _band_window(q0, tq, half, seq, win))
        ss = [_pair_scores(q_ref[pl.ds(q0, tq), :], k_ref[pl.ds(ks, win), :]) + bias_ref[kind]
              for q0, ks, kind in geo]
        ms = [jnp.maximum(jnp.max(s, axis=-1, keepdims=True), sink) for s in ss]
        ps = [jnp.exp2(s - m) for s, m in zip(ss, ms)]
        ls = [jnp.sum(p, axis=-1, keepdims=True) + jnp.exp2(sink - m) for p, m in zip(ps, ms)]
        os_ = [_dot(p.astype(BF16), v_ref[pl.ds(ks, win), :]) for p, (_, ks, _) in zip(ps, geo)]
        for (q0, _, _), o2, l in zip(geo, os_, ls):
            o_ref[pl.ds(q0, tq), :] = _pair_merge(o2 / l, tq).astype(o_ref.dtype)
        return carry

    lax.fori_loop(0, seq // (tq * SWA_UNROLL), body, 0)


def _swa(proj, sink, batch, seq):
    tq = SWA_TQ
    win = tq + 2 * SWA_HALF
    assert seq % (tq * SWA_UNROLL) == 0 and tq % SWA_HALF == 0 and seq >= win
    n_pairs = SWA_Q_HEADS // 2
    pairs_per_kv = n_pairs // SWA_KV_HEADS
    bias = _band_bias(tq, SWA_HALF, win)
    return pl.pallas_call(
        functools.partial(_swa_kernel, seq=seq, half=SWA_HALF, tq=tq, win=win),
        grid=(batch, n_pairs),
        in_specs=[pl.BlockSpec(memory_space=pltpu.SMEM),
                  pl.BlockSpec((None, seq, LANES), lambda b, p: (b, 0, p)),
                  pl.BlockSpec((None, seq, LANES), lambda b, p: (b, 0, 4 + p // pairs_per_kv)),
                  pl.BlockSpec((None, seq, LANES), lambda b, p: (b, 0, 6 + p // pairs_per_kv)),
                  pl.BlockSpec(bias.shape, lambda b, p: (0, 0, 0))],
        out_specs=pl.BlockSpec((None, seq, LANES), lambda b, p: (b, 0, p)),
        out_shape=jax.ShapeDtypeStruct((batch, seq, n_pairs * LANES), BF16),
        compiler_params=_params("parallel", "parallel"),
        name="swa",
    )(sink, proj, proj, proj, bias)


def _mla_kernel(q_ref, k_ref, v_ref, o_ref, s_ref, p_ref, *, seq, tk):
    tq = MLA_TQ
    groups = tk // LANES
    chunks = [slice(c * tk, (c + 1) * tk) for c in range(seq // tk)]
    n_units = 2 * (q_ref.shape[1] // tq)
    first = _lane_is_first_head((tq, LANES))

    def scores(u):
        t, h = divmod(u, 2)
        q = q_ref[0, t * tq:(t + 1) * tq, h * LANES:(h + 1) * LANES]
        m_l = jnp.full((tq, LANES), NEG_INF, F32)
        for ck in chunks:
            s = _dot_nt(q, k_ref[0, ck, h * LANES:(h + 1) * LANES])
            s_ref[u % 2, :, ck] = s
            for g in range(groups):
                m_l = jnp.maximum(m_l, s[:, g * LANES:(g + 1) * LANES])
        return jnp.max(m_l, axis=-1, keepdims=True)

    def probs(u, m):
        l_l = jnp.zeros((tq, LANES), F32)
        for ck in chunks:
            p = jnp.exp2(s_ref[u % 2, :, ck] - m)
            for g in range(groups):
                l_l = l_l + p[:, g * LANES:(g + 1) * LANES]
            p_ref[u % 2, :, ck] = p.astype(BF16)
        return jnp.sum(l_l, axis=-1, keepdims=True)

    m = scores(0)
    out_a = None
    for u in range(n_units):
        m_next = scores(u + 1) if u + 1 < n_units else None
        l = probs(u, m)
        out = _dot(p_ref[u % 2], v_ref[0]) / l
        if u % 2 == 0:
            out_a = out
        else:
            t = u // 2
            o_ref[0, t * tq:(t + 1) * tq, :] = jnp.where(first, out_a, out).astype(o_ref.dtype)
        m = m_next


def _mla(proj, batch, seq):
    tq = MLA_TQ * MLA_SUB
    assert seq % tq == 0 and seq % MLA_TK == 0
    n_pairs = MLA_HEADS // 2
    return pl.pallas_call(
        functools.partial(_mla_kernel, seq=seq, tk=MLA_TK),
        grid=(batch, n_pairs, seq // tq),
        in_specs=[pl.BlockSpec((1, tq, 2 * LANES), lambda b, p, i: (b, i, 4 + p)),
                  pl.BlockSpec((1, seq, 2 * LANES), lambda b, p, i: (b, 0, 8 + p)),
                  pl.BlockSpec((1, seq, LANES), lambda b, p, i: (b, 0, 24 + p))],
        out_specs=pl.BlockSpec((1, tq, LANES), lambda b, p, i: (b, i, p)),
        out_shape=jax.ShapeDtypeStruct((batch, seq, n_pairs * LANES), BF16),
        scratch_shapes=[pltpu.VMEM((2, MLA_TQ, seq), F32), pltpu.VMEM((2, MLA_TQ, seq), BF16)],
        compiler_params=_params("parallel", "parallel", "parallel"),
        name="mla",
    )(proj, proj, proj)


def _rope_tables(seq):
    pos = jnp.arange(seq, dtype=F32)

    def tables(dim):
        inv_freq = ROPE_THETA ** (-jnp.arange(0, dim, 2, dtype=F32) / dim)
        ang = pos[:, None] * inv_freq[None, :]
        return jnp.cos(ang), jnp.sin(ang)

    cos, sin = tables(HEAD_DIM)
    reps = LANES // HEAD_DIM
    cos_t = jnp.tile(jnp.concatenate([cos, cos], axis=1), (1, reps))
    sin_t = jnp.tile(jnp.concatenate([-sin, sin], axis=1), (1, reps))
    cos_r, sin_r = tables(MLA_ROPE)
    pad_l, pad_r = MLA_NOPE, LANES - MLA_NOPE - MLA_ROPE
    cos_rt = jnp.pad(jnp.concatenate([cos_r, cos_r], axis=1), ((0, 0), (pad_l, pad_r)), constant_values=1.0)
    sin_rt = jnp.pad(jnp.concatenate([-sin_r, sin_r], axis=1), ((0, 0), (pad_l, pad_r)))
    return cos_t, sin_t, cos_rt, sin_rt


def kernel(x, ffn1_norm, ffn1_w1, ffn1_w3, ffn1_w2, mix_norm, ffn2_norm, ffn2_w1, ffn2_w3, ffn2_w2, even_w_in, even_w_out, na_rel_bias, odd_w_in, odd_w_out, swa_sink, mla_q_norm, mla_w_qb, mla_kv_norm, mla_w_kvb, final_norm):
    batch, seq, d = x.shape
    depth = ffn1_norm.shape[0]
    assert d == D_MODEL and seq % FFN_TM == 0 and seq % (GRID_W * NA_ROWS) == 0
    cos_t, sin_t, cos_rt, sin_rt = _rope_tables(seq)
    h = x.reshape(batch * seq, d)
    row = lambda v: v.reshape(1, -1).astype(F32)
    fg = row(final_norm)

    ffn1_w = _ffn_weights(ffn1_w1, ffn1_w3, ffn1_w2)
    ffn2_w = _ffn_weights(ffn2_w1, ffn2_w3, ffn2_w2)
    for i in range(depth):
        j = i // 2
        h = _ffn(h, row(ffn1_norm[i]), *ffn1_w, i, fg, False)
        if i % 2 == 0:
            dil_proj, na_proj = _even_proj(h, row(mix_norm[i]), even_w_in[j].astype(BF16), cos_t, sin_t, seq)
            oa = _dilated(dil_proj.reshape(batch, seq, -1), batch, seq).reshape(batch * seq, -1)
            on = _neighbourhood(na_proj.reshape(batch, seq, -1), na_rel_bias[j], batch, seq).reshape(batch * seq, -1)
            mix = (oa, on, even_w_out[j].astype(BF16))
        else:
            w, wqb, wkb, wvb = _odd_weights(odd_w_in[j], mla_w_qb[j], mla_w_kvb[j])
            proj = _odd_proj(h, row(mix_norm[i]), w, row(mla_q_norm[j]), wqb, row(mla_kv_norm[j]), wkb, wvb,
                             cos_t, sin_t, cos_rt, sin_rt, seq)
            proj = proj.reshape(batch, seq, -1)
            oc = _swa(proj, swa_sink[j].astype(F32), batch, seq).reshape(batch * seq, -1)
            od = _mla(proj, batch, seq).reshape(batch * seq, -1)
            mix = (oc, od, odd_w_out[j].astype(BF16))
        h = _ffn(h, row(ffn2_norm[i]), *ffn2_w, i, fg, i == depth - 1, mix)
    return h.reshape(batch, seq, d)
```

```python
import functools

import jax
import jax.numpy as jnp
import numpy as np
from jax import lax
from jax.experimental import pallas as pl
from jax.experimental.pallas import tpu as pltpu

D_MODEL = 1024
D_FF = 2816
HEAD_DIM = 64
ROPE_THETA = 10000.0
NORM_EPS = 1e-6
NEG_INF = -1e30
LOG2_E = 1.4426950408889634

DIL_HEADS = 8
DIL_CONFIGS = ((128, 1), (512, 4), (2048, 16))
NA_HEADS = 8
GRID_W = 64
NA_ROWS = 8
NA_COLS = 16

SWA_Q_HEADS = 8
SWA_KV_HEADS = 2
SWA_HALF = 128
MLA_HEADS = 8
MLA_Q_RANK = 384
MLA_KV_RANK = 256
MLA_NOPE = 64
MLA_ROPE = 32
MLA_V = 64

LANES = 128
VMEM_LIMIT = 56 * 1024 * 1024

FFN_TM = 512
FFN_TF = 256
MLA_TQ = 256
MLA_SUB = 4
MLA_TK = 512
DIL_TQ = 128
DIL_UNROLL = 8
SWA_TQ = 128
SWA_UNROLL = 4
NA_GROUP = 1
NA_UNROLL = 8

BF16 = jnp.bfloat16
F32 = jnp.float32


def _params(*sem):
    return pltpu.CompilerParams(dimension_semantics=sem, vmem_limit_bytes=VMEM_LIMIT)


def _rms(x, g):
    ms = jnp.mean(x * x, axis=-1, keepdims=True)
    return x * lax.rsqrt(ms + NORM_EPS) * g


def _dot(a, b):
    return jnp.dot(a, b, preferred_element_type=F32)


def _dot_nt(a, b):
    return lax.dot_general(a, b, (((1,), (1,)), ((), ())), preferred_element_type=F32)


def _lane_is_first_head(shape):
    return lax.broadcasted_iota(jnp.int32, shape, len(shape) - 1) < HEAD_DIM


def _rope_block(x, cos, sin_signed, first_half, up, down):
    partner = jnp.where(first_half, pltpu.roll(x, up, 1), pltpu.roll(x, down, 1))
    return x * cos + partner * sin_signed


def _ffn_kernel(*refs, mix, final):
    if mix:
        x_ref, oc_ref, od_ref, wa_ref, wb_ref, g_ref, w1_ref, w3_ref, w2_ref, fg_ref, o_ref, a_ref = refs
        x = x_ref[...] + _dot(oc_ref[...], wa_ref[...]) + _dot(od_ref[...], wb_ref[...])
    else:
        x_ref, g_ref, w1_ref, w3_ref, w2_ref, fg_ref, o_ref, a_ref = refs
        x = x_ref[...]
    hn = _rms(x, g_ref[...]).astype(BF16)
    for c in range(a_ref.shape[1] // FFN_TF):
        cols = slice(c * FFN_TF, (c + 1) * FFN_TF)
        h1 = _dot(hn, w1_ref[:, cols])
        h3 = _dot(hn, w3_ref[:, cols])
        a_ref[:, cols] = (h1 * jax.nn.sigmoid(h1) * h3).astype(BF16)
    y = x + 0.5 * _dot(a_ref[...], w2_ref[...])
    if final:
        y = _rms(y, fg_ref[...])
    o_ref[...] = y


def _ffn(x2, g, w1, w3, w2, layer, fg, final, mix=None):
    m, d = x2.shape
    f = w1.shape[2]
    assert f % FFN_TF == 0
    tm = FFN_TM
    const = lambda i: (0, 0)
    of_layer = lambda i: (layer, 0, 0)
    resident = dict(pipeline_mode=pl.Buffered(1))
    mix_args, mix_specs = (), []
    if mix is not None:
        oc, od, w_out = mix
        half_w = w_out.shape[0] // 2
        mix_args = (oc, od, w_out[:half_w], w_out[half_w:])
        mix_specs = [pl.BlockSpec((tm, half_w), lambda i: (i, 0))] * 2 + [pl.BlockSpec((half_w, d), const, **resident)] * 2
    return pl.pallas_call(
        functools.partial(_ffn_kernel, mix=mix is not None, final=final),
        grid=(m // tm,),
        in_specs=[pl.BlockSpec((tm, d), lambda i: (i, 0))] + mix_specs + [
            pl.BlockSpec((1, d), const),
            pl.BlockSpec((None, d, f), of_layer, **resident),
            pl.BlockSpec((None, d, f), of_layer, **resident),
            pl.BlockSpec((None, f, d), of_layer, **resident),
            pl.BlockSpec((1, d), const),
        ],
        out_specs=pl.BlockSpec((tm, d), lambda i: (i, 0)),
        out_shape=jax.ShapeDtypeStruct((m, d), F32),
        scratch_shapes=[pltpu.VMEM((tm, f), BF16)],
        compiler_params=_params("parallel"),
        name="ffn",
    )(x2, *mix_args, g, w1, w3, w2, fg)


def _ffn_weights(w1, w3, w2):
    return w1.astype(BF16), w3.astype(BF16), w2.astype(BF16)


_EVEN_DIL_BLOCKS = 12
_EVEN_Q_BLOCKS = tuple(range(0, 4)) + tuple(range(12, 16))
_EVEN_ROPE_BLOCKS = 8


def _even_proj_kernel(x_ref, g_ref, w_ref, cos_ref, sin_ref, dil_ref, na_ref, *, n_blocks):
    hn = _rms(x_ref[...], g_ref[...]).astype(BF16)
    cos = cos_ref[...]
    sin = sin_ref[...]
    lane = lax.broadcasted_iota(jnp.int32, cos.shape, 1)
    first_half = (lane % HEAD_DIM) < HEAD_DIM // 2
    scale = HEAD_DIM ** -0.5 * LOG2_E
    for j2 in range(0, n_blocks, 2):
        wide = _dot(hn, w_ref[:, j2 * LANES:(j2 + 2) * LANES])
        for j in (j2, j2 + 1):
            blk = wide[:, (j - j2) * LANES:(j - j2 + 1) * LANES]
            if j < _EVEN_ROPE_BLOCKS:
                blk = _rope_block(blk, cos, sin, first_half, LANES - HEAD_DIM // 2, HEAD_DIM // 2)
            if j in _EVEN_Q_BLOCKS:
                blk = blk * scale
            if j < _EVEN_DIL_BLOCKS:
                dil_ref[:, j * LANES:(j + 1) * LANES] = blk
            else:
                jn = j - _EVEN_DIL_BLOCKS
                na_ref[:, jn * LANES:(jn + 1) * LANES] = blk.astype(BF16)


def _even_proj(h2, g, w_in, cos, sin, seq):
    m, d = h2.shape
    n_out = w_in.shape[1]
    n_dil = _EVEN_DIL_BLOCKS * LANES
    tm = FFN_TM
    per_seq = seq // tm
    return pl.pallas_call(
        functools.partial(_even_proj_kernel, n_blocks=n_out // LANES),
        grid=(m // tm,),
        in_specs=[
            pl.BlockSpec((tm, d), lambda i: (i, 0)),
            pl.BlockSpec((1, d), lambda i: (0, 0)),
            pl.BlockSpec((d, n_out), lambda i: (0, 0)),
            pl.BlockSpec((tm, LANES), lambda i: (i % per_seq, 0)),
            pl.BlockSpec((tm, LANES), lambda i: (i % per_seq, 0)),
        ],
        out_specs=[pl.BlockSpec((tm, n_dil), lambda i: (i, 0)),
                   pl.BlockSpec((tm, n_out - n_dil), lambda i: (i, 0))],
        out_shape=[jax.ShapeDtypeStruct((m, n_dil), F32),
                   jax.ShapeDtypeStruct((m, n_out - n_dil), BF16)],
        compiler_params=_params("parallel"),
        name="even_proj",
    )(h2, g, w_in, cos, sin)


def _pair_scores(q, k):
    first = _lane_is_first_head(q.shape)
    zero = jnp.zeros_like(q)
    q2 = jnp.concatenate([jnp.where(first, q, zero), jnp.where(first, zero, q)], axis=0)
    return _dot_nt(q2, k)


def _pair_merge(x2, tq):
    first = _lane_is_first_head((tq, LANES))
    return jnp.where(first, x2[:tq], x2[tq:])


def _band_bias(tq, half, win):
    row = np.arange(2 * tq)[:, None] % tq
    col = np.arange(win)[None, :]
    kinds = [np.where(np.abs(col - row - k * half) <= half, 0.0, NEG_INF) for k in range(3)]
    return jnp.asarray(np.stack(kinds), F32)


def _band_window(q0, tq, half, length, win):
    ks = pl.multiple_of(jnp.clip(q0 - half, 0, length - win), half)
    return ks, (q0 - ks) // half


def _pair_softmax(q, k, bias):
    s = _pair_scores(q, k) + bias
    m = jnp.max(s, axis=-1, keepdims=True)
    p = jnp.exp2(s - m)
    return p, m, jnp.sum(p, axis=-1, keepdims=True)


def _dilated_kernel(q_ref, k_ref, v_ref, *refs, seq, configs, tiles):
    n = len(configs)
    bias_refs, o_ref = refs[:n], refs[n]
    qs_ref, ks_ref, vs_ref = refs[n + 1:n + 4]
    ob_refs, lb_refs = refs[n + 4:2 * n + 4], refs[2 * n + 4:3 * n + 4]
    stage_refs = refs[3 * n + 4:]
    copies = {1: (q_ref, k_ref, v_ref)}
    for c, ((half, dil), (tq, win)) in enumerate(zip(configs, tiles)):
        length = seq // dil
        blocks = length // tq
        base_dil = max(d for d in copies if dil % d == 0)
        step = dil // base_dil
        keep = dil > 1 and any(d2 > dil and d2 % dil == 0 for _, d2 in configs)
        if keep:
            copies[dil] = stage_refs
        for a, (src, dst) in enumerate(zip(copies[base_dil], (qs_ref, ks_ref, vs_ref))):
            for r in range(dil):
                start = (r % base_dil) * (seq // base_dil) + r // base_dil
                rows = pl.ds(start, length, stride=step) if step > 1 else pl.ds(start, length)
                x = src[rows, :]
                dst[r * length:(r + 1) * length, :] = x.astype(BF16)
                if keep:
                    stage_refs[a][r * length:(r + 1) * length, :] = x

        def body(i, carry, c=c, half=half, dil=dil, tq=tq, win=win, length=length, blocks=blocks):
            geo = []
            for g in range(DIL_UNROLL):
                n = i * DIL_UNROLL + g
                r = n // blocks
                q0 = pl.multiple_of((n % blocks) * tq, tq)
                ks, kind = _band_window(q0, tq, half, length, win)
                geo.append((r, q0, pl.multiple_of(r * length, tq) + ks, kind))
            ss = [_pair_scores(qs_ref[pl.ds(pl.multiple_of(r * length, tq) + q0, tq), :], ks_ref[pl.ds(kb, win), :])
                  + bias_refs[c][kind] for r, q0, kb, kind in geo]
            ms = [jnp.max(s, axis=-1, keepdims=True) for s in ss]
            ps = [jnp.exp2(s - m) for s, m in zip(ss, ms)]
            ls = [jnp.sum(p, axis=-1, keepdims=True) for p in ps]
            os_ = [_dot(p.astype(BF16), vs_ref[pl.ds(kb, win), :]) for p, (_, _, kb, _) in zip(ps, geo)]
            for (r, q0, _, _), o2, m, l in zip(geo, os_, ms, ls):
                o2 = o2 / l
                lse2 = jnp.broadcast_to(m + jnp.log2(l), o2.shape)
                rows = pl.ds(r + dil * q0, tq, stride=dil) if dil > 1 else pl.ds(q0, tq)
                ob_refs[c][rows, :] = _pair_merge(o2, tq)
                lb_refs[c][rows, :] = _pair_merge(lse2, tq)
            return carry

        lax.fori_loop(0, seq // (tq * DIL_UNROLL), body, 0)

    chunk = 512

    def merge(i, carry):
        rows = pl.ds(pl.multiple_of(i * chunk, chunk), chunk)
        lses = [lb[rows, :] for lb in lb_refs]
        mx = functools.reduce(jnp.maximum, lses)
        es = [jnp.exp2(l - mx) for l in lses]
        num = sum(e * ob[rows, :] for e, ob in zip(es, ob_refs))
        o_ref[rows, :] = (num / sum(es)).astype(o_ref.dtype)
        return carry

    lax.fori_loop(0, seq // chunk, merge, 0)


def _dilated(dil_proj, batch, seq):
    n_pairs = DIL_HEADS // 2
    configs = tuple((window // 2 // dil, dil) for window, dil in DIL_CONFIGS)
    tiles = []
    for half, dil in configs:
        tq = min(DIL_TQ, seq // dil)
        tiles.append((tq, min(tq + 2 * half, seq // dil)))
        assert seq % (dil * tq) == 0 and tq % half == 0
    biases = [_band_bias(tq, half, win) for (half, _), (tq, win) in zip(configs, tiles)]

    def spec(base):
        return pl.BlockSpec((None, seq, LANES), lambda b, p: (b, 0, base + p))

    return pl.pallas_call(
        functools.partial(_dilated_kernel, seq=seq, configs=configs, tiles=tuple(tiles)),
        grid=(batch, n_pairs),
        in_specs=[spec(0), spec(4), spec(8)] + [pl.BlockSpec(b.shape, lambda b_, p: (0, 0, 0)) for b in biases],
        out_specs=pl.BlockSpec((None, seq, LANES), lambda b, p: (b, 0, p)),
        out_shape=jax.ShapeDtypeStruct((batch, seq, n_pairs * LANES), BF16),
        scratch_shapes=[pltpu.VMEM((seq, LANES), BF16)] * 3 + [pltpu.VMEM((seq, LANES), F32)] * (2 * len(configs) + 3),
        compiler_params=_params("parallel", "parallel"),
        name="dilated",
    )(dil_proj, dil_proj, dil_proj, *biases)


def _na_group_geometry(rows, kr):
    n_groups = rows // NA_GROUP
    wr = -(-(kr + NA_GROUP - 1) * GRID_W // LANES) * LANES // GRID_W
    g = np.arange(n_groups)
    start = np.minimum(np.clip(g * NA_GROUP - kr // 2, 0, rows - kr), rows - wr)
    return n_groups, wr, start


def _na_kernel(start_ref, kind_ref, q_ref, k_ref, v_ref, bias_ref, o_ref, *, rows, kr):
    n_groups, wr, _ = _na_group_geometry(rows, kr)
    tq = NA_GROUP * GRID_W

    def body(i, carry):
        geo = []
        for u in range(NA_UNROLL):
            g = i * NA_UNROLL + u
            geo.append((pl.multiple_of(g * tq, tq), pl.multiple_of(start_ref[g] * GRID_W, GRID_W), kind_ref[g]))
        ss = [_pair_scores(q_ref[pl.ds(q0, tq), :], k_ref[pl.ds(k0, wr * GRID_W), :]) + bias_ref[kind]
              for q0, k0, kind in geo]
        ms = [jnp.max(s, axis=-1, keepdims=True) for s in ss]
        ps = [jnp.exp2(s - m) for s, m in zip(ss, ms)]
        ls = [jnp.sum(p, axis=-1, keepdims=True) for p in ps]
        os_ = [_dot(p.astype(BF16), v_ref[pl.ds(k0, wr * GRID_W), :]) for p, (_, k0, _) in zip(ps, geo)]
        for (q0, _, _), o2, l in zip(geo, os_, ls):
            o_ref[pl.ds(q0, tq), :] = _pair_merge(o2 / l, tq).astype(o_ref.dtype)
        return carry

    lax.fori_loop(0, n_groups // NA_UNROLL, body, 0)


def _na_bias_table(rpb, rows, kr):
    n_groups, wr, start = _na_group_geometry(rows, kr)
    qcol = np.arange(GRID_W)
    kcol = np.arange(GRID_W)
    wstart = np.clip(qcol - NA_COLS // 2, 0, GRID_W - NA_COLS)
    col_ok = (kcol[None, :] >= wstart[:, None]) & (kcol[None, :] < wstart[:, None] + NA_COLS)
    dc = np.clip(kcol[None, :] - qcol[:, None] + NA_COLS - 1, 0, 2 * NA_COLS - 2)

    def geometry(g):
        qrow = g * NA_GROUP + np.arange(NA_GROUP)
        krow = start[g] + np.arange(wr)
        rs = np.clip(qrow - kr // 2, 0, rows - kr)
        row_ok = (krow[None, :] >= rs[:, None]) & (krow[None, :] < rs[:, None] + kr)
        dr = np.where(row_ok, krow[None, :] - qrow[:, None] + NA_ROWS - 1, -1)
        return dr

    patterns, kind_of = [], []
    for g in range(n_groups):
        dr = geometry(g)
        match = [k for k, p in enumerate(patterns) if np.array_equal(p, dr)]
        if not match:
            patterns.append(dr)
        kind_of.append(match[0] if match else len(patterns) - 1)
    onehot = (dc[None] == np.arange(2 * NA_COLS - 1)[:, None, None]) & col_ok[None]
    planes = jnp.einsum("hrd,dqk->hrqk", rpb * LOG2_E, jnp.asarray(onehot, F32), precision=lax.Precision.HIGHEST)
    planes = planes + jnp.asarray(np.where(col_ok, 0.0, NEG_INF), F32)
    masked = jnp.full(planes.shape[:1] + planes.shape[2:], NEG_INF, F32)
    slabs = []
    for dr in patterns:
        rows_i = []
        for i in range(NA_GROUP):
            cols_j = [planes[:, int(dr[i, j])] if dr[i, j] >= 0 else masked for j in range(wr)]
            rows_i.append(jnp.concatenate(cols_j, axis=-1))
        slab = jnp.stack(rows_i, axis=1)
        slabs.append(slab.reshape(NA_HEADS // 2, 2 * NA_GROUP * GRID_W, wr * GRID_W))
    table = jnp.stack(slabs)
    return table, jnp.asarray(start, jnp.int32), jnp.asarray(kind_of, jnp.int32)


def _neighbourhood(proj, rpb, batch, seq):
    rows = seq // GRID_W
    kr = min(NA_ROWS, rows)
    assert rows % (NA_GROUP * NA_UNROLL) == 0 and rows >= kr + NA_GROUP
    table, start, kind_of = _na_bias_table(rpb.astype(F32), rows, kr)
    n_pairs = NA_HEADS // 2

    def spec(base):
        return pl.BlockSpec((None, seq, LANES), lambda b, p: (b, 0, base + p))

    smem = pl.BlockSpec(memory_space=pltpu.SMEM)
    return pl.pallas_call(
        functools.partial(_na_kernel, rows=rows, kr=kr),
        grid=(batch, n_pairs),
        in_specs=[smem, smem, spec(0), spec(4), spec(8),
                  pl.BlockSpec((table.shape[0], None) + table.shape[2:], lambda b, p: (0, p, 0, 0))],
        out_specs=pl.BlockSpec((None, seq, LANES), lambda b, p: (b, 0, p)),
        out_shape=jax.ShapeDtypeStruct((batch, seq, n_pairs * LANES), BF16),
        compiler_params=_params("parallel", "parallel"),
        name="neighbourhood",
    )(start, kind_of, proj, proj, proj, table)


_ODD_W_COLS = 14 * LANES
_ODD_OUT_COLS = 28 * LANES


def _odd_proj_kernel(x_ref, g_ref, w_ref, qn_ref, wqb_ref, kvn_ref, wkb_ref, wvb_ref,
                     cos_ref, sin_ref, cosr_ref, sinr_ref, o_ref):
    hn = _rms(x_ref[...], g_ref[...]).astype(BF16)
    cos, sin = cos_ref[...], sin_ref[...]
    cosr, sinr = cosr_ref[...], sinr_ref[...]
    lane = lax.broadcasted_iota(jnp.int32, cos.shape, 1)
    first_half = (lane % HEAD_DIM) < HEAD_DIM // 2
    first_half_r = lane < MLA_NOPE + MLA_ROPE // 2
    swa_scale = HEAD_DIM ** -0.5 * LOG2_E
    mla_scale = (MLA_NOPE + MLA_ROPE) ** -0.5 * LOG2_E

    def col(j, n=1):
        return slice(j * LANES, (j + n) * LANES)

    def rope(blk):
        return _rope_block(blk, cos, sin, first_half, LANES - HEAD_DIM // 2, HEAD_DIM // 2)

    def rope_r(blk):
        return _rope_block(blk, cosr, sinr, first_half_r, LANES - MLA_ROPE // 2, MLA_ROPE // 2)

    def halves(wide):
        return wide[:, :LANES], wide[:, LANES:]

    latent = _dot(hn, w_ref[:, col(8, 6)])
    for j in range(0, 6, 2):
        for jj, blk in zip((j, j + 1), halves(_dot(hn, w_ref[:, col(j, 2)]))):
            blk = rope(blk) * swa_scale if jj < 4 else rope(blk)
            o_ref[:, col(jj)] = blk.astype(BF16)

    q_an = _rms(latent[:, :3 * LANES], qn_ref[...]).astype(BF16)
    kv_an = _rms(latent[:, 3 * LANES:5 * LANES], kvn_ref[...]).astype(BF16)
    k_pe = rope_r(latent[:, 5 * LANES:])
    for h in range(0, MLA_HEADS, 2):
        for hh, qh in zip((h, h + 1), halves(_dot(q_an, wqb_ref[:, col(h, 2)]))):
            o_ref[:, col(8 + hh)] = (rope_r(qh) * mla_scale).astype(BF16)
    for h in range(0, MLA_HEADS, 2):
        for hh, kh in zip((h, h + 1), halves(_dot(kv_an, wkb_ref[:, col(h, 2)]))):
            o_ref[:, col(16 + hh)] = (kh + k_pe).astype(BF16)
    o_ref[:, col(6, 2)] = _dot(hn, w_ref[:, col(6, 2)]).astype(BF16)
    for j in range(0, MLA_HEADS // 2, 2):
        o_ref[:, col(24 + j, 2)] = _dot(kv_an, wvb_ref[:, col(j, 2)]).astype(BF16)


def _odd_proj(h2, g, w, qn, wqb, kvn, wkb, wvb, cos, sin, cosr, sinr, seq):
    m, d = h2.shape
    tm = FFN_TM
    per_seq = seq // tm
    full = lambda a: pl.BlockSpec(a.shape, lambda i: (0, 0))
    tab = pl.BlockSpec((tm, LANES), lambda i: (i % per_seq, 0))
    return pl.pallas_call(
        _odd_proj_kernel,
        grid=(m // tm,),
        in_specs=[pl.BlockSpec((tm, d), lambda i: (i, 0)), full(g), full(w), full(qn), full(wqb),
                  full(kvn), full(wkb), full(wvb), tab, tab, tab, tab],
        out_specs=pl.BlockSpec((tm, _ODD_OUT_COLS), lambda i: (i, 0)),
        out_shape=jax.ShapeDtypeStruct((m, _ODD_OUT_COLS), BF16),
        compiler_params=_params("parallel"),
        name="odd_proj",
    )(h2, g, w, qn, wqb, kvn, wkb, wvb, cos, sin, cosr, sinr)


def _odd_weights(w_in, w_qb, w_kvb):
    d = w_in.shape[0]
    c = np.cumsum([0, SWA_Q_HEADS * HEAD_DIM, SWA_KV_HEADS * HEAD_DIM, SWA_KV_HEADS * HEAD_DIM,
                   MLA_Q_RANK, MLA_KV_RANK, MLA_ROPE])

    def dup(w):
        w = w.reshape(d, SWA_KV_HEADS, 1, HEAD_DIM)
        return jnp.broadcast_to(w, (d, SWA_KV_HEADS, 2, HEAD_DIM)).reshape(d, SWA_KV_HEADS * LANES)

    w_kpe = jnp.pad(w_in[:, c[5]:c[6]], ((0, 0), (MLA_NOPE, LANES - MLA_NOPE - MLA_ROPE)))
    w = jnp.concatenate([w_in[:, c[0]:c[1]], dup(w_in[:, c[1]:c[2]]), dup(w_in[:, c[2]:c[3]]),
                         w_in[:, c[3]:c[4]], w_in[:, c[4]:c[5]], w_kpe], axis=1).astype(BF16)
    qk = MLA_NOPE + MLA_ROPE
    wqb = jnp.pad(w_qb.reshape(MLA_Q_RANK, MLA_HEADS, qk), ((0, 0), (0, 0), (0, LANES - qk)))
    wqb = wqb.reshape(MLA_Q_RANK, MLA_HEADS * LANES).astype(BF16)
    kv = w_kvb.reshape(MLA_KV_RANK, MLA_HEADS, MLA_NOPE + MLA_V)
    wkb = jnp.pad(kv[:, :, :MLA_NOPE], ((0, 0), (0, 0), (0, LANES - MLA_NOPE)))
    wkb = wkb.reshape(MLA_KV_RANK, MLA_HEADS * LANES).astype(BF16)
    wvb = kv[:, :, MLA_NOPE:].reshape(MLA_KV_RANK, MLA_HEADS * MLA_V).astype(BF16)
    return w, wqb, wkb, wvb


def _swa_kernel(sink_ref, q_ref, k_ref, v_ref, bias_ref, o_ref, *, seq, half, tq, win):
    pair = pl.program_id(1)
    first = lax.broadcasted_iota(jnp.int32, (2 * tq, 1), 0) < tq
    sink = jnp.where(first, sink_ref[2 * pair], sink_ref[2 * pair + 1]) * LOG2_E

    def body(i, carry):
        geo = []
        for g in range(SWA_UNROLL):
            q0 = pl.multiple_of((i * SWA_UNROLL + g) * tq, tq)
            geo.append((q0,) + _band_window(q0, tq, half, seq, win))
        ss = [_pair_scores(q_ref[pl.ds(q0, tq), :], k_ref[pl.ds(ks, win), :]) + bias_ref[kind]
              for q0, ks, kind in geo]
        ms = [jnp.maximum(jnp.max(s, axis=-1, keepdims=True), sink) for s in ss]
        ps = [jnp.exp2(s - m) for s, m in zip(ss, ms)]
        ls = [jnp.sum(p, axis=-1, keepdims=True) + jnp.exp2(sink - m) for p, m in zip(ps, ms)]
        os_ = [_dot(p.astype(BF16), v_ref[pl.ds(ks, win), :]) for p, (_, ks, _) in zip(ps, geo)]
        for (q0, _, _), o2, l in zip(geo, os_, ls):
            o_ref[pl.ds(q0, tq), :] = _pair_merge(o2 / l, tq).astype(o_ref.dtype)
        return carry

    lax.fori_loop(0, seq // (tq * SWA_UNROLL), body, 0)


def _swa(proj, sink, batch, seq):
    tq = SWA_TQ
    win = tq + 2 * SWA_HALF
    assert seq % (tq * SWA_UNROLL) == 0 and tq % SWA_HALF == 0 and seq >= win
    n_pairs = SWA_Q_HEADS // 2
    pairs_per_kv = n_pairs // SWA_KV_HEADS
    bias = _band_bias(tq, SWA_HALF, win)
    return pl.pallas_call(
        functools.partial(_swa_kernel, seq=seq, half=SWA_HALF, tq=tq, win=win),
        grid=(batch, n_pairs),
        in_specs=[pl.BlockSpec(memory_space=pltpu.SMEM),
                  pl.BlockSpec((None, seq, LANES), lambda b, p: (b, 0, p)),
                  pl.BlockSpec((None, seq, LANES), lambda b, p: (b, 0, 4 + p // pairs_per_kv)),
                  pl.BlockSpec((None, seq, LANES), lambda b, p: (b, 0, 6 + p // pairs_per_kv)),
                  pl.BlockSpec(bias.shape, lambda b, p: (0, 0, 0))],
        out_specs=pl.BlockSpec((None, seq, LANES), lambda b, p: (b, 0, p)),
        out_shape=jax.ShapeDtypeStruct((batch, seq, n_pairs * LANES), BF16),
        compiler_params=_params("parallel", "parallel"),
        name="swa",
    )(sink, proj, proj, proj, bias)


def _mla_kernel(q_ref, k_ref, v_ref, o_ref, s_ref, p_ref, *, seq, tk):
    tq = MLA_TQ
    groups = tk // LANES
    chunks = [slice(c * tk, (c + 1) * tk) for c in range(seq // tk)]
    n_units = 2 * (q_ref.shape[1] // tq)
    first = _lane_is_first_head((tq, LANES))

    def scores(u):
        t, h = divmod(u, 2)
        q = q_ref[0, t * tq:(t + 1) * tq, h * LANES:(h + 1) * LANES]
        m_l = jnp.full((tq, LANES), NEG_INF, F32)
        for ck in chunks:
            s = _dot_nt(q, k_ref[0, ck, h * LANES:(h + 1) * LANES])
            s_ref[u % 2, :, ck] = s
            for g in range(groups):
                m_l = jnp.maximum(m_l, s[:, g * LANES:(g + 1) * LANES])
        return jnp.max(m_l, axis=-1, keepdims=True)

    def probs(u, m):
        l_l = jnp.zeros((tq, LANES), F32)
        for ck in chunks:
            p = jnp.exp2(s_ref[u % 2, :, ck] - m)
            for g in range(groups):
                l_l = l_l + p[:, g * LANES:(g + 1) * LANES]
            p_ref[u % 2, :, ck] = p.astype(BF16)
        return jnp.sum(l_l, axis=-1, keepdims=True)

    m = scores(0)
    out_a = None
    for u in range(n_units):
        m_next = scores(u + 1) if u + 1 < n_units else None
        l = probs(u, m)
        out = _dot(p_ref[u % 2], v_ref[0]) / l
        if u % 2 == 0:
            out_a = out
        else:
            t = u // 2
            o_ref[0, t * tq:(t + 1) * tq, :] = jnp.where(first, out_a, out).astype(o_ref.dtype)
        m = m_next


def _mla(proj, batch, seq):
    tq = MLA_TQ * MLA_SUB
    assert seq % tq == 0 and seq % MLA_TK == 0
    n_pairs = MLA_HEADS // 2
    return pl.pallas_call(
        functools.partial(_mla_kernel, seq=seq, tk=MLA_TK),
        grid=(batch, n_pairs, seq // tq),
        in_specs=[pl.BlockSpec((1, tq, 2 * LANES), lambda b, p, i: (b, i, 4 + p)),
                  pl.BlockSpec((1, seq, 2 * LANES), lambda b, p, i: (b, 0, 8 + p)),
                  pl.BlockSpec((1, seq, LANES), lambda b, p, i: (b, 0, 24 + p))],
        out_specs=pl.BlockSpec((1, tq, LANES), lambda b, p, i: (b, i, p)),
        out_shape=jax.ShapeDtypeStruct((batch, seq, n_pairs * LANES), BF16),
        scratch_shapes=[pltpu.VMEM((2, MLA_TQ, seq), F32), pltpu.VMEM((2, MLA_TQ, seq), BF16)],
        compiler_params=_params("parallel", "parallel", "parallel"),
        name="mla",
    )(proj, proj, proj)


def _rope_tables(seq):
    pos = jnp.arange(seq, dtype=F32)

    def tables(dim):
        inv_freq = ROPE_THETA ** (-jnp.arange(0, dim, 2, dtype=F32) / dim)
        ang = pos[:, None] * inv_freq[None, :]
        return jnp.cos(ang), jnp.sin(ang)

    cos, sin = tables(HEAD_DIM)
    reps = LANES // HEAD_DIM
    cos_t = jnp.tile(jnp.concatenate([cos, cos], axis=1), (1, reps))
    sin_t = jnp.tile(jnp.concatenate([-sin, sin], axis=1), (1, reps))
    cos_r, sin_r = tables(MLA_ROPE)
    pad_l, pad_r = MLA_NOPE, LANES - MLA_NOPE - MLA_ROPE
    cos_rt = jnp.pad(jnp.concatenate([cos_r, cos_r], axis=1), ((0, 0), (pad_l, pad_r)), constant_values=1.0)
    sin_rt = jnp.pad(jnp.concatenate([-sin_r, sin_r], axis=1), ((0, 0), (pad_l, pad_r)))
    return cos_t, sin_t, cos_rt, sin_rt


def kernel(x, ffn1_norm, ffn1_w1, ffn1_w3, ffn1_w2, mix_norm, ffn2_norm, ffn2_w1, ffn2_w3, ffn2_w2, even_w_in, even_w_out, na_rel_bias, odd_w_in, odd_w_out, swa_sink, mla_q_norm, mla_w_qb, mla_kv_norm, mla_w_kvb, final_norm):
    batch, seq, d = x.shape
    depth = ffn1_norm.shape[0]
    assert d == D_MODEL and seq % FFN_TM == 0 and seq % (GRID_W * NA_ROWS) == 0
    cos_t, sin_t, cos_rt, sin_rt = _rope_tables(seq)
    h = x.reshape(batch * seq, d)
    row = lambda v: v.reshape(1, -1).astype(F32)
    fg = row(final_norm)

    ffn1_w = _ffn_weights(ffn1_w1, ffn1_w3, ffn1_w2)
    ffn2_w = _ffn_weights(ffn2_w1, ffn2_w3, ffn2_w2)
    for i in range(depth):
        j = i // 2
        h = _ffn(h, row(ffn1_norm[i]), *ffn1_w, i, fg, False)
        if i % 2 == 0:
            dil_proj, na_proj = _even_proj(h, row(mix_norm[i]), even_w_in[j].astype(BF16), cos_t, sin_t, seq)
            oa = _dilated(dil_proj.reshape(batch, seq, -1), batch, seq).reshape(batch * seq, -1)
            on = _neighbourhood(na_proj.reshape(batch, seq, -1), na_rel_bias[j], batch, seq).reshape(batch * seq, -1)
            mix = (oa, on, even_w_out[j].astype(BF16))
        else:
            w, wqb, wkb, wvb = _odd_weights(odd_w_in[j], mla_w_qb[j], mla_w_kvb[j])
            proj = _odd_proj(h, row(mix_norm[i]), w, row(mla_q_norm[j]), wqb, row(mla_kv_norm[j]), wkb, wvb,
                             cos_t, sin_t, cos_rt, sin_rt, seq)
            proj = proj.reshape(batch, seq, -1)
            oc = _swa(proj, swa_sink[j].astype(F32), batch, seq).reshape(batch * seq, -1)
            od = _mla(proj, batch, seq).reshape(batch * seq, -1)
            mix = (oc, od, odd_w_out[j].astype(BF16))
        h = _ffn(h, row(ffn2_norm[i]), *ffn2_w, i, fg, i == depth - 1, mix)
    return h.reshape(batch, seq, d)
```

```python
import functools

import jax
import jax.numpy as jnp
import numpy as np
from jax import lax
from jax.experimental import pallas as pl
from jax.experimental.pallas import tpu as pltpu

D_MODEL = 1024
D_FF = 2816
HEAD_DIM = 64
ROPE_THETA = 10000.0
NORM_EPS = 1e-6
NEG_INF = -1e30
LOG2_E = 1.4426950408889634

DIL_HEADS = 8
DIL_CONFIGS = ((128, 1), (512, 4), (2048, 16))
NA_HEADS = 8
GRID_W = 64
NA_ROWS = 8
NA_COLS = 16

SWA_Q_HEADS = 8
SWA_KV_HEADS = 2
SWA_HALF = 128
MLA_HEADS = 8
MLA_Q_RANK = 384
MLA_KV_RANK = 256
MLA_NOPE = 64
MLA_ROPE = 32
MLA_V = 64

LANES = 128
VMEM_LIMIT = 56 * 1024 * 1024

FFN_TM = 512
FFN_TF = 256
MLA_TQ = 128
MLA_SUB = 8
MLA_TK = 512
DIL_TQ = 128
DIL_UNROLL = 8
SWA_TQ = 128
SWA_UNROLL = 4
NA_GROUP = 1
NA_UNROLL = 8

BF16 = jnp.bfloat16
F32 = jnp.float32


def _params(*sem):
    return pltpu.CompilerParams(dimension_semantics=sem, vmem_limit_bytes=VMEM_LIMIT)


def _rms(x, g):
    ms = jnp.mean(x * x, axis=-1, keepdims=True)
    return x * lax.rsqrt(ms + NORM_EPS) * g


def _dot(a, b):
    return jnp.dot(a, b, preferred_element_type=F32)


def _dot_nt(a, b):
    return lax.dot_general(a, b, (((1,), (1,)), ((), ())), preferred_element_type=F32)


def _lane_is_first_head(shape):
    return lax.broadcasted_iota(jnp.int32, shape, len(shape) - 1) < HEAD_DIM


def _rope_block(x, cos, sin_signed, first_half, up, down):
    partner = jnp.where(first_half, pltpu.roll(x, up, 1), pltpu.roll(x, down, 1))
    return x * cos + partner * sin_signed


def _ffn_kernel(*refs, mix, final):
    if mix:
        x_ref, oc_ref, od_ref, wa_ref, wb_ref, g_ref, w1_ref, w3_ref, w2_ref, fg_ref, o_ref, a_ref = refs
        x = x_ref[...] + _dot(oc_ref[...], wa_ref[...]) + _dot(od_ref[...], wb_ref[...])
    else:
        x_ref, g_ref, w1_ref, w3_ref, w2_ref, fg_ref, o_ref, a_ref = refs
        x = x_ref[...]
    hn = _rms(x, g_ref[...]).astype(BF16)
    for c in range(a_ref.shape[1] // FFN_TF):
        cols = slice(c * FFN_TF, (c + 1) * FFN_TF)
        h1 = _dot(hn, w1_ref[:, cols])
        h3 = _dot(hn, w3_ref[:, cols])
        a_ref[:, cols] = (h1 * jax.nn.sigmoid(h1) * h3).astype(BF16)
    y = x + 0.5 * _dot(a_ref[...], w2_ref[...])
    if final:
        y = _rms(y, fg_ref[...])
    o_ref[...] = y


def _ffn(x2, g, w1, w3, w2, layer, fg, final, mix=None):
    m, d = x2.shape
    f = w1.shape[2]
    assert f % FFN_TF == 0
    tm = FFN_TM
    const = lambda i: (0, 0)
    of_layer = lambda i: (layer, 0, 0)
    resident = dict(pipeline_mode=pl.Buffered(1))
    mix_args, mix_specs = (), []
    if mix is not None:
        oc, od, w_out = mix
        half_w = w_out.shape[0] // 2
        mix_args = (oc, od, w_out[:half_w], w_out[half_w:])
        mix_specs = [pl.BlockSpec((tm, half_w), lambda i: (i, 0))] * 2 + [pl.BlockSpec((half_w, d), const, **resident)] * 2
    return pl.pallas_call(
        functools.partial(_ffn_kernel, mix=mix is not None, final=final),
        grid=(m // tm,),
        in_specs=[pl.BlockSpec((tm, d), lambda i: (i, 0))] + mix_specs + [
            pl.BlockSpec((1, d), const),
            pl.BlockSpec((None, d, f), of_layer, **resident),
            pl.BlockSpec((None, d, f), of_layer, **resident),
            pl.BlockSpec((None, f, d), of_layer, **resident),
            pl.BlockSpec((1, d), const),
        ],
        out_specs=pl.BlockSpec((tm, d), lambda i: (i, 0)),
        out_shape=jax.ShapeDtypeStruct((m, d), F32),
        scratch_shapes=[pltpu.VMEM((tm, f), BF16)],
        compiler_params=_params("parallel"),
        name="ffn",
    )(x2, *mix_args, g, w1, w3, w2, fg)


def _ffn_weights(w1, w3, w2):
    return w1.astype(BF16), w3.astype(BF16), w2.astype(BF16)


_EVEN_DIL_BLOCKS = 12
_EVEN_Q_BLOCKS = tuple(range(0, 4)) + tuple(range(12, 16))
_EVEN_ROPE_BLOCKS = 8


def _even_proj_kernel(x_ref, g_ref, w_ref, cos_ref, sin_ref, dil_ref, na_ref, *, n_blocks):
    hn = _rms(x_ref[...], g_ref[...]).astype(BF16)
    cos = cos_ref[...]
    sin = sin_ref[...]
    lane = lax.broadcasted_iota(jnp.int32, cos.shape, 1)
    first_half = (lane % HEAD_DIM) < HEAD_DIM // 2
    scale = HEAD_DIM ** -0.5 * LOG2_E
    for j2 in range(0, n_blocks, 2):
        wide = _dot(hn, w_ref[:, j2 * LANES:(j2 + 2) * LANES])
        for j in (j2, j2 + 1):
            blk = wide[:, (j - j2) * LANES:(j - j2 + 1) * LANES]
            if j < _EVEN_ROPE_BLOCKS:
                blk = _rope_block(blk, cos, sin, first_half, LANES - HEAD_DIM // 2, HEAD_DIM // 2)
            if j in _EVEN_Q_BLOCKS:
                blk = blk * scale
            if j < _EVEN_DIL_BLOCKS:
                dil_ref[:, j * LANES:(j + 1) * LANES] = blk
            else:
                jn = j - _EVEN_DIL_BLOCKS
                na_ref[:, jn * LANES:(jn + 1) * LANES] = blk.astype(BF16)


def _even_proj(h2, g, w_in, cos, sin, seq):
    m, d = h2.shape
    n_out = w_in.shape[1]
    n_dil = _EVEN_DIL_BLOCKS * LANES
    tm = FFN_TM
    per_seq = seq // tm
    return pl.pallas_call(
        functools.partial(_even_proj_kernel, n_blocks=n_out // LANES),
        grid=(m // tm,),
        in_specs=[
            pl.BlockSpec((tm, d), lambda i: (i, 0)),
            pl.BlockSpec((1, d), lambda i: (0, 0)),
            pl.BlockSpec((d, n_out), lambda i: (0, 0)),
            pl.BlockSpec((tm, LANES), lambda i: (i % per_seq, 0)),
            pl.BlockSpec((tm, LANES), lambda i: (i % per_seq, 0)),
        ],
        out_specs=[pl.BlockSpec((tm, n_dil), lambda i: (i, 0)),
                   pl.BlockSpec((tm, n_out - n_dil), lambda i: (i, 0))],
        out_shape=[jax.ShapeDtypeStruct((m, n_dil), F32),
                   jax.ShapeDtypeStruct((m, n_out - n_dil), BF16)],
        compiler_params=_params("parallel"),
        name="even_proj",
    )(h2, g, w_in, cos, sin)


def _pair_scores(q, k):
    first = _lane_is_first_head(q.shape)
    zero = jnp.zeros_like(q)
    q2 = jnp.concatenate([jnp.where(first, q, zero), jnp.where(first, zero, q)], axis=0)
    return _dot_nt(q2, k)


def _pair_merge(x2, tq):
    first = _lane_is_first_head((tq, LANES))
    return jnp.where(first, x2[:tq], x2[tq:])


def _band_bias(tq, half, win):
    row = np.arange(2 * tq)[:, None] % tq
    col = np.arange(win)[None, :]
    kinds = [np.where(np.abs(col - row - k * half) <= half, 0.0, NEG_INF) for k in range(3)]
    return jnp.asarray(np.stack(kinds), F32)


def _band_window(q0, tq, half, length, win):
    ks = pl.multiple_of(jnp.clip(q0 - half, 0, length - win), half)
    return ks, (q0 - ks) // half


def _dilated_kernel(q_ref, k_ref, v_ref, *refs, seq, configs, tiles):
    n = len(configs)
    bias_refs, o_ref = refs[:n], refs[n]
    qs_ref, ks_ref, vs_ref = refs[n + 1:n + 4]
    ob_refs, lb_refs = refs[n + 4:2 * n + 4], refs[2 * n + 4:3 * n + 4]
    stage_refs = refs[3 * n + 4:]
    copies = {1: (q_ref, k_ref, v_ref)}
    for c, ((half, dil), (tq, win)) in enumerate(zip(configs, tiles)):
        length = seq // dil
        blocks = length // tq
        base_dil = max(d for d in copies if dil % d == 0)
        step = dil // base_dil
        keep = dil > 1 and any(d2 > dil and d2 % dil == 0 for _, d2 in configs)
        if keep:
            copies[dil] = stage_refs
        for a, (src, dst) in enumerate(zip(copies[base_dil], (qs_ref, ks_ref, vs_ref))):
            for r in range(dil):
                start = (r % base_dil) * (seq // base_dil) + r // base_dil
                rows = pl.ds(start, length, stride=step) if step > 1 else pl.ds(start, length)
                x = src[rows, :]
                dst[r * length:(r + 1) * length, :] = x.astype(BF16)
                if keep:
                    stage_refs[a][r * length:(r + 1) * length, :] = x

        def body(i, carry, c=c, half=half, dil=dil, tq=tq, win=win, length=length, blocks=blocks):
            geo = []
            for g in range(DIL_UNROLL):
                n = i * DIL_UNROLL + g
                r = n // blocks
                q0 = pl.multiple_of((n % blocks) * tq, tq)
                ks, kind = _band_window(q0, tq, half, length, win)
                geo.append((r, q0, pl.multiple_of(r * length, tq) + ks, kind))
            ss = [_pair_scores(qs_ref[pl.ds(pl.multiple_of(r * length, tq) + q0, tq), :], ks_ref[pl.ds(kb, win), :])
                  + bias_refs[c][kind] for r, q0, kb, kind in geo]
            ms = [jnp.max(s, axis=-1, keepdims=True) for s in ss]
            ps = [jnp.exp2(s - m) for s, m in zip(ss, ms)]
            ls = [jnp.sum(p, axis=-1, keepdims=True) for p in ps]
            os_ = [_dot(p.astype(BF16), vs_ref[pl.ds(kb, win), :]) for p, (_, _, kb, _) in zip(ps, geo)]
            for (r, q0, _, _), o2, m, l in zip(geo, os_, ms, ls):
                o2 = o2 / l
                lse2 = jnp.broadcast_to(m + jnp.log2(l), o2.shape)
                rows = pl.ds(r + dil * q0, tq, stride=dil) if dil > 1 else pl.ds(q0, tq)
                ob_refs[c][rows, :] = _pair_merge(o2, tq)
                lb_refs[c][rows, :] = _pair_merge(lse2, tq)
            return carry

        lax.fori_loop(0, seq // (tq * DIL_UNROLL), body, 0)

    chunk = 512

    def merge(i, carry):
        rows = pl.ds(pl.multiple_of(i * chunk, chunk), chunk)
        lses = [lb[rows, :] for lb in lb_refs]
        mx = functools.reduce(jnp.maximum, lses)
        es = [jnp.exp2(l - mx) for l in lses]
        num = sum(e * ob[rows, :] for e, ob in zip(es, ob_refs))
        o_ref[rows, :] = (num / sum(es)).astype(o_ref.dtype)
        return carry

    lax.fori_loop(0, seq // chunk, merge, 0)


def _dilated(dil_proj, batch, seq):
    n_pairs = DIL_HEADS // 2
    configs = tuple((window // 2 // dil, dil) for window, dil in DIL_CONFIGS)
    tiles = []
    for half, dil in configs:
        tq = min(DIL_TQ, seq // dil)
        tiles.append((tq, min(tq + 2 * half, seq // dil)))
        assert seq % (dil * tq) == 0 and tq % half == 0
    biases = [_band_bias(tq, half, win) for (half, _), (tq, win) in zip(configs, tiles)]

    def spec(base):
        return pl.BlockSpec((None, seq, LANES), lambda b, p: (b, 0, base + p))

    return pl.pallas_call(
        functools.partial(_dilated_kernel, seq=seq, configs=configs, tiles=tuple(tiles)),
        grid=(batch, n_pairs),
        in_specs=[spec(0), spec(4), spec(8)] + [pl.BlockSpec(b.shape, lambda b_, p: (0, 0, 0)) for b in biases],
        out_specs=pl.BlockSpec((None, seq, LANES), lambda b, p: (b, 0, p)),
        out_shape=jax.ShapeDtypeStruct((batch, seq, n_pairs * LANES), BF16),
        scratch_shapes=[pltpu.VMEM((seq, LANES), BF16)] * 3 + [pltpu.VMEM((seq, LANES), F32)] * (2 * len(configs) + 3),
        compiler_params=_params("parallel", "parallel"),
        name="dilated",
    )(dil_proj, dil_proj, dil_proj, *biases)


def _na_group_geometry(rows, kr):
    n_groups = rows // NA_GROUP
    wr = -(-(kr + NA_GROUP - 1) * GRID_W // LANES) * LANES // GRID_W
    g = np.arange(n_groups)
    start = np.minimum(np.clip(g * NA_GROUP - kr // 2, 0, rows - kr), rows - wr)
    return n_groups, wr, start


def _na_kernel(start_ref, kind_ref, q_ref, k_ref, v_ref, bias_ref, o_ref, *, rows, kr):
    n_groups, wr, _ = _na_group_geometry(rows, kr)
    tq = NA_GROUP * GRID_W

    def body(i, carry):
        geo = []
        for u in range(NA_UNROLL):
            g = i * NA_UNROLL + u
            geo.append((pl.multiple_of(g * tq, tq), pl.multiple_of(start_ref[g] * GRID_W, GRID_W), kind_ref[g]))
        ss = [_pair_scores(q_ref[pl.ds(q0, tq), :], k_ref[pl.ds(k0, wr * GRID_W), :]) + bias_ref[kind]
              for q0, k0, kind in geo]
        ms = [jnp.max(s, axis=-1, keepdims=True) for s in ss]
        ps = [jnp.exp2(s - m) for s, m in zip(ss, ms)]
        ls = [jnp.sum(p, axis=-1, keepdims=True) for p in ps]
        os_ = [_dot(p.astype(BF16), v_ref[pl.ds(k0, wr * GRID_W), :]) for p, (_, k0, _) in zip(ps, geo)]
        for (q0, _, _), o2, l in zip(geo, os_, ls):
            o_ref[pl.ds(q0, tq), :] = _pair_merge(o2 / l, tq).astype(o_ref.dtype)
        return carry

    lax.fori_loop(0, n_groups // NA_UNROLL, body, 0)


def _na_bias_table(rpb, rows, kr):
    n_groups, wr, start = _na_group_geometry(rows, kr)
    qcol = np.arange(GRID_W)
    kcol = np.arange(GRID_W)
    wstart = np.clip(qcol - NA_COLS // 2, 0, GRID_W - NA_COLS)
    col_ok = (kcol[None, :] >= wstart[:, None]) & (kcol[None, :] < wstart[:, None] + NA_COLS)
    dc = np.clip(kcol[None, :] - qcol[:, None] + NA_COLS - 1, 0, 2 * NA_COLS - 2)

    def geometry(g):
        qrow = g * NA_GROUP + np.arange(NA_GROUP)
        krow = start[g] + np.arange(wr)
        rs = np.clip(qrow - kr // 2, 0, rows - kr)
        row_ok = (krow[None, :] >= rs[:, None]) & (krow[None, :] < rs[:, None] + kr)
        dr = np.where(row_ok, krow[None, :] - qrow[:, None] + NA_ROWS - 1, -1)
        return dr

    patterns, kind_of = [], []
    for g in range(n_groups):
        dr = geometry(g)
        match = [k for k, p in enumerate(patterns) if np.array_equal(p, dr)]
        if not match:
            patterns.append(dr)
        kind_of.append(match[0] if match else len(patterns) - 1)
    onehot = (dc[None] == np.arange(2 * NA_COLS - 1)[:, None, None]) & col_ok[None]
    planes = jnp.einsum("hrd,dqk->hrqk", rpb * LOG2_E, jnp.asarray(onehot, F32), precision=lax.Precision.HIGHEST)
    planes = planes + jnp.asarray(np.where(col_ok, 0.0, NEG_INF), F32)
    masked = jnp.full(planes.shape[:1] + planes.shape[2:], NEG_INF, F32)
    slabs = []
    for dr in patterns:
        rows_i = []
        for i in range(NA_GROUP):
            cols_j = [planes[:, int(dr[i, j])] if dr[i, j] >= 0 else masked for j in range(wr)]
            rows_i.append(jnp.concatenate(cols_j, axis=-1))
        slab = jnp.stack(rows_i, axis=1)
        slabs.append(slab.reshape(NA_HEADS // 2, 2 * NA_GROUP * GRID_W, wr * GRID_W))
    table = jnp.stack(slabs)
    return table, jnp.asarray(start, jnp.int32), jnp.asarray(kind_of, jnp.int32)


def _neighbourhood(proj, rpb, batch, seq):
    rows = seq // GRID_W
    kr = min(NA_ROWS, rows)
    assert rows % (NA_GROUP * NA_UNROLL) == 0 and rows >= kr + NA_GROUP
    table, start, kind_of = _na_bias_table(rpb.astype(F32), rows, kr)
    n_pairs = NA_HEADS // 2

    def spec(base):
        return pl.BlockSpec((None, seq, LANES), lambda b, p: (b, 0, base + p))

    smem = pl.BlockSpec(memory_space=pltpu.SMEM)
    return pl.pallas_call(
        functools.partial(_na_kernel, rows=rows, kr=kr),
        grid=(batch, n_pairs),
        in_specs=[smem, smem, spec(0), spec(4), spec(8),
                  pl.BlockSpec((table.shape[0], None) + table.shape[2:], lambda b, p: (0, p, 0, 0))],
        out_specs=pl.BlockSpec((None, seq, LANES), lambda b, p: (b, 0, p)),
        out_shape=jax.ShapeDtypeStruct((batch, seq, n_pairs * LANES), BF16),
        compiler_params=_params("parallel", "parallel"),
        name="neighbourhood",
    )(start, kind_of, proj, proj, proj, table)


_ODD_W_COLS = 14 * LANES
_ODD_OUT_COLS = 28 * LANES


def _odd_proj_kernel(x_ref, g_ref, w_ref, qn_ref, wqb_ref, kvn_ref, wkb_ref, wvb_ref,
                     cos_ref, sin_ref, cosr_ref, sinr_ref, o_ref):
    hn = _rms(x_ref[...], g_ref[...]).astype(BF16)
    cos, sin = cos_ref[...], sin_ref[...]
    cosr, sinr = cosr_ref[...], sinr_ref[...]
    lane = lax.broadcasted_iota(jnp.int32, cos.shape, 1)
    first_half = (lane % HEAD_DIM) < HEAD_DIM // 2
    first_half_r = lane < MLA_NOPE + MLA_ROPE // 2
    swa_scale = HEAD_DIM ** -0.5 * LOG2_E
    mla_scale = (MLA_NOPE + MLA_ROPE) ** -0.5 * LOG2_E

    def col(j, n=1):
        return slice(j * LANES, (j + n) * LANES)

    def rope(blk):
        return _rope_block(blk, cos, sin, first_half, LANES - HEAD_DIM // 2, HEAD_DIM // 2)

    def rope_r(blk):
        return _rope_block(blk, cosr, sinr, first_half_r, LANES - MLA_ROPE // 2, MLA_ROPE // 2)

    def halves(wide):
        return wide[:, :LANES], wide[:, LANES:]

    latent = _dot(hn, w_ref[:, col(8, 6)])
    for j in range(0, 6, 2):
        for jj, blk in zip((j, j + 1), halves(_dot(hn, w_ref[:, col(j, 2)]))):
            blk = rope(blk) * swa_scale if jj < 4 else rope(blk)
            o_ref[:, col(jj)] = blk.astype(BF16)

    q_an = _rms(latent[:, :3 * LANES], qn_ref[...]).astype(BF16)
    kv_an = _rms(latent[:, 3 * LANES:5 * LANES], kvn_ref[...]).astype(BF16)
    k_pe = rope_r(latent[:, 5 * LANES:])
    for h in range(0, MLA_HEADS, 2):
        for hh, qh in zip((h, h + 1), halves(_dot(q_an, wqb_ref[:, col(h, 2)]))):
            o_ref[:, col(8 + hh)] = (rope_r(qh) * mla_scale).astype(BF16)
    for h in range(0, MLA_HEADS, 2):
        for hh, kh in zip((h, h + 1), halves(_dot(kv_an, wkb_ref[:, col(h, 2)]))):
            o_ref[:, col(16 + hh)] = (kh + k_pe).astype(BF16)
    o_ref[:, col(6, 2)] = _dot(hn, w_ref[:, col(6, 2)]).astype(BF16)
    for j in range(0, MLA_HEADS // 2, 2):
        o_ref[:, col(24 + j, 2)] = _dot(kv_an, wvb_ref[:, col(j, 2)]).astype(BF16)


def _odd_proj(h2, g, w, qn, wqb, kvn, wkb, wvb, cos, sin, cosr, sinr, seq):
    m, d = h2.shape
    tm = FFN_TM
    per_seq = seq // tm
    full = lambda a: pl.BlockSpec(a.shape, lambda i: (0, 0))
    tab = pl.BlockSpec((tm, LANES), lambda i: (i % per_seq, 0))
    return pl.pallas_call(
        _odd_proj_kernel,
        grid=(m // tm,),
        in_specs=[pl.BlockSpec((tm, d), lambda i: (i, 0)), full(g), full(w), full(qn), full(wqb),
                  full(kvn), full(wkb), full(wvb), tab, tab, tab, tab],
        out_specs=pl.BlockSpec((tm, _ODD_OUT_COLS), lambda i: (i, 0)),
        out_shape=jax.ShapeDtypeStruct((m, _ODD_OUT_COLS), BF16),
        compiler_params=_params("parallel"),
        name="odd_proj",
    )(h2, g, w, qn, wqb, kvn, wkb, wvb, cos, sin, cosr, sinr)


def _odd_weights(w_in, w_qb, w_kvb):
    d = w_in.shape[0]
    c = np.cumsum([0, SWA_Q_HEADS * HEAD_DIM, SWA_KV_HEADS * HEAD_DIM, SWA_KV_HEADS * HEAD_DIM,
                   MLA_Q_RANK, MLA_KV_RANK, MLA_ROPE])

    def dup(w):
        w = w.reshape(d, SWA_KV_HEADS, 1, HEAD_DIM)
        return jnp.broadcast_to(w, (d, SWA_KV_HEADS, 2, HEAD_DIM)).reshape(d, SWA_KV_HEADS * LANES)

    w_kpe = jnp.pad(w_in[:, c[5]:c[6]], ((0, 0), (MLA_NOPE, LANES - MLA_NOPE - MLA_ROPE)))
    w = jnp.concatenate([w_in[:, c[0]:c[1]], dup(w_in[:, c[1]:c[2]]), dup(w_in[:, c[2]:c[3]]),
                         w_in[:, c[3]:c[4]], w_in[:, c[4]:c[5]], w_kpe], axis=1).astype(BF16)
    qk = MLA_NOPE + MLA_ROPE
    wqb = jnp.pad(w_qb.reshape(MLA_Q_RANK, MLA_HEADS, qk), ((0, 0), (0, 0), (0, LANES - qk)))
    wqb = wqb.reshape(MLA_Q_RANK, MLA_HEADS * LANES).astype(BF16)
    kv = w_kvb.reshape(MLA_KV_RANK, MLA_HEADS, MLA_NOPE + MLA_V)
    wkb = jnp.pad(kv[:, :, :MLA_NOPE], ((0, 0), (0, 0), (0, LANES - MLA_NOPE)))
    wkb = wkb.reshape(MLA_KV_RANK, MLA_HEADS * LANES).astype(BF16)
    wvb = kv[:, :, MLA_NOPE:].reshape(MLA_KV_RANK, MLA_HEADS * MLA_V).astype(BF16)
    return w, wqb, wkb, wvb


def _swa_kernel(sink_ref, q_ref, k_ref, v_ref, bias_ref, o_ref, *, seq, half, tq, win):
    pair = pl.program_id(1)
    first = lax.broadcasted_iota(jnp.int32, (2 * tq, 1), 0) < tq
    sink = jnp.where(first, sink_ref[2 * pair], sink_ref[2 * pair + 1]) * LOG2_E

    def body(i, carry):
        geo = []
        for g in range(SWA_UNROLL):
            q0 = pl.multiple_of((i * SWA_UNROLL + g) * tq, tq)
            geo.append((q0,) + _band_window(q0, tq, half, seq, win))
        ss = [_pair_scores(q_ref[pl.ds(q0, tq), :], k_ref[pl.ds(ks, win), :]) + bias_ref[kind]
              for q0, ks, kind in geo]
        ms = [jnp.maximum(jnp.max(s, axis=-1, keepdims=True), sink) for s in ss]
        ps = [jnp.exp2(s - m) for s, m in zip(ss, ms)]
        ls = [jnp.sum(p, axis=-1, keepdims=True) + jnp.exp2(sink - m) for p, m in zip(ps, ms)]
        os_ = [_dot(p.astype(BF16), v_ref[pl.ds(ks, win), :]) for p, (_, ks, _) in zip(ps, geo)]
        for (q0, _, _), o2, l in zip(geo, os_, ls):
            o_ref[pl.ds(q0, tq), :] = _pair_merge(o2 / l, tq).astype(o_ref.dtype)
        return carry

    lax.fori_loop(0, seq // (tq * SWA_UNROLL), body, 0)


def _swa(proj, sink, batch, seq):
    tq = SWA_TQ
    win = tq + 2 * SWA_HALF
    assert seq % (tq * SWA_UNROLL) == 0 and tq % SWA_HALF == 0 and seq >= win
    n_pairs = SWA_Q_HEADS // 2
    pairs_per_kv = n_pairs // SWA_KV_HEADS
    bias = _band_bias(tq, SWA_HALF, win)
    return pl.pallas_call(
        functools.partial(_swa_kernel, seq=seq, half=SWA_HALF, tq=tq, win=win),
        grid=(batch, n_pairs),
        in_specs=[pl.BlockSpec(memory_space=pltpu.SMEM),
                  pl.BlockSpec((None, seq, LANES), lambda b, p: (b, 0, p)),
                  pl.BlockSpec((None, seq, LANES), lambda b, p: (b, 0, 4 + p // pairs_per_kv)),
                  pl.BlockSpec((None, seq, LANES), lambda b, p: (b, 0, 6 + p // pairs_per_kv)),
                  pl.BlockSpec(bias.shape, lambda b, p: (0, 0, 0))],
        out_specs=pl.BlockSpec((None, seq, LANES), lambda b, p: (b, 0, p)),
        out_shape=jax.ShapeDtypeStruct((batch, seq, n_pairs * LANES), BF16),
        compiler_params=_params("parallel", "parallel"),
        name="swa",
    )(sink, proj, proj, proj, bias)


def _mla_kernel(q_ref, k_ref, v_ref, o_ref, s_ref, p_ref, *, seq, tk):
    tq = MLA_TQ
    groups = tk // LANES
    chunks = [slice(c * tk, (c + 1) * tk) for c in range(seq // tk)]
    n_units = 2 * (q_ref.shape[1] // tq)
    first = _lane_is_first_head((tq, LANES))

    def scores(u):
        t, h = divmod(u, 2)
        q = q_ref[0, t * tq:(t + 1) * tq, h * LANES:(h + 1) * LANES]
        m_l = jnp.full((tq, LANES), NEG_INF, F32)
        for ck in chunks:
            s = _dot_nt(q, k_ref[0, ck, h * LANES:(h + 1) * LANES])
            s_ref[u % 2, :, ck] = s
            for g in range(groups):
                m_l = jnp.maximum(m_l, s[:, g * LANES:(g + 1) * LANES])
        return jnp.max(m_l, axis=-1, keepdims=True)

    def probs(u, m):
        l_l = jnp.zeros((tq, LANES), F32)
        for ck in chunks:
            p = jnp.exp2(s_ref[u % 2, :, ck] - m)
            for g in range(groups):
                l_l = l_l + p[:, g * LANES:(g + 1) * LANES]
            p_ref[u % 2, :, ck] = p.astype(BF16)
        return jnp.sum(l_l, axis=-1, keepdims=True)

    m = scores(0)
    out_a = None
    for u in range(n_units):
        m_next = scores(u + 1) if u + 1 < n_units else None
        l = probs(u, m)
        out = _dot(p_ref[u % 2], v_ref[0]) / l
        if u % 2 == 0:
            out_a = out
        else:
            t = u // 2
            o_ref[0, t * tq:(t + 1) * tq, :] = jnp.where(first, out_a, out).astype(o_ref.dtype)
        m = m_next


def _mla(proj, batch, seq):
    tq = MLA_TQ * MLA_SUB
    assert seq % tq == 0 and seq % MLA_TK == 0
    n_pairs = MLA_HEADS // 2
    return pl.pallas_call(
        functools.partial(_mla_kernel, seq=seq, tk=MLA_TK),
        grid=(batch, n_pairs, seq // tq),
        in_specs=[pl.BlockSpec((1, tq, 2 * LANES), lambda b, p, i: (b, i, 4 + p)),
                  pl.BlockSpec((1, seq, 2 * LANES), lambda b, p, i: (b, 0, 8 + p)),
                  pl.BlockSpec((1, seq, LANES), lambda b, p, i: (b, 0, 24 + p))],
        out_specs=pl.BlockSpec((1, tq, LANES), lambda b, p, i: (b, i, p)),
        out_shape=jax.ShapeDtypeStruct((batch, seq, n_pairs * LANES), BF16),
        scratch_shapes=[pltpu.VMEM((2, MLA_TQ, seq), F32), pltpu.VMEM((2, MLA_TQ, seq), BF16)],
        compiler_params=_params("parallel", "parallel", "parallel"),
        name="mla",
    )(proj, proj, proj)


def _rope_tables(seq):
    pos = jnp.arange(seq, dtype=F32)

    def tables(dim):
        inv_freq = ROPE_THETA ** (-jnp.arange(0, dim, 2, dtype=F32) / dim)
        ang = pos[:, None] * inv_freq[None, :]
        return jnp.cos(ang), jnp.sin(ang)

    cos, sin = tables(HEAD_DIM)
    reps = LANES // HEAD_DIM
    cos_t = jnp.tile(jnp.concatenate([cos, cos], axis=1), (1, reps))
    sin_t = jnp.tile(jnp.concatenate([-sin, sin], axis=1), (1, reps))
    cos_r, sin_r = tables(MLA_ROPE)
    pad_l, pad_r = MLA_NOPE, LANES - MLA_NOPE - MLA_ROPE
    cos_rt = jnp.pad(jnp.concatenate([cos_r, cos_r], axis=1), ((0, 0), (pad_l, pad_r)), constant_values=1.0)
    sin_rt = jnp.pad(jnp.concatenate([-sin_r, sin_r], axis=1), ((0, 0), (pad_l, pad_r)))
    return cos_t, sin_t, cos_rt, sin_rt


def kernel(x, ffn1_norm, ffn1_w1, ffn1_w3, ffn1_w2, mix_norm, ffn2_norm, ffn2_w1, ffn2_w3, ffn2_w2, even_w_in, even_w_out, na_rel_bias, odd_w_in, odd_w_out, swa_sink, mla_q_norm, mla_w_qb, mla_kv_norm, mla_w_kvb, final_norm):
    batch, seq, d = x.shape
    depth = ffn1_norm.shape[0]
    assert d == D_MODEL and seq % FFN_TM == 0 and seq % (GRID_W * NA_ROWS) == 0
    cos_t, sin_t, cos_rt, sin_rt = _rope_tables(seq)
    h = x.reshape(batch * seq, d)
    row = lambda v: v.reshape(1, -1).astype(F32)
    fg = row(final_norm)

    ffn1_w = _ffn_weights(ffn1_w1, ffn1_w3, ffn1_w2)
    ffn2_w = _ffn_weights(ffn2_w1, ffn2_w3, ffn2_w2)
    for i in range(depth):
        j = i // 2
        h = _ffn(h, row(ffn1_norm[i]), *ffn1_w, i, fg, False)
        if i % 2 == 0:
            dil_proj, na_proj = _even_proj(h, row(mix_norm[i]), even_w_in[j].astype(BF16), cos_t, sin_t, seq)
            oa = _dilated(dil_proj.reshape(batch, seq, -1), batch, seq).reshape(batch * seq, -1)
            on = _neighbourhood(na_proj.reshape(batch, seq, -1), na_rel_bias[j], batch, seq).reshape(batch * seq, -1)
            mix = (oa, on, even_w_out[j].astype(BF16))
        else:
            w, wqb, wkb, wvb = _odd_weights(odd_w_in[j], mla_w_qb[j], mla_w_kvb[j])
            proj = _odd_proj(h, row(mix_norm[i]), w, row(mla_q_norm[j]), wqb, row(mla_kv_norm[j]), wkb, wvb,
                             cos_t, sin_t, cos_rt, sin_rt, seq)
            proj = proj.reshape(batch, seq, -1)
            oc = _swa(proj, swa_sink[j].astype(F32), batch, seq).reshape(batch * seq, -1)
            od = _mla(proj, batch, seq).reshape(batch * seq, -1)
            mix = (oc, od, odd_w_out[j].astype(BF16))
        h = _ffn(h, row(ffn2_norm[i]), *ffn2_w, i, fg, i == depth - 1, mix)
    return h.reshape(batch, seq, d)
```

```python
import functools

import jax
import jax.numpy as jnp
import numpy as np
from jax import lax
from jax.experimental import pallas as pl
from jax.experimental.pallas import tpu as pltpu

D_MODEL = 1024
D_FF = 2816
HEAD_DIM = 64
ROPE_THETA = 10000.0
NORM_EPS = 1e-6
NEG_INF = -1e30
LOG2_E = 1.4426950408889634

DIL_HEADS = 8
DIL_CONFIGS = ((128, 1), (512, 4), (2048, 16))
NA_HEADS = 8
GRID_W = 64
NA_ROWS = 8
NA_COLS = 16

SWA_Q_HEADS = 8
SWA_KV_HEADS = 2
SWA_HALF = 128
MLA_HEADS = 8
MLA_Q_RANK = 384
MLA_KV_RANK = 256
MLA_NOPE = 64
MLA_ROPE = 32
MLA_V = 64

LANES = 128
VMEM_LIMIT = 56 * 1024 * 1024

FFN_TM = 512
FFN_TF = 256
FFN_SUB = 2
MLA_TQ = 128
MLA_SUB = 8
MLA_TK = 512
DIL_TQ = 128
DIL_UNROLL = 8
SWA_TQ = 128
SWA_UNROLL = 4
NA_GROUP = 1
NA_UNROLL = 8

BF16 = jnp.bfloat16
F32 = jnp.float32


def _params(*sem):
    return pltpu.CompilerParams(dimension_semantics=sem, vmem_limit_bytes=VMEM_LIMIT)


def _rms(x, g):
    ms = jnp.mean(x * x, axis=-1, keepdims=True)
    return x * lax.rsqrt(ms + NORM_EPS) * g


def _dot(a, b):
    return jnp.dot(a, b, preferred_element_type=F32)


def _dot_nt(a, b):
    return lax.dot_general(a, b, (((1,), (1,)), ((), ())), preferred_element_type=F32)


def _lane_is_first_head(shape):
    return lax.broadcasted_iota(jnp.int32, shape, len(shape) - 1) < HEAD_DIM


def _rope_block(x, cos, sin_signed, first_half, up, down):
    partner = jnp.where(first_half, pltpu.roll(x, up, 1), pltpu.roll(x, down, 1))
    return x * cos + partner * sin_signed


def _ffn_kernel(*refs, mix, final):
    if mix:
        x_ref, oc_ref, od_ref, wa_ref, wb_ref, g_ref, w1_ref, w3_ref, w2_ref, fg_ref, o_ref, a_ref = refs
    else:
        x_ref, g_ref, w1_ref, w3_ref, w2_ref, fg_ref, o_ref, a_ref = refs
    tm = x_ref.shape[0] // FFN_SUB
    for t in range(FFN_SUB):
        rows = slice(t * tm, (t + 1) * tm)
        x = x_ref[rows, :]
        if mix:
            x = x + _dot(oc_ref[rows, :], wa_ref[...]) + _dot(od_ref[rows, :], wb_ref[...])
        hn = _rms(x, g_ref[...]).astype(BF16)
        for c in range(a_ref.shape[1] // FFN_TF):
            cols = slice(c * FFN_TF, (c + 1) * FFN_TF)
            h1 = _dot(hn, w1_ref[:, cols])
            h3 = _dot(hn, w3_ref[:, cols])
            a_ref[rows, cols] = (h1 * jax.nn.sigmoid(h1) * h3).astype(BF16)
        y = x + 0.5 * _dot(a_ref[rows, :], w2_ref[...])
        if final:
            y = _rms(y, fg_ref[...])
        o_ref[rows, :] = y


def _ffn(x2, g, w1, w3, w2, layer, fg, final, mix=None):
    m, d = x2.shape
    f = w1.shape[2]
    assert f % FFN_TF == 0
    tm = FFN_TM * FFN_SUB
    const = lambda i: (0, 0)
    of_layer = lambda i: (layer, 0, 0)
    resident = dict(pipeline_mode=pl.Buffered(1))
    mix_args, mix_specs = (), []
    if mix is not None:
        oc, od, w_out = mix
        half_w = w_out.shape[0] // 2
        mix_args = (oc, od, w_out[:half_w], w_out[half_w:])
        mix_specs = [pl.BlockSpec((tm, half_w), lambda i: (i, 0))] * 2 + [pl.BlockSpec((half_w, d), const, **resident)] * 2
    return pl.pallas_call(
        functools.partial(_ffn_kernel, mix=mix is not None, final=final),
        grid=(m // tm,),
        in_specs=[pl.BlockSpec((tm, d), lambda i: (i, 0))] + mix_specs + [
            pl.BlockSpec((1, d), const),
            pl.BlockSpec((None, d, f), of_layer, **resident),
            pl.BlockSpec((None, d, f), of_layer, **resident),
            pl.BlockSpec((None, f, d), of_layer, **resident),
            pl.BlockSpec((1, d), const),
        ],
        out_specs=pl.BlockSpec((tm, d), lambda i: (i, 0)),
        out_shape=jax.ShapeDtypeStruct((m, d), F32),
        scratch_shapes=[pltpu.VMEM((tm, f), BF16)],
        compiler_params=_params("parallel"),
        name="ffn",
    )(x2, *mix_args, g, w1, w3, w2, fg)


def _ffn_weights(w1, w3, w2):
    return w1.astype(BF16), w3.astype(BF16), w2.astype(BF16)


_EVEN_DIL_BLOCKS = 12
_EVEN_Q_BLOCKS = tuple(range(0, 4)) + tuple(range(12, 16))
_EVEN_ROPE_BLOCKS = 8


def _even_proj_kernel(x_ref, g_ref, w_ref, cos_ref, sin_ref, dil_ref, na_ref, *, n_blocks):
    hn = _rms(x_ref[...], g_ref[...]).astype(BF16)
    cos = cos_ref[...]
    sin = sin_ref[...]
    lane = lax.broadcasted_iota(jnp.int32, cos.shape, 1)
    first_half = (lane % HEAD_DIM) < HEAD_DIM // 2
    scale = HEAD_DIM ** -0.5 * LOG2_E
    for j2 in range(0, n_blocks, 2):
        wide = _dot(hn, w_ref[:, j2 * LANES:(j2 + 2) * LANES])
        for j in (j2, j2 + 1):
            blk = wide[:, (j - j2) * LANES:(j - j2 + 1) * LANES]
            if j < _EVEN_ROPE_BLOCKS:
                blk = _rope_block(blk, cos, sin, first_half, LANES - HEAD_DIM // 2, HEAD_DIM // 2)
            if j in _EVEN_Q_BLOCKS:
                blk = blk * scale
            if j < _EVEN_DIL_BLOCKS:
                dil_ref[:, j * LANES:(j + 1) * LANES] = blk
            else:
                jn = j - _EVEN_DIL_BLOCKS
                na_ref[:, jn * LANES:(jn + 1) * LANES] = blk.astype(BF16)


def _even_proj(h2, g, w_in, cos, sin, seq):
    m, d = h2.shape
    n_out = w_in.shape[1]
    n_dil = _EVEN_DIL_BLOCKS * LANES
    tm = FFN_TM
    per_seq = seq // tm
    return pl.pallas_call(
        functools.partial(_even_proj_kernel, n_blocks=n_out // LANES),
        grid=(m // tm,),
        in_specs=[
            pl.BlockSpec((tm, d), lambda i: (i, 0)),
            pl.BlockSpec((1, d), lambda i: (0, 0)),
            pl.BlockSpec((d, n_out), lambda i: (0, 0)),
            pl.BlockSpec((tm, LANES), lambda i: (i % per_seq, 0)),
            pl.BlockSpec((tm, LANES), lambda i: (i % per_seq, 0)),
        ],
        out_specs=[pl.BlockSpec((tm, n_dil), lambda i: (i, 0)),
                   pl.BlockSpec((tm, n_out - n_dil), lambda i: (i, 0))],
        out_shape=[jax.ShapeDtypeStruct((m, n_dil), F32),
                   jax.ShapeDtypeStruct((m, n_out - n_dil), BF16)],
        compiler_params=_params("parallel"),
        name="even_proj",
    )(h2, g, w_in, cos, sin)


def _pair_scores(q, k):
    first = _lane_is_first_head(q.shape)
    zero = jnp.zeros_like(q)
    q2 = jnp.concatenate([jnp.where(first, q, zero), jnp.where(first, zero, q)], axis=0)
    return _dot_nt(q2, k)


def _pair_merge(x2, tq):
    first = _lane_is_first_head((tq, LANES))
    return jnp.where(first, x2[:tq], x2[tq:])


def _band_bias(tq, half, win):
    row = np.arange(2 * tq)[:, None] % tq
    col = np.arange(win)[None, :]
    kinds = [np.where(np.abs(col - row - k * half) <= half, 0.0, NEG_INF) for k in range(3)]
    return jnp.asarray(np.stack(kinds), F32)


def _band_window(q0, tq, half, length, win):
    ks = pl.multiple_of(jnp.clip(q0 - half, 0, length - win), half)
    return ks, (q0 - ks) // half


def _dilated_kernel(q_ref, k_ref, v_ref, *refs, seq, configs, tiles):
    n = len(configs)
    bias_refs, o_ref = refs[:n], refs[n]
    qs_ref, ks_ref, vs_ref = refs[n + 1:n + 4]
    ob_refs, lb_refs = refs[n + 4:2 * n + 4], refs[2 * n + 4:3 * n + 4]
    stage_refs = refs[3 * n + 4:]
    copies = {1: (q_ref, k_ref, v_ref)}
    for c, ((half, dil), (tq, win)) in enumerate(zip(configs, tiles)):
        length = seq // dil
        blocks = length // tq
        base_dil = max(d for d in copies if dil % d == 0)
        step = dil // base_dil
        keep = dil > 1 and any(d2 > dil and d2 % dil == 0 for _, d2 in configs)
        if keep:
            copies[dil] = stage_refs
        for a, (src, dst) in enumerate(zip(copies[base_dil], (qs_ref, ks_ref, vs_ref))):
            for r in range(dil):
                start = (r % base_dil) * (seq // base_dil) + r // base_dil
                rows = pl.ds(start, length, stride=step) if step > 1 else pl.ds(start, length)
                x = src[rows, :]
                dst[r * length:(r + 1) * length, :] = x.astype(BF16)
                if keep:
                    stage_refs[a][r * length:(r + 1) * length, :] = x

        def body(i, carry, c=c, half=half, dil=dil, tq=tq, win=win, length=length, blocks=blocks):
            geo = []
            for g in range(DIL_UNROLL):
                n = i * DIL_UNROLL + g
                r = n // blocks
                q0 = pl.multiple_of((n % blocks) * tq, tq)
                ks, kind = _band_window(q0, tq, half, length, win)
                geo.append((r, q0, pl.multiple_of(r * length, tq) + ks, kind))
            ss = [_pair_scores(qs_ref[pl.ds(pl.multiple_of(r * length, tq) + q0, tq), :], ks_ref[pl.ds(kb, win), :])
                  + bias_refs[c][kind] for r, q0, kb, kind in geo]
            ms = [jnp.max(s, axis=-1, keepdims=True) for s in ss]
            ps = [jnp.exp2(s - m) for s, m in zip(ss, ms)]
            ls = [jnp.sum(p, axis=-1, keepdims=True) for p in ps]
            os_ = [_dot(p.astype(BF16), vs_ref[pl.ds(kb, win), :]) for p, (_, _, kb, _) in zip(ps, geo)]
            for (r, q0, _, _), o2, m, l in zip(geo, os_, ms, ls):
                o2 = o2 / l
                lse2 = jnp.broadcast_to(m + jnp.log2(l), o2.shape)
                rows = pl.ds(r + dil * q0, tq, stride=dil) if dil > 1 else pl.ds(q0, tq)
                ob_refs[c][rows, :] = _pair_merge(o2, tq)
                lb_refs[c][rows, :] = _pair_merge(lse2, tq)
            return carry

        lax.fori_loop(0, seq // (tq * DIL_UNROLL), body, 0)

    chunk = 512

    def merge(i, carry):
        rows = pl.ds(pl.multiple_of(i * chunk, chunk), chunk)
        lses = [lb[rows, :] for lb in lb_refs]
        mx = functools.reduce(jnp.maximum, lses)
        es = [jnp.exp2(l - mx) for l in lses]
        num = sum(e * ob[rows, :] for e, ob in zip(es, ob_refs))
        o_ref[rows, :] = (num / sum(es)).astype(o_ref.dtype)
        return carry

    lax.fori_loop(0, seq // chunk, merge, 0)


def _dilated(dil_proj, batch, seq):
    n_pairs = DIL_HEADS // 2
    configs = tuple((window // 2 // dil, dil) for window, dil in DIL_CONFIGS)
    tiles = []
    for half, dil in configs:
        tq = min(DIL_TQ, seq // dil)
        tiles.append((tq, min(tq + 2 * half, seq // dil)))
        assert seq % (dil * tq) == 0 and tq % half == 0
    biases = [_band_bias(tq, half, win) for (half, _), (tq, win) in zip(configs, tiles)]

    def spec(base):
        return pl.BlockSpec((None, seq, LANES), lambda b, p: (b, 0, base + p))

    return pl.pallas_call(
        functools.partial(_dilated_kernel, seq=seq, configs=configs, tiles=tuple(tiles)),
        grid=(batch, n_pairs),
        in_specs=[spec(0), spec(4), spec(8)] + [pl.BlockSpec(b.shape, lambda b_, p: (0, 0, 0)) for b in biases],
        out_specs=pl.BlockSpec((None, seq, LANES), lambda b, p: (b, 0, p)),
        out_shape=jax.ShapeDtypeStruct((batch, seq, n_pairs * LANES), BF16),
        scratch_shapes=[pltpu.VMEM((seq, LANES), BF16)] * 3 + [pltpu.VMEM((seq, LANES), F32)] * (2 * len(configs) + 3),
        compiler_params=_params("parallel", "parallel"),
        name="dilated",
    )(dil_proj, dil_proj, dil_proj, *biases)


def _na_group_geometry(rows, kr):
    n_groups = rows // NA_GROUP
    wr = -(-(kr + NA_GROUP - 1) * GRID_W // LANES) * LANES // GRID_W
    g = np.arange(n_groups)
    start = np.minimum(np.clip(g * NA_GROUP - kr // 2, 0, rows - kr), rows - wr)
    return n_groups, wr, start


def _na_kernel(start_ref, kind_ref, q_ref, k_ref, v_ref, bias_ref, o_ref, *, rows, kr):
    n_groups, wr, _ = _na_group_geometry(rows, kr)
    tq = NA_GROUP * GRID_W

    def body(i, carry):
        geo = []
        for u in range(NA_UNROLL):
            g = i * NA_UNROLL + u
            geo.append((pl.multiple_of(g * tq, tq), pl.multiple_of(start_ref[g] * GRID_W, GRID_W), kind_ref[g]))
        ss = [_pair_scores(q_ref[pl.ds(q0, tq), :], k_ref[pl.ds(k0, wr * GRID_W), :]) + bias_ref[kind]
              for q0, k0, kind in geo]
        ms = [jnp.max(s, axis=-1, keepdims=True) for s in ss]
        ps = [jnp.exp2(s - m) for s, m in zip(ss, ms)]
        ls = [jnp.sum(p, axis=-1, keepdims=True) for p in ps]
        os_ = [_dot(p.astype(BF16), v_ref[pl.ds(k0, wr * GRID_W), :]) for p, (_, k0, _) in zip(ps, geo)]
        for (q0, _, _), o2, l in zip(geo, os_, ls):
            o_ref[pl.ds(q0, tq), :] = _pair_merge(o2 / l, tq).astype(o_ref.dtype)
        return carry

    lax.fori_loop(0, n_groups // NA_UNROLL, body, 0)


def _na_bias_table(rpb, rows, kr):
    n_groups, wr, start = _na_group_geometry(rows, kr)
    qcol = np.arange(GRID_W)
    kcol = np.arange(GRID_W)
    wstart = np.clip(qcol - NA_COLS // 2, 0, GRID_W - NA_COLS)
    col_ok = (kcol[None, :] >= wstart[:, None]) & (kcol[None, :] < wstart[:, None] + NA_COLS)
    dc = np.clip(kcol[None, :] - qcol[:, None] + NA_COLS - 1, 0, 2 * NA_COLS - 2)

    def geometry(g):
        qrow = g * NA_GROUP + np.arange(NA_GROUP)
        krow = start[g] + np.arange(wr)
        rs = np.clip(qrow - kr // 2, 0, rows - kr)
        row_ok = (krow[None, :] >= rs[:, None]) & (krow[None, :] < rs[:, None] + kr)
        dr = np.where(row_ok, krow[None, :] - qrow[:, None] + NA_ROWS - 1, -1)
        return dr

    patterns, kind_of = [], []
    for g in range(n_groups):
        dr = geometry(g)
        match = [k for k, p in enumerate(patterns) if np.array_equal(p, dr)]
        if not match:
            patterns.append(dr)
        kind_of.append(match[0] if match else len(patterns) - 1)
    onehot = (dc[None] == np.arange(2 * NA_COLS - 1)[:, None, None]) & col_ok[None]
    planes = jnp.einsum("hrd,dqk->hrqk", rpb * LOG2_E, jnp.asarray(onehot, F32), precision=lax.Precision.HIGHEST)
    planes = planes + jnp.asarray(np.where(col_ok, 0.0, NEG_INF), F32)
    masked = jnp.full(planes.shape[:1] + planes.shape[2:], NEG_INF, F32)
    slabs = []
    for dr in patterns:
        rows_i = []
        for i in range(NA_GROUP):
            cols_j = [planes[:, int(dr[i, j])] if dr[i, j] >= 0 else masked for j in range(wr)]
            rows_i.append(jnp.concatenate(cols_j, axis=-1))
        slab = jnp.stack(rows_i, axis=1)
        slabs.append(slab.reshape(NA_HEADS // 2, 2 * NA_GROUP * GRID_W, wr * GRID_W))
    table = jnp.stack(slabs)
    return table, jnp.asarray(start, jnp.int32), jnp.asarray(kind_of, jnp.int32)


def _neighbourhood(proj, rpb, batch, seq):
    rows = seq // GRID_W
    kr = min(NA_ROWS, rows)
    assert rows % (NA_GROUP * NA_UNROLL) == 0 and rows >= kr + NA_GROUP
    table, start, kind_of = _na_bias_table(rpb.astype(F32), rows, kr)
    n_pairs = NA_HEADS // 2

    def spec(base):
        return pl.BlockSpec((None, seq, LANES), lambda b, p: (b, 0, base + p))

    smem = pl.BlockSpec(memory_space=pltpu.SMEM)
    return pl.pallas_call(
        functools.partial(_na_kernel, rows=rows, kr=kr),
        grid=(batch, n_pairs),
        in_specs=[smem, smem, spec(0), spec(4), spec(8),
                  pl.BlockSpec((table.shape[0], None) + table.shape[2:], lambda b, p: (0, p, 0, 0))],
        out_specs=pl.BlockSpec((None, seq, LANES), lambda b, p: (b, 0, p)),
        out_shape=jax.ShapeDtypeStruct((batch, seq, n_pairs * LANES), BF16),
        compiler_params=_params("parallel", "parallel"),
        name="neighbourhood",
    )(start, kind_of, proj, proj, proj, table)


_ODD_W_COLS = 14 * LANES
_ODD_OUT_COLS = 28 * LANES


def _odd_proj_kernel(x_ref, g_ref, w_ref, qn_ref, wqb_ref, kvn_ref, wkb_ref, wvb_ref,
                     cos_ref, sin_ref, cosr_ref, sinr_ref, o_ref):
    hn = _rms(x_ref[...], g_ref[...]).astype(BF16)
    cos, sin = cos_ref[...], sin_ref[...]
    cosr, sinr = cosr_ref[...], sinr_ref[...]
    lane = lax.broadcasted_iota(jnp.int32, cos.shape, 1)
    first_half = (lane % HEAD_DIM) < HEAD_DIM // 2
    first_half_r = lane < MLA_NOPE + MLA_ROPE // 2
    swa_scale = HEAD_DIM ** -0.5 * LOG2_E
    mla_scale = (MLA_NOPE + MLA_ROPE) ** -0.5 * LOG2_E

    def col(j, n=1):
        return slice(j * LANES, (j + n) * LANES)

    def rope(blk):
        return _rope_block(blk, cos, sin, first_half, LANES - HEAD_DIM // 2, HEAD_DIM // 2)

    def rope_r(blk):
        return _rope_block(blk, cosr, sinr, first_half_r, LANES - MLA_ROPE // 2, MLA_ROPE // 2)

    def halves(wide):
        return wide[:, :LANES], wide[:, LANES:]

    latent = _dot(hn, w_ref[:, col(8, 6)])
    for j in range(0, 6, 2):
        for jj, blk in zip((j, j + 1), halves(_dot(hn, w_ref[:, col(j, 2)]))):
            blk = rope(blk) * swa_scale if jj < 4 else rope(blk)
            o_ref[:, col(jj)] = blk.astype(BF16)

    q_an = _rms(latent[:, :3 * LANES], qn_ref[...]).astype(BF16)
    kv_an = _rms(latent[:, 3 * LANES:5 * LANES], kvn_ref[...]).astype(BF16)
    k_pe = rope_r(latent[:, 5 * LANES:])
    for h in range(0, MLA_HEADS, 2):
        for hh, qh in zip((h, h + 1), halves(_dot(q_an, wqb_ref[:, col(h, 2)]))):
            o_ref[:, col(8 + hh)] = (rope_r(qh) * mla_scale).astype(BF16)
    for h in range(0, MLA_HEADS, 2):
        for hh, kh in zip((h, h + 1), halves(_dot(kv_an, wkb_ref[:, col(h, 2)]))):
            o_ref[:, col(16 + hh)] = (kh + k_pe).astype(BF16)
    o_ref[:, col(6, 2)] = _dot(hn, w_ref[:, col(6, 2)]).astype(BF16)
    for j in range(0, MLA_HEADS // 2, 2):
        o_ref[:, col(24 + j, 2)] = _dot(kv_an, wvb_ref[:, col(j, 2)]).astype(BF16)


def _odd_proj(h2, g, w, qn, wqb, kvn, wkb, wvb, cos, sin, cosr, sinr, seq):
    m, d = h2.shape
    tm = FFN_TM
    per_seq = seq // tm
    full = lambda a: pl.BlockSpec(a.shape, lambda i: (0, 0))
    tab = pl.BlockSpec((tm, LANES), lambda i: (i % per_seq, 0))
    return pl.pallas_call(
        _odd_proj_kernel,
        grid=(m // tm,),
        in_specs=[pl.BlockSpec((tm, d), lambda i: (i, 0)), full(g), full(w), full(qn), full(wqb),
                  full(kvn), full(wkb), full(wvb), tab, tab, tab, tab],
        out_specs=pl.BlockSpec((tm, _ODD_OUT_COLS), lambda i: (i, 0)),
        out_shape=jax.ShapeDtypeStruct((m, _ODD_OUT_COLS), BF16),
        compiler_params=_params("parallel"),
        name="odd_proj",
    )(h2, g, w, qn, wqb, kvn, wkb, wvb, cos, sin, cosr, sinr)


def _odd_weights(w_in, w_qb, w_kvb):
    d = w_in.shape[0]
    c = np.cumsum([0, SWA_Q_HEADS * HEAD_DIM, SWA_KV_HEADS * HEAD_DIM, SWA_KV_HEADS * HEAD_DIM,
                   MLA_Q_RANK, MLA_KV_RANK, MLA_ROPE])

    def dup(w):
        w = w.reshape(d, SWA_KV_HEADS, 1, HEAD_DIM)
        return jnp.broadcast_to(w, (d, SWA_KV_HEADS, 2, HEAD_DIM)).reshape(d, SWA_KV_HEADS * LANES)

    w_kpe = jnp.pad(w_in[:, c[5]:c[6]], ((0, 0), (MLA_NOPE, LANES - MLA_NOPE - MLA_ROPE)))
    w = jnp.concatenate([w_in[:, c[0]:c[1]], dup(w_in[:, c[1]:c[2]]), dup(w_in[:, c[2]:c[3]]),
                         w_in[:, c[3]:c[4]], w_in[:, c[4]:c[5]], w_kpe], axis=1).astype(BF16)
    qk = MLA_NOPE + MLA_ROPE
    wqb = jnp.pad(w_qb.reshape(MLA_Q_RANK, MLA_HEADS, qk), ((0, 0), (0, 0), (0, LANES - qk)))
    wqb = wqb.reshape(MLA_Q_RANK, MLA_HEADS * LANES).astype(BF16)
    kv = w_kvb.reshape(MLA_KV_RANK, MLA_HEADS, MLA_NOPE + MLA_V)
    wkb = jnp.pad(kv[:, :, :MLA_NOPE], ((0, 0), (0, 0), (0, LANES - MLA_NOPE)))
    wkb = wkb.reshape(MLA_KV_RANK, MLA_HEADS * LANES).astype(BF16)
    wvb = kv[:, :, MLA_NOPE:].reshape(MLA_KV_RANK, MLA_HEADS * MLA_V).astype(BF16)
    return w, wqb, wkb, wvb


def _swa_kernel(sink_ref, q_ref, k_ref, v_ref, bias_ref, o_ref, *, seq, half, tq, win):
    pair = pl.program_id(1)
    first = lax.broadcasted_iota(jnp.int32, (2 * tq, 1), 0) < tq
    sink = jnp.where(first, sink_ref[2 * pair], sink_ref[2 * pair + 1]) * LOG2_E

    def body(i, carry):
        geo = []
        for g in range(SWA_UNROLL):
            q0 = pl.multiple_of((i * SWA_UNROLL + g) * tq, tq)
            geo.append((q0,) + _band_window(q0, tq, half, seq, win))
        ss = [_pair_scores(q_ref[pl.ds(q0, tq), :], k_ref[pl.ds(ks, win), :]) + bias_ref[kind]
              for q0, ks, kind in geo]
        ms = [jnp.maximum(jnp.max(s, axis=-1, keepdims=True), sink) for s in ss]
        ps = [jnp.exp2(s - m) for s, m in zip(ss, ms)]
        ls = [jnp.sum(p, axis=-1, keepdims=True) + jnp.exp2(sink - m) for p, m in zip(ps, ms)]
        os_ = [_dot(p.astype(BF16), v_ref[pl.ds(ks, win), :]) for p, (_, ks, _) in zip(ps, geo)]
        for (q0, _, _), o2, l in zip(geo, os_, ls):
            o_ref[pl.ds(q0, tq), :] = _pair_merge(o2 / l, tq).astype(o_ref.dtype)
        return carry

    lax.fori_loop(0, seq // (tq * SWA_UNROLL), body, 0)


def _swa(proj, sink, batch, seq):
    tq = SWA_TQ
    win = tq + 2 * SWA_HALF
    assert seq % (tq * SWA_UNROLL) == 0 and tq % SWA_HALF == 0 and seq >= win
    n_pairs = SWA_Q_HEADS // 2
    pairs_per_kv = n_pairs // SWA_KV_HEADS
    bias = _band_bias(tq, SWA_HALF, win)
    return pl.pallas_call(
        functools.partial(_swa_kernel, seq=seq, half=SWA_HALF, tq=tq, win=win),
        grid=(batch, n_pairs),
        in_specs=[pl.BlockSpec(memory_space=pltpu.SMEM),
                  pl.BlockSpec((None, seq, LANES), lambda b, p: (b, 0, p)),
                  pl.BlockSpec((None, seq, LANES), lambda b, p: (b, 0, 4 + p // pairs_per_kv)),
                  pl.BlockSpec((None, seq, LANES), lambda b, p: (b, 0, 6 + p // pairs_per_kv)),
                  pl.BlockSpec(bias.shape, lambda b, p: (0, 0, 0))],
        out_specs=pl.BlockSpec((None, seq, LANES), lambda b, p: (b, 0, p)),
        out_shape=jax.ShapeDtypeStruct((batch, seq, n_pairs * LANES), BF16),
        compiler_params=_params("parallel", "parallel"),
        name="swa",
    )(sink, proj, proj, proj, bias)


def _mla_kernel(q_ref, k_ref, v_ref, o_ref, s_ref, p_ref, *, seq, tk):
    tq = MLA_TQ
    groups = tk // LANES
    chunks = [slice(c * tk, (c + 1) * tk) for c in range(seq // tk)]
    n_units = 2 * (q_ref.shape[1] // tq)
    first = _lane_is_first_head((tq, LANES))

    def scores(u):
        t, h = divmod(u, 2)
        q = q_ref[0, t * tq:(t + 1) * tq, h * LANES:(h + 1) * LANES]
        m_l = jnp.full((tq, LANES), NEG_INF, F32)
        for ck in chunks:
            s = _dot_nt(q, k_ref[0, ck, h * LANES:(h + 1) * LANES])
            s_ref[u % 2, :, ck] = s
            for g in range(groups):
                m_l = jnp.maximum(m_l, s[:, g * LANES:(g + 1) * LANES])
        return jnp.max(m_l, axis=-1, keepdims=True)

    def probs(u, m):
        l_l = jnp.zeros((tq, LANES), F32)
        for ck in chunks:
            p = jnp.exp2(s_ref[u % 2, :, ck] - m)
            for g in range(groups):
                l_l = l_l + p[:, g * LANES:(g + 1) * LANES]
            p_ref[u % 2, :, ck] = p.astype(BF16)
        return jnp.sum(l_l, axis=-1, keepdims=True)

    m = scores(0)
    out_a = None
    for u in range(n_units):
        m_next = scores(u + 1) if u + 1 < n_units else None
        l = probs(u, m)
        out = _dot(p_ref[u % 2], v_ref[0]) / l
        if u % 2 == 0:
            out_a = out
        else:
            t = u // 2
            o_ref[0, t * tq:(t + 1) * tq, :] = jnp.where(first, out_a, out).astype(o_ref.dtype)
        m = m_next


def _mla(proj, batch, seq):
    tq = MLA_TQ * MLA_SUB
    assert seq % tq == 0 and seq % MLA_TK == 0
    n_pairs = MLA_HEADS // 2
    return pl.pallas_call(
        functools.partial(_mla_kernel, seq=seq, tk=MLA_TK),
        grid=(batch, n_pairs, seq // tq),
        in_specs=[pl.BlockSpec((1, tq, 2 * LANES), lambda b, p, i: (b, i, 4 + p)),
                  pl.BlockSpec((1, seq, 2 * LANES), lambda b, p, i: (b, 0, 8 + p)),
                  pl.BlockSpec((1, seq, LANES), lambda b, p, i: (b, 0, 24 + p))],
        out_specs=pl.BlockSpec((1, tq, LANES), lambda b, p, i: (b, i, p)),
        out_shape=jax.ShapeDtypeStruct((batch, seq, n_pairs * LANES), BF16),
        scratch_shapes=[pltpu.VMEM((2, MLA_TQ, seq), F32), pltpu.VMEM((2, MLA_TQ, seq), BF16)],
        compiler_params=_params("parallel", "parallel", "parallel"),
        name="mla",
    )(proj, proj, proj)


def _rope_tables(seq):
    pos = jnp.arange(seq, dtype=F32)

    def tables(dim):
        inv_freq = ROPE_THETA ** (-jnp.arange(0, dim, 2, dtype=F32) / dim)
        ang = pos[:, None] * inv_freq[None, :]
        return jnp.cos(ang), jnp.sin(ang)

    cos, sin = tables(HEAD_DIM)
    reps = LANES // HEAD_DIM
    cos_t = jnp.tile(jnp.concatenate([cos, cos], axis=1), (1, reps))
    sin_t = jnp.tile(jnp.concatenate([-sin, sin], axis=1), (1, reps))
    cos_r, sin_r = tables(MLA_ROPE)
    pad_l, pad_r = MLA_NOPE, LANES - MLA_NOPE - MLA_ROPE
    cos_rt = jnp.pad(jnp.concatenate([cos_r, cos_r], axis=1), ((0, 0), (pad_l, pad_r)), constant_values=1.0)
    sin_rt = jnp.pad(jnp.concatenate([-sin_r, sin_r], axis=1), ((0, 0), (pad_l, pad_r)))
    return cos_t, sin_t, cos_rt, sin_rt


def kernel(x, ffn1_norm, ffn1_w1, ffn1_w3, ffn1_w2, mix_norm, ffn2_norm, ffn2_w1, ffn2_w3, ffn2_w2, even_w_in, even_w_out, na_rel_bias, odd_w_in, odd_w_out, swa_sink, mla_q_norm, mla_w_qb, mla_kv_norm, mla_w_kvb, final_norm):
    batch, seq, d = x.shape
    depth = ffn1_norm.shape[0]
    assert d == D_MODEL and seq % FFN_TM == 0 and seq % (GRID_W * NA_ROWS) == 0
    cos_t, sin_t, cos_rt, sin_rt = _rope_tables(seq)
    h = x.reshape(batch * seq, d)
    row = lambda v: v.reshape(1, -1).astype(F32)
    fg = row(final_norm)

    ffn1_w = _ffn_weights(ffn1_w1, ffn1_w3, ffn1_w2)
    ffn2_w = _ffn_weights(ffn2_w1, ffn2_w3, ffn2_w2)
    for i in range(depth):
        j = i // 2
        h = _ffn(h, row(ffn1_norm[i]), *ffn1_w, i, fg, False)
        if i % 2 == 0:
            dil_proj, na_proj = _even_proj(h, row(mix_norm[i]), even_w_in[j].astype(BF16), cos_t, sin_t, seq)
            oa = _dilated(dil_proj.reshape(batch, seq, -1), batch, seq).reshape(batch * seq, -1)
            on = _neighbourhood(na_proj.reshape(batch, seq, -1), na_rel_bias[j], batch, seq).reshape(batch * seq, -1)
            mix = (oa, on, even_w_out[j].astype(BF16))
        else:
            w, wqb, wkb, wvb = _odd_weights(odd_w_in[j], mla_w_qb[j], mla_w_kvb[j])
            proj = _odd_proj(h, row(mix_norm[i]), w, row(mla_q_norm[j]), wqb, row(mla_kv_norm[j]), wkb, wvb,
                             cos_t, sin_t, cos_rt, sin_rt, seq)
            proj = proj.reshape(batch, seq, -1)
            oc = _swa(proj, swa_sink[j].astype(F32), batch, seq).reshape(batch * seq, -1)
            od = _mla(proj, batch, seq).reshape(batch * seq, -1)
            mix = (oc, od, odd_w_out[j].astype(BF16))
        h = _ffn(h, row(ffn2_norm[i]), *ffn2_w, i, fg, i == depth - 1, mix)
    return h.reshape(batch, seq, d)
```

```python
import functools

import jax
import jax.numpy as jnp
import numpy as np
from jax import lax
from jax.experimental import pallas as pl
from jax.experimental.pallas import tpu as pltpu

D_MODEL = 1024
D_FF = 2816
HEAD_DIM = 64
ROPE_THETA = 10000.0
NORM_EPS = 1e-6
NEG_INF = -1e30
LOG2_E = 1.4426950408889634

DIL_HEADS = 8
DIL_CONFIGS = ((128, 1), (512, 4), (2048, 16))
NA_HEADS = 8
GRID_W = 64
NA_ROWS = 8
NA_COLS = 16

SWA_Q_HEADS = 8
SWA_KV_HEADS = 2
SWA_HALF = 128
MLA_HEADS = 8
MLA_Q_RANK = 384
MLA_KV_RANK = 256
MLA_NOPE = 64
MLA_ROPE = 32
MLA_V = 64

LANES = 128
SUBLANES = 8
VMEM_LIMIT = 56 * 1024 * 1024

FFN_TM = 512
FFN_TF = 256
FFN_SUB = 2
MLA_TQ = 256
MLA_SUB = 4
MLA_TK = 1024
DIL_TQ = 128
DIL_UNROLL = 8
SWA_TQ = 128
SWA_UNROLL = 4
NA_GROUP = 1
NA_UNROLL = 8

BF16 = jnp.bfloat16
F32 = jnp.float32


def _params(*sem):
    return pltpu.CompilerParams(dimension_semantics=sem, vmem_limit_bytes=VMEM_LIMIT)


def _rms(x, g):
    ms = jnp.mean(x * x, axis=-1, keepdims=True)
    return x * lax.rsqrt(ms + NORM_EPS) * g


def _dot(a, b):
    return jnp.dot(a, b, preferred_element_type=F32)


def _dot_nt(a, b):
    return lax.dot_general(a, b, (((1,), (1,)), ((), ())), preferred_element_type=F32)


def _lane_is_first_head(shape):
    return lax.broadcasted_iota(jnp.int32, shape, len(shape) - 1) < HEAD_DIM


def _rope_block(x, cos, sin_signed, first_half, up, down):
    partner = jnp.where(first_half, pltpu.roll(x, up, 1), pltpu.roll(x, down, 1))
    return x * cos + partner * sin_signed


def _ffn_kernel(*refs, mix, final):
    if mix:
        x_ref, oc_ref, od_ref, wa_ref, wb_ref, g_ref, w1_ref, w3_ref, w2_ref, fg_ref, o_ref, a_ref = refs
    else:
        x_ref, g_ref, w1_ref, w3_ref, w2_ref, fg_ref, o_ref, a_ref = refs
    tm = x_ref.shape[0] // FFN_SUB
    for t in range(FFN_SUB):
        rows = slice(t * tm, (t + 1) * tm)
        x = x_ref[rows, :]
        if mix:
            x = x + _dot(oc_ref[rows, :], wa_ref[...]) + _dot(od_ref[rows, :], wb_ref[...])
        hn = _rms(x, g_ref[...]).astype(BF16)
        for c in range(a_ref.shape[1] // FFN_TF):
            cols = slice(c * FFN_TF, (c + 1) * FFN_TF)
            h1 = _dot(hn, w1_ref[:, cols])
            h3 = _dot(hn, w3_ref[:, cols])
            a_ref[rows, cols] = (h1 * jax.nn.sigmoid(h1) * h3).astype(BF16)
        y = x + 0.5 * _dot(a_ref[rows, :], w2_ref[...])
        if final:
            y = _rms(y, fg_ref[...])
        o_ref[rows, :] = y


def _ffn(x2, g, w1, w3, w2, layer, fg, final, mix=None):
    m, d = x2.shape
    f = w1.shape[2]
    assert f % FFN_TF == 0
    tm = FFN_TM * FFN_SUB
    const = lambda i: (0, 0)
    of_layer = lambda i: (layer, 0, 0)
    resident = dict(pipeline_mode=pl.Buffered(1))
    mix_args, mix_specs = (), []
    if mix is not None:
        oc, od, w_out = mix
        half_w = w_out.shape[0] // 2
        mix_args = (oc, od, w_out[:half_w], w_out[half_w:])
        mix_specs = [pl.BlockSpec((tm, half_w), lambda i: (i, 0))] * 2 + [pl.BlockSpec((half_w, d), const, **resident)] * 2
    return pl.pallas_call(
        functools.partial(_ffn_kernel, mix=mix is not None, final=final),
        grid=(m // tm,),
        in_specs=[pl.BlockSpec((tm, d), lambda i: (i, 0))] + mix_specs + [
            pl.BlockSpec((1, d), const),
            pl.BlockSpec((None, d, f), of_layer, **resident),
            pl.BlockSpec((None, d, f), of_layer, **resident),
            pl.BlockSpec((None, f, d), of_layer, **resident),
            pl.BlockSpec((1, d), const),
        ],
        out_specs=pl.BlockSpec((tm, d), lambda i: (i, 0)),
        out_shape=jax.ShapeDtypeStruct((m, d), F32),
        scratch_shapes=[pltpu.VMEM((tm, f), BF16)],
        compiler_params=_params("parallel"),
        name="ffn",
    )(x2, *mix_args, g, w1, w3, w2, fg)


def _ffn_weights(w1, w3, w2):
    return w1.astype(BF16), w3.astype(BF16), w2.astype(BF16)


_EVEN_DIL_BLOCKS = 12
_EVEN_Q_BLOCKS = tuple(range(0, 4)) + tuple(range(12, 16))
_EVEN_ROPE_BLOCKS = 8


def _even_proj_kernel(x_ref, g_ref, w_ref, cos_ref, sin_ref, dil_ref, na_ref, *, n_blocks):
    hn = _rms(x_ref[...], g_ref[...]).astype(BF16)
    cos = cos_ref[...]
    sin = sin_ref[...]
    lane = lax.broadcasted_iota(jnp.int32, cos.shape, 1)
    first_half = (lane % HEAD_DIM) < HEAD_DIM // 2
    scale = HEAD_DIM ** -0.5 * LOG2_E
    for j2 in range(0, n_blocks, 2):
        wide = _dot(hn, w_ref[:, j2 * LANES:(j2 + 2) * LANES])
        for j in (j2, j2 + 1):
            blk = wide[:, (j - j2) * LANES:(j - j2 + 1) * LANES]
            if j < _EVEN_ROPE_BLOCKS:
                blk = _rope_block(blk, cos, sin, first_half, LANES - HEAD_DIM // 2, HEAD_DIM // 2)
            if j in _EVEN_Q_BLOCKS:
                blk = blk * scale
            if j < _EVEN_DIL_BLOCKS:
                dil_ref[:, j * LANES:(j + 1) * LANES] = blk
            else:
                jn = j - _EVEN_DIL_BLOCKS
                na_ref[:, jn * LANES:(jn + 1) * LANES] = blk.astype(BF16)


def _even_proj(h2, g, w_in, cos, sin, seq):
    m, d = h2.shape
    n_out = w_in.shape[1]
    n_dil = _EVEN_DIL_BLOCKS * LANES
    tm = FFN_TM
    per_seq = seq // tm
    return pl.pallas_call(
        functools.partial(_even_proj_kernel, n_blocks=n_out // LANES),
        grid=(m // tm,),
        in_specs=[
            pl.BlockSpec((tm, d), lambda i: (i, 0)),
            pl.BlockSpec((1, d), lambda i: (0, 0)),
            pl.BlockSpec((d, n_out), lambda i: (0, 0)),
            pl.BlockSpec((tm, LANES), lambda i: (i % per_seq, 0)),
            pl.BlockSpec((tm, LANES), lambda i: (i % per_seq, 0)),
        ],
        out_specs=[pl.BlockSpec((tm, n_dil), lambda i: (i, 0)),
                   pl.BlockSpec((tm, n_out - n_dil), lambda i: (i, 0))],
        out_shape=[jax.ShapeDtypeStruct((m, n_dil), F32),
                   jax.ShapeDtypeStruct((m, n_out - n_dil), BF16)],
        compiler_params=_params("parallel"),
        name="even_proj",
    )(h2, g, w_in, cos, sin)


def _pair_scores(q, k):
    first = _lane_is_first_head(q.shape)
    zero = jnp.zeros_like(q)
    q2 = jnp.concatenate([jnp.where(first, q, zero), jnp.where(first, zero, q)], axis=0)
    return _dot_nt(q2, k)


def _pair_merge(x2, tq):
    first = _lane_is_first_head((tq, LANES))
    return jnp.where(first, x2[:tq], x2[tq:])


def _band_bias(tq, half, win):
    row = np.arange(2 * tq)[:, None] % tq
    col = np.arange(win)[None, :]
    kinds = [np.where(np.abs(col - row - k * half) <= half, 0.0, NEG_INF) for k in range(3)]
    return jnp.asarray(np.stack(kinds), F32)


def _band_window(q0, tq, half, length, win):
    ks = pl.multiple_of(jnp.clip(q0 - half, 0, length - win), half)
    return ks, (q0 - ks) // half


def _dilated_kernel(q_ref, k_ref, v_ref, *refs, seq, configs, tiles):
    n = len(configs)
    bias_refs, o_ref = refs[:n], refs[n]
    qs_ref, ks_ref, vs_ref = refs[n + 1:n + 4]
    ob_refs, lb_refs = refs[n + 4:2 * n + 4], refs[2 * n + 4:3 * n + 4]
    stage_refs = refs[3 * n + 4:]
    copies = {1: (q_ref, k_ref, v_ref)}
    for c, ((half, dil), (tq, win)) in enumerate(zip(configs, tiles)):
        length = seq // dil
        blocks = length // tq
        base_dil = max(d for d in copies if dil % d == 0)
        step = dil // base_dil
        keep = dil > 1 and any(d2 > dil and d2 % dil == 0 for _, d2 in configs)
        if keep:
            copies[dil] = stage_refs
        for a, (src, dst) in enumerate(zip(copies[base_dil], (qs_ref, ks_ref, vs_ref))):
            for r in range(dil):
                start = (r % base_dil) * (seq // base_dil) + r // base_dil
                rows = pl.ds(start, length, stride=step) if step > 1 else pl.ds(start, length)
                x = src[rows, :]
                dst[r * length:(r + 1) * length, :] = x.astype(BF16)
                if keep:
                    stage_refs[a][r * length:(r + 1) * length, :] = x

        def body(i, carry, c=c, half=half, dil=dil, tq=tq, win=win, length=length, blocks=blocks):
            geo = []
            for g in range(DIL_UNROLL):
                n = i * DIL_UNROLL + g
                r = n // blocks
                q0 = pl.multiple_of((n % blocks) * tq, tq)
                ks, kind = _band_window(q0, tq, half, length, win)
                geo.append((r, q0, pl.multiple_of(r * length, tq) + ks, kind))
            ss = [_pair_scores(qs_ref[pl.ds(pl.multiple_of(r * length, tq) + q0, tq), :], ks_ref[pl.ds(kb, win), :])
                  + bias_refs[c][kind] for r, q0, kb, kind in geo]
            ms = [jnp.max(s, axis=-1, keepdims=True) for s in ss]
            ps = [jnp.exp2(s - m) for s, m in zip(ss, ms)]
            ls = [jnp.sum(p, axis=-1, keepdims=True) for p in ps]
            os_ = [_dot(p.astype(BF16), vs_ref[pl.ds(kb, win), :]) for p, (_, _, kb, _) in zip(ps, geo)]
            for (r, q0, _, _), o2, m, l in zip(geo, os_, ms, ls):
                o2 = o2 / l
                lse2 = jnp.broadcast_to(m + jnp.log2(l), o2.shape)
                rows = pl.ds(r + dil * q0, tq, stride=dil) if dil > 1 else pl.ds(q0, tq)
                ob_refs[c][rows, :] = _pair_merge(o2, tq)
                lb_refs[c][rows, :] = _pair_merge(lse2, tq)
            return carry

        lax.fori_loop(0, seq // (tq * DIL_UNROLL), body, 0)

    chunk = 512

    def merge(i, carry):
        rows = pl.ds(pl.multiple_of(i * chunk, chunk), chunk)
        lses = [lb[rows, :] for lb in lb_refs]
        mx = functools.reduce(jnp.maximum, lses)
        es = [jnp.exp2(l - mx) for l in lses]
        num = sum(e * ob[rows, :] for e, ob in zip(es, ob_refs))
        o_ref[rows, :] = (num / sum(es)).astype(o_ref.dtype)
        return carry

    lax.fori_loop(0, seq // chunk, merge, 0)


def _dilated(dil_proj, batch, seq):
    n_pairs = DIL_HEADS // 2
    configs = tuple((window // 2 // dil, dil) for window, dil in DIL_CONFIGS)
    tiles = []
    for half, dil in configs:
        tq = min(DIL_TQ, seq // dil)
        tiles.append((tq, min(tq + 2 * half, seq // dil)))
        assert seq % (dil * tq) == 0 and tq % half == 0
    biases = [_band_bias(tq, half, win) for (half, _), (tq, win) in zip(configs, tiles)]

    def spec(base):
        return pl.BlockSpec((None, seq, LANES), lambda b, p: (b, 0, base + p))

    return pl.pallas_call(
        functools.partial(_dilated_kernel, seq=seq, configs=configs, tiles=tuple(tiles)),
        grid=(batch, n_pairs),
        in_specs=[spec(0), spec(4), spec(8)] + [pl.BlockSpec(b.shape, lambda b_, p: (0, 0, 0)) for b in biases],
        out_specs=pl.BlockSpec((None, seq, LANES), lambda b, p: (b, 0, p)),
        out_shape=jax.ShapeDtypeStruct((batch, seq, n_pairs * LANES), BF16),
        scratch_shapes=[pltpu.VMEM((seq, LANES), BF16)] * 3 + [pltpu.VMEM((seq, LANES), F32)] * (2 * len(configs) + 3),
        compiler_params=_params("parallel", "parallel"),
        name="dilated",
    )(dil_proj, dil_proj, dil_proj, *biases)


def _na_group_geometry(rows, kr):
    n_groups = rows // NA_GROUP
    wr = -(-(kr + NA_GROUP - 1) * GRID_W // LANES) * LANES // GRID_W
    g = np.arange(n_groups)
    start = np.minimum(np.clip(g * NA_GROUP - kr // 2, 0, rows - kr), rows - wr)
    return n_groups, wr, start


def _na_kernel(start_ref, kind_ref, q_ref, k_ref, v_ref, bias_ref, o_ref, *, rows, kr):
    n_groups, wr, _ = _na_group_geometry(rows, kr)
    tq = NA_GROUP * GRID_W

    def body(i, carry):
        geo = []
        for u in range(NA_UNROLL):
            g = i * NA_UNROLL + u
            geo.append((pl.multiple_of(g * tq, tq), pl.multiple_of(start_ref[g] * GRID_W, GRID_W), kind_ref[g]))
        ss = [_pair_scores(q_ref[pl.ds(q0, tq), :], k_ref[pl.ds(k0, wr * GRID_W), :]) + bias_ref[kind]
              for q0, k0, kind in geo]
        ms = [jnp.max(s, axis=-1, keepdims=True) for s in ss]
        ps = [jnp.exp2(s - m) for s, m in zip(ss, ms)]
        ls = [jnp.sum(p, axis=-1, keepdims=True) for p in ps]
        os_ = [_dot(p.astype(BF16), v_ref[pl.ds(k0, wr * GRID_W), :]) for p, (_, k0, _) in zip(ps, geo)]
        for (q0, _, _), o2, l in zip(geo, os_, ls):
            o_ref[pl.ds(q0, tq), :] = _pair_merge(o2 / l, tq).astype(o_ref.dtype)
        return carry

    lax.fori_loop(0, n_groups // NA_UNROLL, body, 0)


def _na_bias_table(rpb, rows, kr):
    n_groups, wr, start = _na_group_geometry(rows, kr)
    qcol = np.arange(GRID_W)
    kcol = np.arange(GRID_W)
    wstart = np.clip(qcol - NA_COLS // 2, 0, GRID_W - NA_COLS)
    col_ok = (kcol[None, :] >= wstart[:, None]) & (kcol[None, :] < wstart[:, None] + NA_COLS)
    dc = np.clip(kcol[None, :] - qcol[:, None] + NA_COLS - 1, 0, 2 * NA_COLS - 2)

    def geometry(g):
        qrow = g * NA_GROUP + np.arange(NA_GROUP)
        krow = start[g] + np.arange(wr)
        rs = np.clip(qrow - kr // 2, 0, rows - kr)
        row_ok = (krow[None, :] >= rs[:, None]) & (krow[None, :] < rs[:, None] + kr)
        dr = np.where(row_ok, krow[None, :] - qrow[:, None] + NA_ROWS - 1, -1)
        return dr

    patterns, kind_of = [], []
    for g in range(n_groups):
        dr = geometry(g)
        match = [k for k, p in enumerate(patterns) if np.array_equal(p, dr)]
        if not match:
            patterns.append(dr)
        kind_of.append(match[0] if match else len(patterns) - 1)
    onehot = (dc[None] == np.arange(2 * NA_COLS - 1)[:, None, None]) & col_ok[None]
    planes = jnp.einsum("hrd,dqk->hrqk", rpb * LOG2_E, jnp.asarray(onehot, F32), precision=lax.Precision.HIGHEST)
    planes = planes + jnp.asarray(np.where(col_ok, 0.0, NEG_INF), F32)
    masked = jnp.full(planes.shape[:1] + planes.shape[2:], NEG_INF, F32)
    slabs = []
    for dr in patterns:
        rows_i = []
        for i in range(NA_GROUP):
            cols_j = [planes[:, int(dr[i, j])] if dr[i, j] >= 0 else masked for j in range(wr)]
            rows_i.append(jnp.concatenate(cols_j, axis=-1))
        slab = jnp.stack(rows_i, axis=1)
        slabs.append(slab.reshape(NA_HEADS // 2, 2 * NA_GROUP * GRID_W, wr * GRID_W))
    table = jnp.stack(slabs)
    return table, jnp.asarray(start, jnp.int32), jnp.asarray(kind_of, jnp.int32)


def _neighbourhood(proj, rpb, batch, seq):
    rows = seq // GRID_W
    kr = min(NA_ROWS, rows)
    assert rows % (NA_GROUP * NA_UNROLL) == 0 and rows >= kr + NA_GROUP
    table, start, kind_of = _na_bias_table(rpb.astype(F32), rows, kr)
    n_pairs = NA_HEADS // 2

    def spec(base):
        return pl.BlockSpec((None, seq, LANES), lambda b, p: (b, 0, base + p))

    smem = pl.BlockSpec(memory_space=pltpu.SMEM)
    return pl.pallas_call(
        functools.partial(_na_kernel, rows=rows, kr=kr),
        grid=(batch, n_pairs),
        in_specs=[smem, smem, spec(0), spec(4), spec(8),
                  pl.BlockSpec((table.shape[0], None) + table.shape[2:], lambda b, p: (0, p, 0, 0))],
        out_specs=pl.BlockSpec((None, seq, LANES), lambda b, p: (b, 0, p)),
        out_shape=jax.ShapeDtypeStruct((batch, seq, n_pairs * LANES), BF16),
        compiler_params=_params("parallel", "parallel"),
        name="neighbourhood",
    )(start, kind_of, proj, proj, proj, table)


_ODD_W_COLS = 14 * LANES
_ODD_OUT_COLS = 24 * LANES


def _odd_proj_kernel(x_ref, g_ref, w_ref, qn_ref, wqb_ref, kvn_ref, wkb_ref, wvbt_ref,
                     cos_ref, sin_ref, cosr_ref, sinr_ref, o_ref, vt_ref):
    hn = _rms(x_ref[...], g_ref[...]).astype(BF16)
    cos, sin = cos_ref[...], sin_ref[...]
    cosr, sinr = cosr_ref[...], sinr_ref[...]
    lane = lax.broadcasted_iota(jnp.int32, cos.shape, 1)
    first_half = (lane % HEAD_DIM) < HEAD_DIM // 2
    first_half_r = lane < MLA_NOPE + MLA_ROPE // 2
    swa_scale = HEAD_DIM ** -0.5 * LOG2_E
    mla_scale = (MLA_NOPE + MLA_ROPE) ** -0.5 * LOG2_E

    def col(j, n=1):
        return slice(j * LANES, (j + n) * LANES)

    def rope(blk):
        return _rope_block(blk, cos, sin, first_half, LANES - HEAD_DIM // 2, HEAD_DIM // 2)

    def rope_r(blk):
        return _rope_block(blk, cosr, sinr, first_half_r, LANES - MLA_ROPE // 2, MLA_ROPE // 2)

    def halves(wide):
        return wide[:, :LANES], wide[:, LANES:]

    latent = _dot(hn, w_ref[:, col(8, 6)])
    for j in range(0, 6, 2):
        for jj, blk in zip((j, j + 1), halves(_dot(hn, w_ref[:, col(j, 2)]))):
            blk = rope(blk) * swa_scale if jj < 4 else rope(blk)
            o_ref[:, col(jj)] = blk.astype(BF16)

    q_an = _rms(latent[:, :3 * LANES], qn_ref[...]).astype(BF16)
    kv_an = _rms(latent[:, 3 * LANES:5 * LANES], kvn_ref[...]).astype(BF16)
    k_pe = rope_r(latent[:, 5 * LANES:])
    for h in range(0, MLA_HEADS, 2):
        for hh, qh in zip((h, h + 1), halves(_dot(q_an, wqb_ref[:, col(h, 2)]))):
            o_ref[:, col(8 + hh)] = (rope_r(qh) * mla_scale).astype(BF16)
    for h in range(0, MLA_HEADS, 2):
        for hh, kh in zip((h, h + 1), halves(_dot(kv_an, wkb_ref[:, col(h, 2)]))):
            o_ref[:, col(16 + hh)] = (kh + k_pe).astype(BF16)
    o_ref[:, col(6, 2)] = _dot(hn, w_ref[:, col(6, 2)]).astype(BF16)
    vt_ref[...] = _dot_nt(wvbt_ref[...], kv_an).astype(BF16)


def _odd_proj(h2, g, w, qn, wqb, kvn, wkb, wvbt, cos, sin, cosr, sinr, seq):
    m, d = h2.shape
    tm = FFN_TM
    per_seq = seq // tm
    n_v = wvbt.shape[0]
    full = lambda a: pl.BlockSpec(a.shape, lambda i: (0, 0))
    tab = pl.BlockSpec((tm, LANES), lambda i: (i % per_seq, 0))
    return pl.pallas_call(
        _odd_proj_kernel,
        grid=(m // tm,),
        in_specs=[pl.BlockSpec((tm, d), lambda i: (i, 0)), full(g), full(w), full(qn), full(wqb),
                  full(kvn), full(wkb), full(wvbt), tab, tab, tab, tab],
        out_specs=[pl.BlockSpec((tm, _ODD_OUT_COLS), lambda i: (i, 0)),
                   pl.BlockSpec((None, n_v, tm), lambda i: (i // per_seq, 0, i % per_seq))],
        out_shape=[jax.ShapeDtypeStruct((m, _ODD_OUT_COLS), BF16),
                   jax.ShapeDtypeStruct((m // seq, n_v, seq), BF16)],
        compiler_params=_params("parallel"),
        name="odd_proj",
    )(h2, g, w, qn, wqb, kvn, wkb, wvbt, cos, sin, cosr, sinr)


def _odd_weights(w_in, w_qb, w_kvb):
    d = w_in.shape[0]
    c = np.cumsum([0, SWA_Q_HEADS * HEAD_DIM, SWA_KV_HEADS * HEAD_DIM, SWA_KV_HEADS * HEAD_DIM,
                   MLA_Q_RANK, MLA_KV_RANK, MLA_ROPE])

    def dup(w):
        w = w.reshape(d, SWA_KV_HEADS, 1, HEAD_DIM)
        return jnp.broadcast_to(w, (d, SWA_KV_HEADS, 2, HEAD_DIM)).reshape(d, SWA_KV_HEADS * LANES)

    w_kpe = jnp.pad(w_in[:, c[5]:c[6]], ((0, 0), (MLA_NOPE, LANES - MLA_NOPE - MLA_ROPE)))
    w = jnp.concatenate([w_in[:, c[0]:c[1]], dup(w_in[:, c[1]:c[2]]), dup(w_in[:, c[2]:c[3]]),
                         w_in[:, c[3]:c[4]], w_in[:, c[4]:c[5]], w_kpe], axis=1).astype(BF16)
    qk = MLA_NOPE + MLA_ROPE
    wqb = jnp.pad(w_qb.reshape(MLA_Q_RANK, MLA_HEADS, qk), ((0, 0), (0, 0), (0, LANES - qk)))
    wqb = wqb.reshape(MLA_Q_RANK, MLA_HEADS * LANES).astype(BF16)
    kv = w_kvb.reshape(MLA_KV_RANK, MLA_HEADS, MLA_NOPE + MLA_V)
    wkb = jnp.pad(kv[:, :, :MLA_NOPE], ((0, 0), (0, 0), (0, LANES - MLA_NOPE)))
    wkb = wkb.reshape(MLA_KV_RANK, MLA_HEADS * LANES).astype(BF16)
    wvbt = kv[:, :, MLA_NOPE:].reshape(MLA_KV_RANK, MLA_HEADS * MLA_V).T.astype(BF16)
    return w, wqb, wkb, wvbt


def _swa_kernel(sink_ref, q_ref, k_ref, v_ref, bias_ref, o_ref, *, seq, half, tq, win):
    pair = pl.program_id(1)
    first = lax.broadcasted_iota(jnp.int32, (2 * tq, 1), 0) < tq
    sink = jnp.where(first, sink_ref[2 * pair], sink_ref[2 * pair + 1]) * LOG2_E

    def body(i, carry):
        geo = []
        for g in range(SWA_UNROLL):
            q0 = pl.multiple_of((i * SWA_UNROLL + g) * tq, tq)
            geo.append((q0,) + _band_window(q0, tq, half, seq, win))
        ss = [_pair_scores(q_ref[pl.ds(q0, tq), :], k_ref[pl.ds(ks, win), :]) + bias_ref[kind]
              for q0, ks, kind in geo]
        ms = [jnp.maximum(jnp.max(s, axis=-1, keepdims=True), sink) for s in ss]
        ps = [jnp.exp2(s - m) for s, m in zip(ss, ms)]
        ls = [jnp.sum(p, axis=-1, keepdims=True) + jnp.exp2(sink - m) for p, m in zip(ps, ms)]
        os_ = [_dot(p.astype(BF16), v_ref[pl.ds(ks, win), :]) for p, (_, ks, _) in zip(ps, geo)]
        for (q0, _, _), o2, l in zip(geo, os_, ls):
            o_ref[pl.ds(q0, tq), :] = _pair_merge(o2 / l, tq).astype(o_ref.dtype)
        return carry

    lax.fori_loop(0, seq // (tq * SWA_UNROLL), body, 0)


def _swa(proj, sink, batch, seq):
    tq = SWA_TQ
    win = tq + 2 * SWA_HALF
    assert seq % (tq * SWA_UNROLL) == 0 and tq % SWA_HALF == 0 and seq >= win
    n_pairs = SWA_Q_HEADS // 2
    pairs_per_kv = n_pairs // SWA_KV_HEADS
    bias = _band_bias(tq, SWA_HALF, win)
    return pl.pallas_call(
        functools.partial(_swa_kernel, seq=seq, half=SWA_HALF, tq=tq, win=win),
        grid=(batch, n_pairs),
        in_specs=[pl.BlockSpec(memory_space=pltpu.SMEM),
                  pl.BlockSpec((None, seq, LANES), lambda b, p: (b, 0, p)),
                  pl.BlockSpec((None, seq, LANES), lambda b, p: (b, 0, 4 + p // pairs_per_kv)),
                  pl.BlockSpec((None, seq, LANES), lambda b, p: (b, 0, 6 + p // pairs_per_kv)),
                  pl.BlockSpec(bias.shape, lambda b, p: (0, 0, 0))],
        out_specs=pl.BlockSpec((None, seq, LANES), lambda b, p: (b, 0, p)),
        out_shape=jax.ShapeDtypeStruct((batch, seq, n_pairs * LANES), BF16),
        compiler_params=_params("parallel", "parallel"),
        name="swa",
    )(sink, proj, proj, proj, bias)


def _mla_kernel(q_ref, k_ref, vt_ref, o_ref, s_ref, p_ref, *, seq, tk):
    tq = MLA_TQ
    chunks = [slice(c * tk, (c + 1) * tk) for c in range(seq // tk)]
    n_units = 2 * (q_ref.shape[1] // tq)

    def fold(x):
        return x.reshape(tk // SUBLANES, SUBLANES, tq)

    def scores(u):
        t, h = divmod(u, 2)
        q = q_ref[0, t * tq:(t + 1) * tq, h * LANES:(h + 1) * LANES]
        m_run = jnp.full((SUBLANES, tq), NEG_INF, F32)
        for ck in chunks:
            st = _dot_nt(k_ref[0, ck, h * LANES:(h + 1) * LANES], q)
            s_ref[u % 2, ck, :] = st
            m_run = jnp.maximum(m_run, jnp.max(fold(st), axis=0))
        return jnp.max(m_run, axis=0, keepdims=True)

    def probs(u, m):
        l_run = jnp.zeros((SUBLANES, tq), F32)
        for ck in chunks:
            pt = jnp.exp2(s_ref[u % 2, ck, :] - m)
            l_run = l_run + jnp.sum(fold(pt), axis=0)
            p_ref[u % 2, ck, :] = pt.astype(BF16)
        return jnp.sum(l_run, axis=0, keepdims=True)

    m = scores(0)
    out_a = None
    for u in range(n_units):
        m_next = scores(u + 1) if u + 1 < n_units else None
        l = probs(u, m)
        h = u % 2
        out_t = _dot(vt_ref[h * MLA_V:(h + 1) * MLA_V, :], p_ref[u % 2]) / l
        if h == 0:
            out_a = out_t
        else:
            t = u // 2
            pair = jnp.concatenate([out_a, out_t], axis=0)
            o_ref[0, t * tq:(t + 1) * tq, :] = pair.T.astype(o_ref.dtype)
        m = m_next


def _mla(proj, vt, batch, seq):
    tq = MLA_TQ * MLA_SUB
    assert seq % tq == 0 and seq % MLA_TK == 0 and 2 * MLA_V == LANES
    n_pairs = MLA_HEADS // 2
    return pl.pallas_call(
        functools.partial(_mla_kernel, seq=seq, tk=MLA_TK),
        grid=(batch, n_pairs, seq // tq),
        in_specs=[pl.BlockSpec((1, tq, 2 * LANES), lambda b, p, i: (b, i, 4 + p)),
                  pl.BlockSpec((1, seq, 2 * LANES), lambda b, p, i: (b, 0, 8 + p)),
                  pl.BlockSpec((None, 2 * MLA_V, seq), lambda b, p, i: (b, p, 0))],
        out_specs=pl.BlockSpec((1, tq, LANES), lambda b, p, i: (b, i, p)),
        out_shape=jax.ShapeDtypeStruct((batch, seq, n_pairs * LANES), BF16),
        scratch_shapes=[pltpu.VMEM((2, seq, MLA_TQ), F32), pltpu.VMEM((2, seq, MLA_TQ), BF16)],
        compiler_params=_params("parallel", "parallel", "parallel"),
        name="mla",
    )(proj, proj, vt)


def _rope_tables(seq):
    pos = jnp.arange(seq, dtype=F32)

    def tables(dim):
        inv_freq = ROPE_THETA ** (-jnp.arange(0, dim, 2, dtype=F32) / dim)
        ang = pos[:, None] * inv_freq[None, :]
        return jnp.cos(ang), jnp.sin(ang)

    cos, sin = tables(HEAD_DIM)
    reps = LANES // HEAD_DIM
    cos_t = jnp.tile(jnp.concatenate([cos, cos], axis=1), (1, reps))
    sin_t = jnp.tile(jnp.concatenate([-sin, sin], axis=1), (1, reps))
    cos_r, sin_r = tables(MLA_ROPE)
    pad_l, pad_r = MLA_NOPE, LANES - MLA_NOPE - MLA_ROPE
    cos_rt = jnp.pad(jnp.concatenate([cos_r, cos_r], axis=1), ((0, 0), (pad_l, pad_r)), constant_values=1.0)
    sin_rt = jnp.pad(jnp.concatenate([-sin_r, sin_r], axis=1), ((0, 0), (pad_l, pad_r)))
    return cos_t, sin_t, cos_rt, sin_rt


def kernel(x, ffn1_norm, ffn1_w1, ffn1_w3, ffn1_w2, mix_norm, ffn2_norm, ffn2_w1, ffn2_w3, ffn2_w2, even_w_in, even_w_out, na_rel_bias, odd_w_in, odd_w_out, swa_sink, mla_q_norm, mla_w_qb, mla_kv_norm, mla_w_kvb, final_norm):
    batch, seq, d = x.shape
    depth = ffn1_norm.shape[0]
    assert d == D_MODEL and seq % FFN_TM == 0 and seq % (GRID_W * NA_ROWS) == 0
    cos_t, sin_t, cos_rt, sin_rt = _rope_tables(seq)
    h = x.reshape(batch * seq, d)
    row = lambda v: v.reshape(1, -1).astype(F32)
    fg = row(final_norm)

    ffn1_w = _ffn_weights(ffn1_w1, ffn1_w3, ffn1_w2)
    ffn2_w = _ffn_weights(ffn2_w1, ffn2_w3, ffn2_w2)
    for i in range(depth):
        j = i // 2
        h = _ffn(h, row(ffn1_norm[i]), *ffn1_w, i, fg, False)
        if i % 2 == 0:
            dil_proj, na_proj = _even_proj(h, row(mix_norm[i]), even_w_in[j].astype(BF16), cos_t, sin_t, seq)
            oa = _dilated(dil_proj.reshape(batch, seq, -1), batch, seq).reshape(batch * seq, -1)
            on = _neighbourhood(na_proj.reshape(batch, seq, -1), na_rel_bias[j], batch, seq).reshape(batch * seq, -1)
            mix = (oa, on, even_w_out[j].astype(BF16))
        else:
            w, wqb, wkb, wvbt = _odd_weights(odd_w_in[j], mla_w_qb[j], mla_w_kvb[j])
            proj, vt = _odd_proj(h, row(mix_norm[i]), w, row(mla_q_norm[j]), wqb, row(mla_kv_norm[j]), wkb, wvbt,
                                 cos_t, sin_t, cos_rt, sin_rt, seq)
            proj = proj.reshape(batch, seq, -1)
            oc = _swa(proj, swa_sink[j].astype(F32), batch, seq).reshape(batch * seq, -1)
            od = _mla(proj, vt, batch, seq).reshape(batch * seq, -1)
            mix = (oc, od, odd_w_out[j].astype(BF16))
        h = _ffn(h, row(ffn2_norm[i]), *ffn2_w, i, fg, i == depth - 1, mix)
    return h.reshape(batch, seq, d)
```

```python
import functools

import jax
import jax.numpy as jnp
import numpy as np
from jax import lax
from jax.experimental import pallas as pl
from jax.experimental.pallas import tpu as pltpu

D_MODEL = 1024
D_FF = 2816
HEAD_DIM = 64
ROPE_THETA = 10000.0
NORM_EPS = 1e-6
NEG_INF = -1e30
LOG2_E = 1.4426950408889634

DIL_HEADS = 8
DIL_CONFIGS = ((128, 1), (512, 4), (2048, 16))
NA_HEADS = 8
GRID_W = 64
NA_ROWS = 8
NA_COLS = 16

SWA_Q_HEADS = 8
SWA_KV_HEADS = 2
SWA_HALF = 128
MLA_HEADS = 8
MLA_Q_RANK = 384
MLA_KV_RANK = 256
MLA_NOPE = 64
MLA_ROPE = 32
MLA_V = 64

LANES = 128
SUBLANES = 8
VMEM_LIMIT = 56 * 1024 * 1024

FFN_TM = 512
FFN_TF = 256
FFN_SUB = 2
MLA_TQ = 256
MLA_SUB = 8
MLA_TK = 1024
DIL_TQ = 128
DIL_UNROLL = 8
SWA_TQ = 128
SWA_UNROLL = 4
NA_GROUP = 1
NA_UNROLL = 8

BF16 = jnp.bfloat16
F32 = jnp.float32


def _params(*sem):
    return pltpu.CompilerParams(dimension_semantics=sem, vmem_limit_bytes=VMEM_LIMIT)


def _rms(x, g):
    ms = jnp.mean(x * x, axis=-1, keepdims=True)
    return x * lax.rsqrt(ms + NORM_EPS) * g


def _dot(a, b):
    return jnp.dot(a, b, preferred_element_type=F32)


def _dot_nt(a, b):
    return lax.dot_general(a, b, (((1,), (1,)), ((), ())), preferred_element_type=F32)


def _lane_is_first_head(shape):
    return lax.broadcasted_iota(jnp.int32, shape, len(shape) - 1) < HEAD_DIM


def _rope_block(x, cos, sin_signed, first_half, up, down):
    partner = jnp.where(first_half, pltpu.roll(x, up, 1), pltpu.roll(x, down, 1))
    return x * cos + partner * sin_signed


def _ffn_kernel(*refs, mix, final):
    if mix:
        x_ref, oc_ref, od_ref, wa_ref, wb_ref, g_ref, w1_ref, w3_ref, w2_ref, fg_ref, o_ref, a_ref = refs
    else:
        x_ref, g_ref, w1_ref, w3_ref, w2_ref, fg_ref, o_ref, a_ref = refs
    tm = x_ref.shape[0] // FFN_SUB
    for t in range(FFN_SUB):
        rows = slice(t * tm, (t + 1) * tm)
        x = x_ref[rows, :]
        if mix:
            x = x + _dot(oc_ref[rows, :], wa_ref[...]) + _dot(od_ref[rows, :], wb_ref[...])
        hn = _rms(x, g_ref[...]).astype(BF16)
        for c in range(a_ref.shape[1] // FFN_TF):
            cols = slice(c * FFN_TF, (c + 1) * FFN_TF)
            h1 = _dot(hn, w1_ref[:, cols])
            h3 = _dot(hn, w3_ref[:, cols])
            a_ref[rows, cols] = (h1 * jax.nn.sigmoid(h1) * h3).astype(BF16)
        y = x + 0.5 * _dot(a_ref[rows, :], w2_ref[...])
        if final:
            y = _rms(y, fg_ref[...])
        o_ref[rows, :] = y


def _ffn(x2, g, w1, w3, w2, layer, fg, final, mix=None):
    m, d = x2.shape
    f = w1.shape[2]
    assert f % FFN_TF == 0
    tm = FFN_TM * FFN_SUB
    const = lambda i: (0, 0)
    of_layer = lambda i: (layer, 0, 0)
    resident = dict(pipeline_mode=pl.Buffered(1))
    mix_args, mix_specs = (), []
    if mix is not None:
        oc, od, w_out = mix
        half_w = w_out.shape[0] // 2
        mix_args = (oc, od, w_out[:half_w], w_out[half_w:])
        mix_specs = [pl.BlockSpec((tm, half_w), lambda i: (i, 0))] * 2 + [pl.BlockSpec((half_w, d), const, **resident)] * 2
    return pl.pallas_call(
        functools.partial(_ffn_kernel, mix=mix is not None, final=final),
        grid=(m // tm,),
        in_specs=[pl.BlockSpec((tm, d), lambda i: (i, 0))] + mix_specs + [
            pl.BlockSpec((1, d), const),
            pl.BlockSpec((None, d, f), of_layer, **resident),
            pl.BlockSpec((None, d, f), of_layer, **resident),
            pl.BlockSpec((None, f, d), of_layer, **resident),
            pl.BlockSpec((1, d), const),
        ],
        out_specs=pl.BlockSpec((tm, d), lambda i: (i, 0)),
        out_shape=jax.ShapeDtypeStruct((m, d), F32),
        scratch_shapes=[pltpu.VMEM((tm, f), BF16)],
        compiler_params=_params("parallel"),
        name="ffn",
    )(x2, *mix_args, g, w1, w3, w2, fg)


def _ffn_weights(w1, w3, w2):
    return w1.astype(BF16), w3.astype(BF16), w2.astype(BF16)


_EVEN_DIL_BLOCKS = 12
_EVEN_Q_BLOCKS = tuple(range(0, 4)) + tuple(range(12, 16))
_EVEN_ROPE_BLOCKS = 8


def _even_proj_kernel(x_ref, g_ref, w_ref, cos_ref, sin_ref, dil_ref, na_ref, *, n_blocks):
    hn = _rms(x_ref[...], g_ref[...]).astype(BF16)
    cos = cos_ref[...]
    sin = sin_ref[...]
    lane = lax.broadcasted_iota(jnp.int32, cos.shape, 1)
    first_half = (lane % HEAD_DIM) < HEAD_DIM // 2
    scale = HEAD_DIM ** -0.5 * LOG2_E
    for j2 in range(0, n_blocks, 2):
        wide = _dot(hn, w_ref[:, j2 * LANES:(j2 + 2) * LANES])
        for j in (j2, j2 + 1):
            blk = wide[:, (j - j2) * LANES:(j - j2 + 1) * LANES]
            if j < _EVEN_ROPE_BLOCKS:
                blk = _rope_block(blk, cos, sin, first_half, LANES - HEAD_DIM // 2, HEAD_DIM // 2)
            if j in _EVEN_Q_BLOCKS:
                blk = blk * scale
            if j < _EVEN_DIL_BLOCKS:
                dil_ref[:, j * LANES:(j + 1) * LANES] = blk
            else:
                jn = j - _EVEN_DIL_BLOCKS
                na_ref[:, jn * LANES:(jn + 1) * LANES] = blk.astype(BF16)


def _even_proj(h2, g, w_in, cos, sin, seq):
    m, d = h2.shape
    n_out = w_in.shape[1]
    n_dil = _EVEN_DIL_BLOCKS * LANES
    tm = FFN_TM
    per_seq = seq // tm
    return pl.pallas_call(
        functools.partial(_even_proj_kernel, n_blocks=n_out // LANES),
        grid=(m // tm,),
        in_specs=[
            pl.BlockSpec((tm, d), lambda i: (i, 0)),
            pl.BlockSpec((1, d), lambda i: (0, 0)),
            pl.BlockSpec((d, n_out), lambda i: (0, 0)),
            pl.BlockSpec((tm, LANES), lambda i: (i % per_seq, 0)),
            pl.BlockSpec((tm, LANES), lambda i: (i % per_seq, 0)),
        ],
        out_specs=[pl.BlockSpec((tm, n_dil), lambda i: (i, 0)),
                   pl.BlockSpec((tm, n_out - n_dil), lambda i: (i, 0))],
        out_shape=[jax.ShapeDtypeStruct((m, n_dil), F32),
                   jax.ShapeDtypeStruct((m, n_out - n_dil), BF16)],
        compiler_params=_params("parallel"),
        name="even_proj",
    )(h2, g, w_in, cos, sin)


def _pair_scores(q, k):
    first = _lane_is_first_head(q.shape)
    zero = jnp.zeros_like(q)
    q2 = jnp.concatenate([jnp.where(first, q, zero), jnp.where(first, zero, q)], axis=0)
    return _dot_nt(q2, k)


def _pair_merge(x2, tq):
    first = _lane_is_first_head((tq, LANES))
    return jnp.where(first, x2[:tq], x2[tq:])


def _band_bias(tq, half, win):
    row = np.arange(2 * tq)[:, None] % tq
    col = np.arange(win)[None, :]
    kinds = [np.where(np.abs(col - row - k * half) <= half, 0.0, NEG_INF) for k in range(3)]
    return jnp.asarray(np.stack(kinds), F32)


def _band_window(q0, tq, half, length, win):
    ks = pl.multiple_of(jnp.clip(q0 - half, 0, length - win), half)
    return ks, (q0 - ks) // half


def _dilated_kernel(q_ref, k_ref, v_ref, *refs, seq, configs, tiles):
    n = len(configs)
    bias_refs, o_ref = refs[:n], refs[n]
    qs_ref, ks_ref, vs_ref = refs[n + 1:n + 4]
    ob_refs, lb_refs = refs[n + 4:2 * n + 4], refs[2 * n + 4:3 * n + 4]
    stage_refs = refs[3 * n + 4:]
    copies = {1: (q_ref, k_ref, v_ref)}
    for c, ((half, dil), (tq, win)) in enumerate(zip(configs, tiles)):
        length = seq // dil
        blocks = length // tq
        base_dil = max(d for d in copies if dil % d == 0)
        step = dil // base_dil
        keep = dil > 1 and any(d2 > dil and d2 % dil == 0 for _, d2 in configs)
        if keep:
            copies[dil] = stage_refs
        for a, (src, dst) in enumerate(zip(copies[base_dil], (qs_ref, ks_ref, vs_ref))):
            for r in range(dil):
                start = (r % base_dil) * (seq // base_dil) + r // base_dil
                rows = pl.ds(start, length, stride=step) if step > 1 else pl.ds(start, length)
                x = src[rows, :]
                dst[r * length:(r + 1) * length, :] = x.astype(BF16)
                if keep:
                    stage_refs[a][r * length:(r + 1) * length, :] = x

        def body(i, carry, c=c, half=half, dil=dil, tq=tq, win=win, length=length, blocks=blocks):
            geo = []
            for g in range(DIL_UNROLL):
                n = i * DIL_UNROLL + g
                r = n // blocks
                q0 = pl.multiple_of((n % blocks) * tq, tq)
                ks, kind = _band_window(q0, tq, half, length, win)
                geo.append((r, q0, pl.multiple_of(r * length, tq) + ks, kind))
            ss = [_pair_scores(qs_ref[pl.ds(pl.multiple_of(r * length, tq) + q0, tq), :], ks_ref[pl.ds(kb, win), :])
                  + bias_refs[c][kind] for r, q0, kb, kind in geo]
            ms = [jnp.max(s, axis=-1, keepdims=True) for s in ss]
            ps = [jnp.exp2(s - m) for s, m in zip(ss, ms)]
            ls = [jnp.sum(p, axis=-1, keepdims=True) for p in ps]
            os_ = [_dot(p.astype(BF16), vs_ref[pl.ds(kb, win), :]) for p, (_, _, kb, _) in zip(ps, geo)]
            for (r, q0, _, _), o2, m, l in zip(geo, os_, ms, ls):
                o2 = o2 / l
                lse2 = jnp.broadcast_to(m + jnp.log2(l), o2.shape)
                rows = pl.ds(r + dil * q0, tq, stride=dil) if dil > 1 else pl.ds(q0, tq)
                ob_refs[c][rows, :] = _pair_merge(o2, tq)
                lb_refs[c][rows, :] = _pair_merge(lse2, tq)
            return carry

        lax.fori_loop(0, seq // (tq * DIL_UNROLL), body, 0)

    chunk = 512

    def merge(i, carry):
        rows = pl.ds(pl.multiple_of(i * chunk, chunk), chunk)
        lses = [lb[rows, :] for lb in lb_refs]
        mx = functools.reduce(jnp.maximum, lses)
        es = [jnp.exp2(l - mx) for l in lses]
        num = sum(e * ob[rows, :] for e, ob in zip(es, ob_refs))
        o_ref[rows, :] = (num / sum(es)).astype(o_ref.dtype)
        return carry

    lax.fori_loop(0, seq // chunk, merge, 0)


def _dilated(dil_proj, batch, seq):
    n_pairs = DIL_HEADS // 2
    configs = tuple((window // 2 // dil, dil) for window, dil in DIL_CONFIGS)
    tiles = []
    for half, dil in configs:
        tq = min(DIL_TQ, seq // dil)
        tiles.append((tq, min(tq + 2 * half, seq // dil)))
        assert seq % (dil * tq) == 0 and tq % half == 0
    biases = [_band_bias(tq, half, win) for (half, _), (tq, win) in zip(configs, tiles)]

    def spec(base):
        return pl.BlockSpec((None, seq, LANES), lambda b, p: (b, 0, base + p))

    return pl.pallas_call(
        functools.partial(_dilated_kernel, seq=seq, configs=configs, tiles=tuple(tiles)),
        grid=(batch, n_pairs),
        in_specs=[spec(0), spec(4), spec(8)] + [pl.BlockSpec(b.shape, lambda b_, p: (0, 0, 0)) for b in biases],
        out_specs=pl.BlockSpec((None, seq, LANES), lambda b, p: (b, 0, p)),
        out_shape=jax.ShapeDtypeStruct((batch, seq, n_pairs * LANES), BF16),
        scratch_shapes=[pltpu.VMEM((seq, LANES), BF16)] * 3 + [pltpu.VMEM((seq, LANES), F32)] * (2 * len(configs) + 3),
        compiler_params=_params("parallel", "parallel"),
        name="dilated",
    )(dil_proj, dil_proj, dil_proj, *biases)


def _na_group_geometry(rows, kr):
    n_groups = rows // NA_GROUP
    wr = -(-(kr + NA_GROUP - 1) * GRID_W // LANES) * LANES // GRID_W
    g = np.arange(n_groups)
    start = np.minimum(np.clip(g * NA_GROUP - kr // 2, 0, rows - kr), rows - wr)
    return n_groups, wr, start


def _na_kernel(start_ref, kind_ref, q_ref, k_ref, v_ref, bias_ref, o_ref, *, rows, kr):
    n_groups, wr, _ = _na_group_geometry(rows, kr)
    tq = NA_GROUP * GRID_W

    def body(i, carry):
        geo = []
        for u in range(NA_UNROLL):
            g = i * NA_UNROLL + u
            geo.append((pl.multiple_of(g * tq, tq), pl.multiple_of(start_ref[g] * GRID_W, GRID_W), kind_ref[g]))
        ss = [_pair_scores(q_ref[pl.ds(q0, tq), :], k_ref[pl.ds(k0, wr * GRID_W), :]) + bias_ref[kind]
              for q0, k0, kind in geo]
        ms = [jnp.max(s, axis=-1, keepdims=True) for s in ss]
        ps = [jnp.exp2(s - m) for s, m in zip(ss, ms)]
        ls = [jnp.sum(p, axis=-1, keepdims=True) for p in ps]
        os_ = [_dot(p.astype(BF16), v_ref[pl.ds(k0, wr * GRID_W), :]) for p, (_, k0, _) in zip(ps, geo)]
        for (q0, _, _), o2, l in zip(geo, os_, ls):
            o_ref[pl.ds(q0, tq), :] = _pair_merge(o2 / l, tq).astype(o_ref.dtype)
        return carry

    lax.fori_loop(0, n_groups // NA_UNROLL, body, 0)


def _na_bias_table(rpb, rows, kr):
    n_groups, wr, start = _na_group_geometry(rows, kr)
    qcol = np.arange(GRID_W)
    kcol = np.arange(GRID_W)
    wstart = np.clip(qcol - NA_COLS // 2, 0, GRID_W - NA_COLS)
    col_ok = (kcol[None, :] >= wstart[:, None]) & (kcol[None, :] < wstart[:, None] + NA_COLS)
    dc = np.clip(kcol[None, :] - qcol[:, None] + NA_COLS - 1, 0, 2 * NA_COLS - 2)

    def geometry(g):
        qrow = g * NA_GROUP + np.arange(NA_GROUP)
        krow = start[g] + np.arange(wr)
        rs = np.clip(qrow - kr // 2, 0, rows - kr)
        row_ok = (krow[None, :] >= rs[:, None]) & (krow[None, :] < rs[:, None] + kr)
        dr = np.where(row_ok, krow[None, :] - qrow[:, None] + NA_ROWS - 1, -1)
        return dr

    patterns, kind_of = [], []
    for g in range(n_groups):
        dr = geometry(g)
        match = [k for k, p in enumerate(patterns) if np.array_equal(p, dr)]
        if not match:
            patterns.append(dr)
        kind_of.append(match[0] if match else len(patterns) - 1)
    onehot = (dc[None] == np.arange(2 * NA_COLS - 1)[:, None, None]) & col_ok[None]
    planes = jnp.einsum("hrd,dqk->hrqk", rpb * LOG2_E, jnp.asarray(onehot, F32), precision=lax.Precision.HIGHEST)
    planes = planes + jnp.asarray(np.where(col_ok, 0.0, NEG_INF), F32)
    masked = jnp.full(planes.shape[:1] + planes.shape[2:], NEG_INF, F32)
    slabs = []
    for dr in patterns:
        rows_i = []
        for i in range(NA_GROUP):
            cols_j = [planes[:, int(dr[i, j])] if dr[i, j] >= 0 else masked for j in range(wr)]
            rows_i.append(jnp.concatenate(cols_j, axis=-1))
        slab = jnp.stack(rows_i, axis=1)
        slabs.append(slab.reshape(NA_HEADS // 2, 2 * NA_GROUP * GRID_W, wr * GRID_W))
    table = jnp.stack(slabs)
    return table, jnp.asarray(start, jnp.int32), jnp.asarray(kind_of, jnp.int32)


def _neighbourhood(proj, rpb, batch, seq):
    rows = seq // GRID_W
    kr = min(NA_ROWS, rows)
    assert rows % (NA_GROUP * NA_UNROLL) == 0 and rows >= kr + NA_GROUP
    table, start, kind_of = _na_bias_table(rpb.astype(F32), rows, kr)
    n_pairs = NA_HEADS // 2

    def spec(base):
        return pl.BlockSpec((None, seq, LANES), lambda b, p: (b, 0, base + p))

    smem = pl.BlockSpec(memory_space=pltpu.SMEM)
    return pl.pallas_call(
        functools.partial(_na_kernel, rows=rows, kr=kr),
        grid=(batch, n_pairs),
        in_specs=[smem, smem, spec(0), spec(4), spec(8),
                  pl.BlockSpec((table.shape[0], None) + table.shape[2:], lambda b, p: (0, p, 0, 0))],
        out_specs=pl.BlockSpec((None, seq, LANES), lambda b, p: (b, 0, p)),
        out_shape=jax.ShapeDtypeStruct((batch, seq, n_pairs * LANES), BF16),
        compiler_params=_params("parallel", "parallel"),
        name="neighbourhood",
    )(start, kind_of, proj, proj, proj, table)


_ODD_W_COLS = 14 * LANES
_ODD_OUT_COLS = 24 * LANES


def _odd_proj_kernel(x_ref, g_ref, w_ref, qn_ref, wqb_ref, kvn_ref, wkb_ref, wvbt_ref,
                     cos_ref, sin_ref, cosr_ref, sinr_ref, o_ref, vt_ref):
    hn = _rms(x_ref[...], g_ref[...]).astype(BF16)
    cos, sin = cos_ref[...], sin_ref[...]
    cosr, sinr = cosr_ref[...], sinr_ref[...]
    lane = lax.broadcasted_iota(jnp.int32, cos.shape, 1)
    first_half = (lane % HEAD_DIM) < HEAD_DIM // 2
    first_half_r = lane < MLA_NOPE + MLA_ROPE // 2
    swa_scale = HEAD_DIM ** -0.5 * LOG2_E
    mla_scale = (MLA_NOPE + MLA_ROPE) ** -0.5 * LOG2_E

    def col(j, n=1):
        return slice(j * LANES, (j + n) * LANES)

    def rope(blk):
        return _rope_block(blk, cos, sin, first_half, LANES - HEAD_DIM // 2, HEAD_DIM // 2)

    def rope_r(blk):
        return _rope_block(blk, cosr, sinr, first_half_r, LANES - MLA_ROPE // 2, MLA_ROPE // 2)

    def halves(wide):
        return wide[:, :LANES], wide[:, LANES:]

    latent = _dot(hn, w_ref[:, col(8, 6)])
    for j in range(0, 6, 2):
        for jj, blk in zip((j, j + 1), halves(_dot(hn, w_ref[:, col(j, 2)]))):
            blk = rope(blk) * swa_scale if jj < 4 else rope(blk)
            o_ref[:, col(jj)] = blk.astype(BF16)

    q_an = _rms(latent[:, :3 * LANES], qn_ref[...]).astype(BF16)
    kv_an = _rms(latent[:, 3 * LANES:5 * LANES], kvn_ref[...]).astype(BF16)
    k_pe = rope_r(latent[:, 5 * LANES:])
    for h in range(0, MLA_HEADS, 2):
        for hh, qh in zip((h, h + 1), halves(_dot(q_an, wqb_ref[:, col(h, 2)]))):
            o_ref[:, col(8 + hh)] = (rope_r(qh) * mla_scale).astype(BF16)
    for h in range(0, MLA_HEADS, 2):
        for hh, kh in zip((h, h + 1), halves(_dot(kv_an, wkb_ref[:, col(h, 2)]))):
            o_ref[:, col(16 + hh)] = (kh + k_pe).astype(BF16)
    o_ref[:, col(6, 2)] = _dot(hn, w_ref[:, col(6, 2)]).astype(BF16)
    vt_ref[...] = _dot_nt(wvbt_ref[...], kv_an).astype(BF16)


def _odd_proj(h2, g, w, qn, wqb, kvn, wkb, wvbt, cos, sin, cosr, sinr, seq):
    m, d = h2.shape
    tm = FFN_TM
    per_seq = seq // tm
    n_v = wvbt.shape[0]
    full = lambda a: pl.BlockSpec(a.shape, lambda i: (0, 0))
    tab = pl.BlockSpec((tm, LANES), lambda i: (i % per_seq, 0))
    return pl.pallas_call(
        _odd_proj_kernel,
        grid=(m // tm,),
        in_specs=[pl.BlockSpec((tm, d), lambda i: (i, 0)), full(g), full(w), full(qn), full(wqb),
                  full(kvn), full(wkb), full(wvbt), tab, tab, tab, tab],
        out_specs=[pl.BlockSpec((tm, _ODD_OUT_COLS), lambda i: (i, 0)),
                   pl.BlockSpec((None, n_v, tm), lambda i: (i // per_seq, 0, i % per_seq))],
        out_shape=[jax.ShapeDtypeStruct((m, _ODD_OUT_COLS), BF16),
                   jax.ShapeDtypeStruct((m // seq, n_v, seq), BF16)],
        compiler_params=_params("parallel"),
        name="odd_proj",
    )(h2, g, w, qn, wqb, kvn, wkb, wvbt, cos, sin, cosr, sinr)


def _odd_weights(w_in, w_qb, w_kvb):
    d = w_in.shape[0]
    c = np.cumsum([0, SWA_Q_HEADS * HEAD_DIM, SWA_KV_HEADS * HEAD_DIM, SWA_KV_HEADS * HEAD_DIM,
                   MLA_Q_RANK, MLA_KV_RANK, MLA_ROPE])

    def dup(w):
        w = w.reshape(d, SWA_KV_HEADS, 1, HEAD_DIM)
        return jnp.broadcast_to(w, (d, SWA_KV_HEADS, 2, HEAD_DIM)).reshape(d, SWA_KV_HEADS * LANES)

    w_kpe = jnp.pad(w_in[:, c[5]:c[6]], ((0, 0), (MLA_NOPE, LANES - MLA_NOPE - MLA_ROPE)))
    w = jnp.concatenate([w_in[:, c[0]:c[1]], dup(w_in[:, c[1]:c[2]]), dup(w_in[:, c[2]:c[3]]),
                         w_in[:, c[3]:c[4]], w_in[:, c[4]:c[5]], w_kpe], axis=1).astype(BF16)
    qk = MLA_NOPE + MLA_ROPE
    wqb = jnp.pad(w_qb.reshape(MLA_Q_RANK, MLA_HEADS, qk), ((0, 0), (0, 0), (0, LANES - qk)))
    wqb = wqb.reshape(MLA_Q_RANK, MLA_HEADS * LANES).astype(BF16)
    kv = w_kvb.reshape(MLA_KV_RANK, MLA_HEADS, MLA_NOPE + MLA_V)
    wkb = jnp.pad(kv[:, :, :MLA_NOPE], ((0, 0), (0, 0), (0, LANES - MLA_NOPE)))
    wkb = wkb.reshape(MLA_KV_RANK, MLA_HEADS * LANES).astype(BF16)
    wvbt = kv[:, :, MLA_NOPE:].reshape(MLA_KV_RANK, MLA_HEADS * MLA_V).T.astype(BF16)
    return w, wqb, wkb, wvbt


def _swa_kernel(sink_ref, q_ref, k_ref, v_ref, bias_ref, o_ref, *, seq, half, tq, win):
    pair = pl.program_id(1)
    first = lax.broadcasted_iota(jnp.int32, (2 * tq, 1), 0) < tq
    sink = jnp.where(first, sink_ref[2 * pair], sink_ref[2 * pair + 1]) * LOG2_E

    def body(i, carry):
        geo = []
        for g in range(SWA_UNROLL):
            q0 = pl.multiple_of((i * SWA_UNROLL + g) * tq, tq)
            geo.append((q0,) + _band_window(q0, tq, half, seq, win))
        ss = [_pair_scores(q_ref[pl.ds(q0, tq), :], k_ref[pl.ds(ks, win), :]) + bias_ref[kind]
              for q0, ks, kind in geo]
        ms = [jnp.maximum(jnp.max(s, axis=-1, keepdims=True), sink) for s in ss]
        ps = [jnp.exp2(s - m) for s, m in zip(ss, ms)]
        ls = [jnp.sum(p, axis=-1, keepdims=True) + jnp.exp2(sink - m) for p, m in zip(ps, ms)]
        os_ = [_dot(p.astype(BF16), v_ref[pl.ds(ks, win), :]) for p, (_, ks, _) in zip(ps, geo)]
        for (q0, _, _), o2, l in zip(geo, os_, ls):
            o_ref[pl.ds(q0, tq), :] = _pair_merge(o2 / l, tq).astype(o_ref.dtype)
        return carry

    lax.fori_loop(0, seq // (tq * SWA_UNROLL), body, 0)


def _swa(proj, sink, batch, seq):
    tq = SWA_TQ
    win = tq + 2 * SWA_HALF
    assert seq % (tq * SWA_UNROLL) == 0 and tq % SWA_HALF == 0 and seq >= win
    n_pairs = SWA_Q_HEADS // 2
    pairs_per_kv = n_pairs // SWA_KV_HEADS
    bias = _band_bias(tq, SWA_HALF, win)
    return pl.pallas_call(
        functools.partial(_swa_kernel, seq=seq, half=SWA_HALF, tq=tq, win=win),
        grid=(batch, n_pairs),
        in_specs=[pl.BlockSpec(memory_space=pltpu.SMEM),
                  pl.BlockSpec((None, seq, LANES), lambda b, p: (b, 0, p)),
                  pl.BlockSpec((None, seq, LANES), lambda b, p: (b, 0, 4 + p // pairs_per_kv)),
                  pl.BlockSpec((None, seq, LANES), lambda b, p: (b, 0, 6 + p // pairs_per_kv)),
                  pl.BlockSpec(bias.shape, lambda b, p: (0, 0, 0))],
        out_specs=pl.BlockSpec((None, seq, LANES), lambda b, p: (b, 0, p)),
        out_shape=jax.ShapeDtypeStruct((batch, seq, n_pairs * LANES), BF16),
        compiler_params=_params("parallel", "parallel"),
        name="swa",
    )(sink, proj, proj, proj, bias)


def _mla_kernel(q_ref, k_ref, vt_ref, o_ref, s_ref, p_ref, *, seq, tk):
    tq = MLA_TQ
    chunks = [slice(c * tk, (c + 1) * tk) for c in range(seq // tk)]
    n_units = 2 * (q_ref.shape[1] // tq)

    def fold(x):
        return x.reshape(tk // SUBLANES, SUBLANES, tq)

    def scores(u):
        t, h = divmod(u, 2)
        q = q_ref[0, t * tq:(t + 1) * tq, h * LANES:(h + 1) * LANES]
        m_run = jnp.full((SUBLANES, tq), NEG_INF, F32)
        for ck in chunks:
            st = _dot_nt(k_ref[0, ck, h * LANES:(h + 1) * LANES], q)
            s_ref[u % 2, ck, :] = st
            m_run = jnp.maximum(m_run, jnp.max(fold(st), axis=0))
        return jnp.max(m_run, axis=0, keepdims=True)

    def probs(u, m):
        l_run = jnp.zeros((SUBLANES, tq), F32)
        for ck in chunks:
            pt = jnp.exp2(s_ref[u % 2, ck, :] - m)
            l_run = l_run + jnp.sum(fold(pt), axis=0)
            p_ref[u % 2, ck, :] = pt.astype(BF16)
        return jnp.sum(l_run, axis=0, keepdims=True)

    m = scores(0)
    out_a = None
    for u in range(n_units):
        m_next = scores(u + 1) if u + 1 < n_units else None
        l = probs(u, m)
        h = u % 2
        out_t = _dot(vt_ref[h * MLA_V:(h + 1) * MLA_V, :], p_ref[u % 2]) / l
        if h == 0:
            out_a = out_t
        else:
            t = u // 2
            pair = jnp.concatenate([out_a, out_t], axis=0)
            o_ref[0, t * tq:(t + 1) * tq, :] = pair.T.astype(o_ref.dtype)
        m = m_next


def _mla(proj, vt, batch, seq):
    tq = MLA_TQ * MLA_SUB
    assert seq % tq == 0 and seq % MLA_TK == 0 and 2 * MLA_V == LANES
    n_pairs = MLA_HEADS // 2
    return pl.pallas_call(
        functools.partial(_mla_kernel, seq=seq, tk=MLA_TK),
        grid=(batch, n_pairs, seq // tq),
        in_specs=[pl.BlockSpec((1, tq, 2 * LANES), lambda b, p, i: (b, i, 4 + p)),
                  pl.BlockSpec((1, seq, 2 * LANES), lambda b, p, i: (b, 0, 8 + p)),
                  pl.BlockSpec((None, 2 * MLA_V, seq), lambda b, p, i: (b, p, 0))],
        out_specs=pl.BlockSpec((1, tq, LANES), lambda b, p, i: (b, i, p)),
        out_shape=jax.ShapeDtypeStruct((batch, seq, n_pairs * LANES), BF16),
        scratch_shapes=[pltpu.VMEM((2, seq, MLA_TQ), F32), pltpu.VMEM((2, seq, MLA_TQ), BF16)],
        compiler_params=_params("parallel", "parallel", "parallel"),
        name="mla",
    )(proj, proj, vt)


def _rope_tables(seq):
    pos = jnp.arange(seq, dtype=F32)

    def tables(dim):
        inv_freq = ROPE_THETA ** (-jnp.arange(0, dim, 2, dtype=F32) / dim)
        ang = pos[:, None] * inv_freq[None, :]
        return jnp.cos(ang), jnp.sin(ang)

    cos, sin = tables(HEAD_DIM)
    reps = LANES // HEAD_DIM
    cos_t = jnp.tile(jnp.concatenate([cos, cos], axis=1), (1, reps))
    sin_t = jnp.tile(jnp.concatenate([-sin, sin], axis=1), (1, reps))
    cos_r, sin_r = tables(MLA_ROPE)
    pad_l, pad_r = MLA_NOPE, LANES - MLA_NOPE - MLA_ROPE
    cos_rt = jnp.pad(jnp.concatenate([cos_r, cos_r], axis=1), ((0, 0), (pad_l, pad_r)), constant_values=1.0)
    sin_rt = jnp.pad(jnp.concatenate([-sin_r, sin_r], axis=1), ((0, 0), (pad_l, pad_r)))
    return cos_t, sin_t, cos_rt, sin_rt


def kernel(x, ffn1_norm, ffn1_w1, ffn1_w3, ffn1_w2, mix_norm, ffn2_norm, ffn2_w1, ffn2_w3, ffn2_w2, even_w_in, even_w_out, na_rel_bias, odd_w_in, odd_w_out, swa_sink, mla_q_norm, mla_w_qb, mla_kv_norm, mla_w_kvb, final_norm):
    batch, seq, d = x.shape
    depth = ffn1_norm.shape[0]
    assert d == D_MODEL and seq % FFN_TM == 0 and seq % (GRID_W * NA_ROWS) == 0
    cos_t, sin_t, cos_rt, sin_rt = _rope_tables(seq)
    h = x.reshape(batch * seq, d)
    row = lambda v: v.reshape(1, -1).astype(F32)
    fg = row(final_norm)

    ffn1_w = _ffn_weights(ffn1_w1, ffn1_w3, ffn1_w2)
    ffn2_w = _ffn_weights(ffn2_w1, ffn2_w3, ffn2_w2)
    for i in range(depth):
        j = i // 2
        h = _ffn(h, row(ffn1_norm[i]), *ffn1_w, i, fg, False)
        if i % 2 == 0:
            dil_proj, na_proj = _even_proj(h, row(mix_norm[i]), even_w_in[j].astype(BF16), cos_t, sin_t, seq)
            oa = _dilated(dil_proj.reshape(batch, seq, -1), batch, seq).reshape(batch * seq, -1)
            on = _neighbourhood(na_proj.reshape(batch, seq, -1), na_rel_bias[j], batch, seq).reshape(batch * seq, -1)
            mix = (oa, on, even_w_out[j].astype(BF16))
        else:
            w, wqb, wkb, wvbt = _odd_weights(odd_w_in[j], mla_w_qb[j], mla_w_kvb[j])
            proj, vt = _odd_proj(h, row(mix_norm[i]), w, row(mla_q_norm[j]), wqb, row(mla_kv_norm[j]), wkb, wvbt,
                                 cos_t, sin_t, cos_rt, sin_rt, seq)
            proj = proj.reshape(batch, seq, -1)
            oc = _swa(proj, swa_sink[j].astype(F32), batch, seq).reshape(batch * seq, -1)
            od = _mla(proj, vt, batch, seq).reshape(batch * seq, -1)
            mix = (oc, od, odd_w_out[j].astype(BF16))
        h = _ffn(h, row(ffn2_norm[i]), *ffn2_w, i, fg, i == depth - 1, mix)
    return h.reshape(batch, seq, d)
```

```python
import functools

import jax
import jax.numpy as jnp
import numpy as np
from jax import lax
from jax.experimental import pallas as pl
from jax.experimental.pallas import tpu as pltpu

D_MODEL = 1024
HEAD_DIM = 64
ROPE_THETA = 10000.0
NORM_EPS = 1e-6
NEG_INF = -1e30
LOG2_E = 1.4426950408889634

DIL_HEADS = 8
DIL_CONFIGS = ((128, 1), (512, 4), (2048, 16))
NA_HEADS = 8
GRID_W = 64
NA_ROWS = 8
NA_COLS = 16

SWA_Q_HEADS = 8
SWA_KV_HEADS = 2
SWA_HALF = 128
MLA_HEADS = 8
MLA_Q_RANK = 384
MLA_KV_RANK = 256
MLA_NOPE = 64
MLA_ROPE = 32
MLA_V = 64

LANES = 128
SUBLANES = 8
VMEM_LIMIT = 56 * 1024 * 1024

FFN_TM = 512
FFN_TF = 256
FFN_SUB = 2
PROJ_SUB = 2
MLA_TQ = 256
MLA_SUB = 8
MLA_TK = 1024
DIL_TQ = 128
DIL_UNROLL = 16
DIL_SKEW = 2
DIL_MERGE_ROWS = 512
SWA_TQ = 128
SWA_UNROLL = 16
SWA_SKEW = 4
NA_GROUP = 1
NA_UNROLL = 32
NA_SKEW = 3

BF16 = jnp.bfloat16
F32 = jnp.float32


def _params(*sem):
    return pltpu.CompilerParams(dimension_semantics=sem, vmem_limit_bytes=VMEM_LIMIT)


def _rms(x, g):
    ms = jnp.mean(x * x, axis=-1, keepdims=True)
    return x * lax.rsqrt(ms + NORM_EPS) * g


def _dot(a, b):
    return jnp.dot(a, b, preferred_element_type=F32)


def _dot_nt(a, b):
    return lax.dot_general(a, b, (((1,), (1,)), ((), ())), preferred_element_type=F32)


def _lane_is_first_head(shape):
    return lax.broadcasted_iota(jnp.int32, shape, len(shape) - 1) < HEAD_DIM


def _rope_block(x, cos, sin_signed, first_half, up, down):
    partner = jnp.where(first_half, pltpu.roll(x, up, 1), pltpu.roll(x, down, 1))
    return x * cos + partner * sin_signed


def _ffn_kernel(*refs, mix, final):
    if mix:
        x_ref, oc_ref, od_ref, wa_ref, wb_ref, g_ref, w1_ref, w3_ref, w2_ref, fg_ref, o_ref, a_ref = refs
    else:
        x_ref, g_ref, w1_ref, w3_ref, w2_ref, fg_ref, o_ref, a_ref = refs
    tm = x_ref.shape[0] // FFN_SUB
    for t in range(FFN_SUB):
        rows = slice(t * tm, (t + 1) * tm)
        x = x_ref[rows, :]
        if mix:
            x = x + _dot(oc_ref[rows, :], wa_ref[...]) + _dot(od_ref[rows, :], wb_ref[...])
        hn = _rms(x, g_ref[...]).astype(BF16)
        for c in range(a_ref.shape[1] // FFN_TF):
            cols = slice(c * FFN_TF, (c + 1) * FFN_TF)
            h1 = _dot(hn, w1_ref[:, cols])
            h3 = _dot(hn, w3_ref[:, cols])
            a_ref[rows, cols] = (h1 * jax.nn.sigmoid(h1) * h3).astype(BF16)
        y = x + 0.5 * _dot(a_ref[rows, :], w2_ref[...])
        if final:
            y = _rms(y, fg_ref[...])
        o_ref[rows, :] = y


def _ffn(x2, g, w1, w3, w2, layer, fg, final, mix=None):
    m, d = x2.shape
    f = w1.shape[2]
    assert f % FFN_TF == 0
    tm = FFN_TM * FFN_SUB
    const = lambda i: (0, 0)
    of_layer = lambda i: (layer, 0, 0)
    resident = dict(pipeline_mode=pl.Buffered(1))
    mix_args, mix_specs = (), []
    if mix is not None:
        oc, od, w_out = mix
        half_w = w_out.shape[0] // 2
        mix_args = (oc, od, w_out[:half_w], w_out[half_w:])
        mix_specs = [pl.BlockSpec((tm, half_w), lambda i: (i, 0))] * 2 + [pl.BlockSpec((half_w, d), const, **resident)] * 2
    return pl.pallas_call(
        functools.partial(_ffn_kernel, mix=mix is not None, final=final),
        grid=(m // tm,),
        in_specs=[pl.BlockSpec((tm, d), lambda i: (i, 0))] + mix_specs + [
            pl.BlockSpec((1, d), const),
            pl.BlockSpec((None, d, f), of_layer, **resident),
            pl.BlockSpec((None, d, f), of_layer, **resident),
            pl.BlockSpec((None, f, d), of_layer, **resident),
            pl.BlockSpec((1, d), const),
        ],
        out_specs=pl.BlockSpec((tm, d), lambda i: (i, 0)),
        out_shape=jax.ShapeDtypeStruct((m, d), F32),
        scratch_shapes=[pltpu.VMEM((tm, f), BF16)],
        compiler_params=_params("parallel"),
        name="ffn",
    )(x2, *mix_args, g, w1, w3, w2, fg)


def _ffn_weights(w1, w3, w2):
    return w1.astype(BF16), w3.astype(BF16), w2.astype(BF16)


_EVEN_DIL_BLOCKS = 12
_EVEN_Q_BLOCKS = tuple(range(0, 4)) + tuple(range(12, 16))
_EVEN_ROPE_BLOCKS = 8


def _even_proj_kernel(x_ref, g_ref, w_ref, cos_ref, sin_ref, dil_ref, na_ref, *, n_blocks):
    tm = x_ref.shape[0] // PROJ_SUB
    for t in range(PROJ_SUB):
        rows = slice(t * tm, (t + 1) * tm)
        _even_proj_rows(x_ref.at[rows], g_ref, w_ref, cos_ref.at[rows], sin_ref.at[rows],
                        dil_ref.at[rows], na_ref.at[rows], n_blocks=n_blocks)


def _even_proj_rows(x_ref, g_ref, w_ref, cos_ref, sin_ref, dil_ref, na_ref, *, n_blocks):
    hn = _rms(x_ref[...], g_ref[...]).astype(BF16)
    cos = cos_ref[...]
    sin = sin_ref[...]
    lane = lax.broadcasted_iota(jnp.int32, cos.shape, 1)
    first_half = (lane % HEAD_DIM) < HEAD_DIM // 2
    scale = HEAD_DIM ** -0.5 * LOG2_E
    for j2 in range(0, n_blocks, 2):
        wide = _dot(hn, w_ref[:, j2 * LANES:(j2 + 2) * LANES])
        for j in (j2, j2 + 1):
            blk = wide[:, (j - j2) * LANES:(j - j2 + 1) * LANES]
            if j < _EVEN_ROPE_BLOCKS:
                blk = _rope_block(blk, cos, sin, first_half, LANES - HEAD_DIM // 2, HEAD_DIM // 2)
            if j in _EVEN_Q_BLOCKS:
                blk = blk * scale
            if j < _EVEN_DIL_BLOCKS:
                dil_ref[:, j * LANES:(j + 1) * LANES] = blk
            else:
                jn = j - _EVEN_DIL_BLOCKS
                na_ref[:, jn * LANES:(jn + 1) * LANES] = blk.astype(BF16)


def _even_proj(h2, g, w_in, cos, sin, seq):
    m, d = h2.shape
    n_out = w_in.shape[1]
    n_dil = _EVEN_DIL_BLOCKS * LANES
    tm = FFN_TM * PROJ_SUB
    per_seq = seq // tm
    return pl.pallas_call(
        functools.partial(_even_proj_kernel, n_blocks=n_out // LANES),
        grid=(m // tm,),
        in_specs=[
            pl.BlockSpec((tm, d), lambda i: (i, 0)),
            pl.BlockSpec((1, d), lambda i: (0, 0)),
            pl.BlockSpec((d, n_out), lambda i: (0, 0)),
            pl.BlockSpec((tm, LANES), lambda i: (i % per_seq, 0)),
            pl.BlockSpec((tm, LANES), lambda i: (i % per_seq, 0)),
        ],
        out_specs=[pl.BlockSpec((tm, n_dil), lambda i: (i, 0)),
                   pl.BlockSpec((tm, n_out - n_dil), lambda i: (i, 0))],
        out_shape=[jax.ShapeDtypeStruct((m, n_dil), F32),
                   jax.ShapeDtypeStruct((m, n_out - n_dil), BF16)],
        compiler_params=_params("parallel"),
        name="even_proj",
    )(h2, g, w_in, cos, sin)


def _pair_scores(q, k):
    first = _lane_is_first_head(q.shape)
    zero = jnp.zeros_like(q)
    q2 = jnp.concatenate([jnp.where(first, q, zero), jnp.where(first, zero, q)], axis=0)
    return _dot_nt(q2, k)


def _pair_merge(x2, tq):
    first = _lane_is_first_head((tq, LANES))
    return jnp.where(first, x2[:tq], x2[tq:])


def _band_bias(tq, half, win):
    row = np.arange(2 * tq)[:, None] % tq
    col = np.arange(win)[None, :]
    kinds = [np.where(np.abs(col - row - k * half) <= half, 0.0, NEG_INF) for k in range(3)]
    return jnp.asarray(np.stack(kinds), F32)


def _band_window(q0, tq, half, length, win):
    ks = pl.multiple_of(jnp.clip(q0 - half, 0, length - win), half)
    return ks, (q0 - ks) // half


def _dilated_kernel(q_ref, k_ref, v_ref, *refs, seq, configs, tiles):
    n = len(configs)
    bias_refs, o_ref = refs[:n], refs[n]
    qs_ref, ks_ref, vs_ref = refs[n + 1:n + 4]
    ob_refs, mb_refs, lb_refs = refs[n + 4:2 * n + 4], refs[2 * n + 4:3 * n + 4], refs[3 * n + 4:4 * n + 4]
    stage_refs = refs[4 * n + 4:]
    copies = {1: (q_ref, k_ref, v_ref)}
    for c, ((half, dil), (tq, win)) in enumerate(zip(configs, tiles)):
        length = seq // dil
        blocks = length // tq
        base_dil = max(d for d in copies if dil % d == 0)
        step = dil // base_dil
        keep = dil > 1 and any(d2 > dil and d2 % dil == 0 for _, d2 in configs)
        if keep:
            copies[dil] = stage_refs
        for a, (src, dst) in enumerate(zip(copies[base_dil], (qs_ref, ks_ref, vs_ref))):
            for r in range(dil):
                start = (r % base_dil) * (seq // base_dil) + r // base_dil
                rows = pl.ds(start, length, stride=step) if step > 1 else pl.ds(start, length)
                x = src[rows, :]
                dst[r * length:(r + 1) * length, :] = x.astype(BF16)
                if keep:
                    stage_refs[a][r * length:(r + 1) * length, :] = x

        def body(i, carry, c=c, half=half, dil=dil, tq=tq, win=win, length=length, blocks=blocks):
            geo = []
            for g in range(DIL_UNROLL):
                n = i * DIL_UNROLL + g
                r = n // blocks
                q0 = pl.multiple_of((n % blocks) * tq, tq)
                ks, kind = _band_window(q0, tq, half, length, win)
                geo.append((r, q0, pl.multiple_of(r * length, tq) + ks, kind))
            def score(u):
                r, q0, kb, kind = geo[u]
                q = qs_ref[pl.ds(pl.multiple_of(r * length, tq) + q0, tq), :]
                return _pair_scores(q, ks_ref[pl.ds(kb, win), :]) + bias_refs[c][kind]

            ss = {u: score(u) for u in range(min(DIL_SKEW, DIL_UNROLL))}
            for u in range(DIL_UNROLL):
                if u + DIL_SKEW < DIL_UNROLL:
                    ss[u + DIL_SKEW] = score(u + DIL_SKEW)
                s = ss.pop(u)
                r, q0, kb, _ = geo[u]
                m = jnp.max(s, axis=-1, keepdims=True)
                p = jnp.exp2(s - m)
                l = jnp.sum(p, axis=-1, keepdims=True)
                o2 = _dot(p.astype(BF16), vs_ref[pl.ds(kb, win), :])
                rows = pl.ds(r + dil * q0, tq, stride=dil) if dil > 1 else pl.ds(q0, tq)
                ob_refs[c][rows, :] = _pair_merge(o2, tq)
                mb_refs[c][rows, :] = _pair_merge(jnp.broadcast_to(m, o2.shape), tq)
                lb_refs[c][rows, :] = _pair_merge(jnp.broadcast_to(l, o2.shape), tq)
            return carry

        lax.fori_loop(0, seq // (tq * DIL_UNROLL), body, 0)

    chunk = DIL_MERGE_ROWS

    def merge(i, carry):
        rows = pl.ds(pl.multiple_of(i * chunk, chunk), chunk)
        ms = [mb[rows, :] for mb in mb_refs]
        mx = functools.reduce(jnp.maximum, ms)
        ws = [jnp.exp2(m - mx) for m in ms]
        num = sum(w * ob[rows, :] for w, ob in zip(ws, ob_refs))
        den = sum(w * lb[rows, :] for w, lb in zip(ws, lb_refs))
        o_ref[rows, :] = (num / den).astype(o_ref.dtype)
        return carry

    lax.fori_loop(0, seq // chunk, merge, 0)


def _dilated(dil_proj, batch, seq):
    n_pairs = DIL_HEADS // 2
    configs = tuple((window // 2 // dil, dil) for window, dil in DIL_CONFIGS)
    tiles = []
    for half, dil in configs:
        tq = min(DIL_TQ, seq // dil)
        tiles.append((tq, min(tq + 2 * half, seq // dil)))
        assert seq % (dil * tq) == 0 and tq % half == 0
    biases = [_band_bias(tq, half, win) for (half, _), (tq, win) in zip(configs, tiles)]

    def spec(base):
        return pl.BlockSpec((None, seq, LANES), lambda b, p: (b, 0, base + p))

    return pl.pallas_call(
        functools.partial(_dilated_kernel, seq=seq, configs=configs, tiles=tuple(tiles)),
        grid=(batch, n_pairs),
        in_specs=[spec(0), spec(4), spec(8)] + [pl.BlockSpec(b.shape, lambda b_, p: (0, 0, 0)) for b in biases],
        out_specs=pl.BlockSpec((None, seq, LANES), lambda b, p: (b, 0, p)),
        out_shape=jax.ShapeDtypeStruct((batch, seq, n_pairs * LANES), BF16),
        scratch_shapes=[pltpu.VMEM((seq, LANES), BF16)] * 3 + [pltpu.VMEM((seq, LANES), F32)] * (3 * len(configs) + 3),
        compiler_params=_params("parallel", "parallel"),
        name="dilated",
    )(dil_proj, dil_proj, dil_proj, *biases)


def _na_group_geometry(rows, kr):
    n_groups = rows // NA_GROUP
    wr = -(-(kr + NA_GROUP - 1) * GRID_W // LANES) * LANES // GRID_W
    g = np.arange(n_groups)
    start = np.minimum(np.clip(g * NA_GROUP - kr // 2, 0, rows - kr), rows - wr)
    return n_groups, wr, start


def _na_kernel(start_ref, kind_ref, q_ref, k_ref, v_ref, bias_ref, o_ref, *, rows, kr):
    n_groups, wr, _ = _na_group_geometry(rows, kr)
    tq = NA_GROUP * GRID_W

    def body(i, carry):
        geo = []
        for u in range(NA_UNROLL):
            g = i * NA_UNROLL + u
            geo.append((pl.multiple_of(g * tq, tq), pl.multiple_of(start_ref[g] * GRID_W, GRID_W), kind_ref[g]))
        def score(u):
            q0, k0, kind = geo[u]
            return _pair_scores(q_ref[pl.ds(q0, tq), :], k_ref[pl.ds(k0, wr * GRID_W), :]) + bias_ref[kind]

        ss = {u: score(u) for u in range(min(NA_SKEW, NA_UNROLL))}
        for u in range(NA_UNROLL):
            if u + NA_SKEW < NA_UNROLL:
                ss[u + NA_SKEW] = score(u + NA_SKEW)
            s = ss.pop(u)
            q0, k0, _ = geo[u]
            m = jnp.max(s, axis=-1, keepdims=True)
            p = jnp.exp2(s - m)
            l = jnp.sum(p, axis=-1, keepdims=True)
            o2 = _dot(p.astype(BF16), v_ref[pl.ds(k0, wr * GRID_W), :])
            o_ref[pl.ds(q0, tq), :] = _pair_merge(o2 / l, tq).astype(o_ref.dtype)
        return carry

    lax.fori_loop(0, n_groups // NA_UNROLL, body, 0)


def _na_bias_table(rpb, rows, kr):
    n_groups, wr, start = _na_group_geometry(rows, kr)
    qcol = np.arange(GRID_W)
    kcol = np.arange(GRID_W)
    wstart = np.clip(qcol - NA_COLS // 2, 0, GRID_W - NA_COLS)
    col_ok = (kcol[None, :] >= wstart[:, None]) & (kcol[None, :] < wstart[:, None] + NA_COLS)
    dc = np.clip(kcol[None, :] - qcol[:, None] + NA_COLS - 1, 0, 2 * NA_COLS - 2)

    def geometry(g):
        qrow = g * NA_GROUP + np.arange(NA_GROUP)
        krow = start[g] + np.arange(wr)
        rs = np.clip(qrow - kr // 2, 0, rows - kr)
        row_ok = (krow[None, :] >= rs[:, None]) & (krow[None, :] < rs[:, None] + kr)
        dr = np.where(row_ok, krow[None, :] - qrow[:, None] + NA_ROWS - 1, -1)
        return dr

    patterns, kind_of = [], []
    for g in range(n_groups):
        dr = geometry(g)
        match = [k for k, p in enumerate(patterns) if np.array_equal(p, dr)]
        if not match:
            patterns.append(dr)
        kind_of.append(match[0] if match else len(patterns) - 1)
    onehot = (dc[None] == np.arange(2 * NA_COLS - 1)[:, None, None]) & col_ok[None]
    planes = jnp.einsum("hrd,dqk->hrqk", rpb * LOG2_E, jnp.asarray(onehot, F32), precision=lax.Precision.HIGHEST)
    planes = planes + jnp.asarray(np.where(col_ok, 0.0, NEG_INF), F32)
    masked = jnp.full(planes.shape[:1] + planes.shape[2:], NEG_INF, F32)
    slabs = []
    for dr in patterns:
        rows_i = []
        for i in range(NA_GROUP):
            cols_j = [planes[:, int(dr[i, j])] if dr[i, j] >= 0 else masked for j in range(wr)]
            rows_i.append(jnp.concatenate(cols_j, axis=-1))
        slab = jnp.stack(rows_i, axis=1)
        slabs.append(slab.reshape(NA_HEADS // 2, 2 * NA_GROUP * GRID_W, wr * GRID_W))
    table = jnp.stack(slabs)
    return table, jnp.asarray(start, jnp.int32), jnp.asarray(kind_of, jnp.int32)


def _neighbourhood(proj, rpb, batch, seq):
    rows = seq // GRID_W
    kr = min(NA_ROWS, rows)
    assert rows % (NA_GROUP * NA_UNROLL) == 0 and rows >= kr + NA_GROUP
    table, start, kind_of = _na_bias_table(rpb.astype(F32), rows, kr)
    n_pairs = NA_HEADS // 2

    def spec(base):
        return pl.BlockSpec((None, seq, LANES), lambda b, p: (b, 0, base + p))

    smem = pl.BlockSpec(memory_space=pltpu.SMEM)
    return pl.pallas_call(
        functools.partial(_na_kernel, rows=rows, kr=kr),
        grid=(batch, n_pairs),
        in_specs=[smem, smem, spec(0), spec(4), spec(8),
                  pl.BlockSpec((table.shape[0], None) + table.shape[2:], lambda b, p: (0, p, 0, 0))],
        out_specs=pl.BlockSpec((None, seq, LANES), lambda b, p: (b, 0, p)),
        out_shape=jax.ShapeDtypeStruct((batch, seq, n_pairs * LANES), BF16),
        compiler_params=_params("parallel", "parallel"),
        name="neighbourhood",
    )(start, kind_of, proj, proj, proj, table)


_ODD_W_COLS = 14 * LANES
_ODD_OUT_COLS = 24 * LANES


def _odd_proj_kernel(x_ref, g_ref, w_ref, qn_ref, wqb_ref, kvn_ref, wkb_ref, wvbt_ref,
                     cos_ref, sin_ref, cosr_ref, sinr_ref, o_ref, vt_ref):
    tm = x_ref.shape[0] // PROJ_SUB
    for t in range(PROJ_SUB):
        rows = slice(t * tm, (t + 1) * tm)
        _odd_proj_rows(x_ref.at[rows], g_ref, w_ref, qn_ref, wqb_ref, kvn_ref, wkb_ref, wvbt_ref,
                       cos_ref.at[rows], sin_ref.at[rows], cosr_ref.at[rows], sinr_ref.at[rows],
                       o_ref.at[rows], vt_ref.at[:, rows])


def _odd_proj_rows(x_ref, g_ref, w_ref, qn_ref, wqb_ref, kvn_ref, wkb_ref, wvbt_ref,
                   cos_ref, sin_ref, cosr_ref, sinr_ref, o_ref, vt_ref):
    hn = _rms(x_ref[...], g_ref[...]).astype(BF16)
    cos, sin = cos_ref[...], sin_ref[...]
    cosr, sinr = cosr_ref[...], sinr_ref[...]
    lane = lax.broadcasted_iota(jnp.int32, cos.shape, 1)
    first_half = (lane % HEAD_DIM) < HEAD_DIM // 2
    first_half_r = lane < MLA_NOPE + MLA_ROPE // 2
    swa_scale = HEAD_DIM ** -0.5 * LOG2_E
    mla_scale = (MLA_NOPE + MLA_ROPE) ** -0.5 * LOG2_E

    def col(j, n=1):
        return slice(j * LANES, (j + n) * LANES)

    def rope(blk):
        return _rope_block(blk, cos, sin, first_half, LANES - HEAD_DIM // 2, HEAD_DIM // 2)

    def rope_r(blk):
        return _rope_block(blk, cosr, sinr, first_half_r, LANES - MLA_ROPE // 2, MLA_ROPE // 2)

    def halves(wide):
        return wide[:, :LANES], wide[:, LANES:]

    latent = _dot(hn, w_ref[:, col(8, 6)])
    for j in range(0, 6, 2):
        for jj, blk in zip((j, j + 1), halves(_dot(hn, w_ref[:, col(j, 2)]))):
            blk = rope(blk) * swa_scale if jj < 4 else rope(blk)
            o_ref[:, col(jj)] = blk.astype(BF16)

    q_an = _rms(latent[:, :3 * LANES], qn_ref[...]).astype(BF16)
    kv_an = _rms(latent[:, 3 * LANES:5 * LANES], kvn_ref[...]).astype(BF16)
    k_pe = rope_r(latent[:, 5 * LANES:])
    for h in range(0, MLA_HEADS, 2):
        for hh, qh in zip((h, h + 1), halves(_dot(q_an, wqb_ref[:, col(h, 2)]))):
            o_ref[:, col(8 + hh)] = (rope_r(qh) * mla_scale).astype(BF16)
    for h in range(0, MLA_HEADS, 2):
        for hh, kh in zip((h, h + 1), halves(_dot(kv_an, wkb_ref[:, col(h, 2)]))):
            o_ref[:, col(16 + hh)] = (kh + k_pe).astype(BF16)
    o_ref[:, col(6, 2)] = _dot(hn, w_ref[:, col(6, 2)]).astype(BF16)
    vt_ref[...] = _dot_nt(wvbt_ref[...], kv_an).astype(BF16)


def _odd_proj(h2, g, w, qn, wqb, kvn, wkb, wvbt, cos, sin, cosr, sinr, seq):
    m, d = h2.shape
    tm = FFN_TM * PROJ_SUB
    per_seq = seq // tm
    n_v = wvbt.shape[0]
    full = lambda a: pl.BlockSpec(a.shape, lambda i: (0, 0))
    tab = pl.BlockSpec((tm, LANES), lambda i: (i % per_seq, 0))
    return pl.pallas_call(
        _odd_proj_kernel,
        grid=(m // tm,),
        in_specs=[pl.BlockSpec((tm, d), lambda i: (i, 0)), full(g), full(w), full(qn), full(wqb),
                  full(kvn), full(wkb), full(wvbt), tab, tab, tab, tab],
        out_specs=[pl.BlockSpec((tm, _ODD_OUT_COLS), lambda i: (i, 0)),
                   pl.BlockSpec((None, n_v, tm), lambda i: (i // per_seq, 0, i % per_seq))],
        out_shape=[jax.ShapeDtypeStruct((m, _ODD_OUT_COLS), BF16),
                   jax.ShapeDtypeStruct((m // seq, n_v, seq), BF16)],
        compiler_params=_params("parallel"),
        name="odd_proj",
    )(h2, g, w, qn, wqb, kvn, wkb, wvbt, cos, sin, cosr, sinr)


def _odd_weights(w_in, w_qb, w_kvb):
    d = w_in.shape[0]
    c = np.cumsum([0, SWA_Q_HEADS * HEAD_DIM, SWA_KV_HEADS * HEAD_DIM, SWA_KV_HEADS * HEAD_DIM,
                   MLA_Q_RANK, MLA_KV_RANK, MLA_ROPE])

    def dup(w):
        w = w.reshape(d, SWA_KV_HEADS, 1, HEAD_DIM)
        return jnp.broadcast_to(w, (d, SWA_KV_HEADS, 2, HEAD_DIM)).reshape(d, SWA_KV_HEADS * LANES)

    w_kpe = jnp.pad(w_in[:, c[5]:c[6]], ((0, 0), (MLA_NOPE, LANES - MLA_NOPE - MLA_ROPE)))
    w = jnp.concatenate([w_in[:, c[0]:c[1]], dup(w_in[:, c[1]:c[2]]), dup(w_in[:, c[2]:c[3]]),
                         w_in[:, c[3]:c[4]], w_in[:, c[4]:c[5]], w_kpe], axis=1).astype(BF16)
    qk = MLA_NOPE + MLA_ROPE
    wqb = jnp.pad(w_qb.reshape(MLA_Q_RANK, MLA_HEADS, qk), ((0, 0), (0, 0), (0, LANES - qk)))
    wqb = wqb.reshape(MLA_Q_RANK, MLA_HEADS * LANES).astype(BF16)
    kv = w_kvb.reshape(MLA_KV_RANK, MLA_HEADS, MLA_NOPE + MLA_V)
    wkb = jnp.pad(kv[:, :, :MLA_NOPE], ((0, 0), (0, 0), (0, LANES - MLA_NOPE)))
    wkb = wkb.reshape(MLA_KV_RANK, MLA_HEADS * LANES).astype(BF16)
    wvbt = kv[:, :, MLA_NOPE:].reshape(MLA_KV_RANK, MLA_HEADS * MLA_V).T.astype(BF16)
    return w, wqb, wkb, wvbt


def _swa_kernel(sink_ref, q_ref, k_ref, v_ref, bias_ref, o_ref, *, seq, half, tq, win):
    pair = pl.program_id(1)
    first = lax.broadcasted_iota(jnp.int32, (2 * tq, 1), 0) < tq
    sink = jnp.where(first, sink_ref[2 * pair], sink_ref[2 * pair + 1]) * LOG2_E

    def body(i, carry):
        geo = []
        for g in range(SWA_UNROLL):
            q0 = pl.multiple_of((i * SWA_UNROLL + g) * tq, tq)
            geo.append((q0,) + _band_window(q0, tq, half, seq, win))
        def score(u):
            q0, ks, kind = geo[u]
            return _pair_scores(q_ref[pl.ds(q0, tq), :], k_ref[pl.ds(ks, win), :]) + bias_ref[kind]

        ss = {u: score(u) for u in range(min(SWA_SKEW, SWA_UNROLL))}
        for u in range(SWA_UNROLL):
            if u + SWA_SKEW < SWA_UNROLL:
                ss[u + SWA_SKEW] = score(u + SWA_SKEW)
            s = ss.pop(u)
            q0, ks, _ = geo[u]
            m = jnp.maximum(jnp.max(s, axis=-1, keepdims=True), sink)
            p = jnp.exp2(s - m)
            l = jnp.sum(p, axis=-1, keepdims=True) + jnp.exp2(sink - m)
            o2 = _dot(p.astype(BF16), v_ref[pl.ds(ks, win), :])
            o_ref[pl.ds(q0, tq), :] = _pair_merge(o2 / l, tq).astype(o_ref.dtype)
        return carry

    lax.fori_loop(0, seq // (tq * SWA_UNROLL), body, 0)


def _swa(proj, sink, batch, seq):
    tq = SWA_TQ
    win = tq + 2 * SWA_HALF
    assert seq % (tq * SWA_UNROLL) == 0 and tq % SWA_HALF == 0 and seq >= win
    n_pairs = SWA_Q_HEADS // 2
    pairs_per_kv = n_pairs // SWA_KV_HEADS
    bias = _band_bias(tq, SWA_HALF, win)
    return pl.pallas_call(
        functools.partial(_swa_kernel, seq=seq, half=SWA_HALF, tq=tq, win=win),
        grid=(batch, n_pairs),
        in_specs=[pl.BlockSpec(memory_space=pltpu.SMEM),
                  pl.BlockSpec((None, seq, LANES), lambda b, p: (b, 0, p)),
                  pl.BlockSpec((None, seq, LANES), lambda b, p: (b, 0, 4 + p // pairs_per_kv)),
                  pl.BlockSpec((None, seq, LANES), lambda b, p: (b, 0, 6 + p // pairs_per_kv)),
                  pl.BlockSpec(bias.shape, lambda b, p: (0, 0, 0))],
        out_specs=pl.BlockSpec((None, seq, LANES), lambda b, p: (b, 0, p)),
        out_shape=jax.ShapeDtypeStruct((batch, seq, n_pairs * LANES), BF16),
        compiler_params=_params("parallel", "parallel"),
        name="swa",
    )(sink, proj, proj, proj, bias)


def _mla_kernel(q_ref, k_ref, vt_ref, o_ref, s_ref, p_ref, *, seq, tk):
    tq = MLA_TQ
    chunks = [slice(c * tk, (c + 1) * tk) for c in range(seq // tk)]
    n_units = 2 * (q_ref.shape[1] // tq)

    def fold(x):
        return x.reshape(tk // SUBLANES, SUBLANES, tq)

    def scores(u):
        t, h = divmod(u, 2)
        q = q_ref[0, t * tq:(t + 1) * tq, h * LANES:(h + 1) * LANES]
        m_run = jnp.full((SUBLANES, tq), NEG_INF, F32)
        for ck in chunks:
            st = _dot_nt(k_ref[0, ck, h * LANES:(h + 1) * LANES], q)
            s_ref[u % 2, ck, :] = st
            m_run = jnp.maximum(m_run, jnp.max(fold(st), axis=0))
        return jnp.max(m_run, axis=0, keepdims=True)

    def probs(u, m):
        l_run = jnp.zeros((SUBLANES, tq), F32)
        for ck in chunks:
            pt = jnp.exp2(s_ref[u % 2, ck, :] - m)
            l_run = l_run + jnp.sum(fold(pt), axis=0)
            p_ref[u % 2, ck, :] = pt.astype(BF16)
        return jnp.sum(l_run, axis=0, keepdims=True)

    m = scores(0)
    out_a = None
    for u in range(n_units):
        m_next = scores(u + 1) if u + 1 < n_units else None
        l = probs(u, m)
        h = u % 2
        out_t = _dot(vt_ref[h * MLA_V:(h + 1) * MLA_V, :], p_ref[u % 2]) / l
        if h == 0:
            out_a = out_t
        else:
            t = u // 2
            pair = jnp.concatenate([out_a, out_t], axis=0)
            o_ref[0, t * tq:(t + 1) * tq, :] = pair.T.astype(o_ref.dtype)
        m = m_next


def _mla(proj, vt, batch, seq):
    tq = MLA_TQ * MLA_SUB
    assert seq % tq == 0 and seq % MLA_TK == 0 and 2 * MLA_V == LANES
    n_pairs = MLA_HEADS // 2
    return pl.pallas_call(
        functools.partial(_mla_kernel, seq=seq, tk=MLA_TK),
        grid=(batch, n_pairs, seq // tq),
        in_specs=[pl.BlockSpec((1, tq, 2 * LANES), lambda b, p, i: (b, i, 4 + p)),
                  pl.BlockSpec((1, seq, 2 * LANES), lambda b, p, i: (b, 0, 8 + p)),
                  pl.BlockSpec((None, 2 * MLA_V, seq), lambda b, p, i: (b, p, 0))],
        out_specs=pl.BlockSpec((1, tq, LANES), lambda b, p, i: (b, i, p)),
        out_shape=jax.ShapeDtypeStruct((batch, seq, n_pairs * LANES), BF16),
        scratch_shapes=[pltpu.VMEM((2, seq, MLA_TQ), F32), pltpu.VMEM((2, seq, MLA_TQ), BF16)],
        compiler_params=_params("parallel", "parallel", "parallel"),
        name="mla",
    )(proj, proj, vt)


def _rope_tables(seq):
    pos = jnp.arange(seq, dtype=F32)

    def tables(dim):
        inv_freq = ROPE_THETA ** (-jnp.arange(0, dim, 2, dtype=F32) / dim)
        ang = pos[:, None] * inv_freq[None, :]
        return jnp.cos(ang), jnp.sin(ang)

    cos, sin = tables(HEAD_DIM)
    reps = LANES // HEAD_DIM
    cos_t = jnp.tile(jnp.concatenate([cos, cos], axis=1), (1, reps))
    sin_t = jnp.tile(jnp.concatenate([-sin, sin], axis=1), (1, reps))
    cos_r, sin_r = tables(MLA_ROPE)
    pad_l, pad_r = MLA_NOPE, LANES - MLA_NOPE - MLA_ROPE
    cos_rt = jnp.pad(jnp.concatenate([cos_r, cos_r], axis=1), ((0, 0), (pad_l, pad_r)), constant_values=1.0)
    sin_rt = jnp.pad(jnp.concatenate([-sin_r, sin_r], axis=1), ((0, 0), (pad_l, pad_r)))
    return cos_t, sin_t, cos_rt, sin_rt


def kernel(x, ffn1_norm, ffn1_w1, ffn1_w3, ffn1_w2, mix_norm, ffn2_norm, ffn2_w1, ffn2_w3, ffn2_w2, even_w_in, even_w_out, na_rel_bias, odd_w_in, odd_w_out, swa_sink, mla_q_norm, mla_w_qb, mla_kv_norm, mla_w_kvb, final_norm):
    batch, seq, d = x.shape
    depth = ffn1_norm.shape[0]
    assert d == D_MODEL and seq % (FFN_TM * PROJ_SUB) == 0 and (batch * seq) % (FFN_TM * FFN_SUB) == 0
    assert seq % (GRID_W * NA_ROWS) == 0
    cos_t, sin_t, cos_rt, sin_rt = _rope_tables(seq)
    h = x.reshape(batch * seq, d)
    row = lambda v: v.reshape(1, -1).astype(F32)
    fg = row(final_norm)

    ffn1_w = _ffn_weights(ffn1_w1, ffn1_w3, ffn1_w2)
    ffn2_w = _ffn_weights(ffn2_w1, ffn2_w3, ffn2_w2)
    for i in range(depth):
        j = i // 2
        h = _ffn(h, row(ffn1_norm[i]), *ffn1_w, i, fg, False)
        if i % 2 == 0:
            dil_proj, na_proj = _even_proj(h, row(mix_norm[i]), even_w_in[j].astype(BF16), cos_t, sin_t, seq)
            oa = _dilated(dil_proj.reshape(batch, seq, -1), batch, seq).reshape(batch * seq, -1)
            on = _neighbourhood(na_proj.reshape(batch, seq, -1), na_rel_bias[j], batch, seq).reshape(batch * seq, -1)
            mix = (oa, on, even_w_out[j].astype(BF16))
        else:
            w, wqb, wkb, wvbt = _odd_weights(odd_w_in[j], mla_w_qb[j], mla_w_kvb[j])
            proj, vt = _odd_proj(h, row(mix_norm[i]), w, row(mla_q_norm[j]), wqb, row(mla_kv_norm[j]), wkb, wvbt,
                                 cos_t, sin_t, cos_rt, sin_rt, seq)
            proj = proj.reshape(batch, seq, -1)
            oc = _swa(proj, swa_sink[j].astype(F32), batch, seq).reshape(batch * seq, -1)
            od = _mla(proj, vt, batch, seq).reshape(batch * seq, -1)
            mix = (oc, od, odd_w_out[j].astype(BF16))
        h = _ffn(h, row(ffn2_norm[i]), *ffn2_w, i, fg, i == depth - 1, mix)
    return h.reshape(batch, seq, d)
```

```python
import functools

import jax
import jax.numpy as jnp
import numpy as np
from jax import lax
from jax.experimental import pallas as pl
from jax.experimental.pallas import tpu as pltpu

D_MODEL = 1024
HEAD_DIM = 64
ROPE_THETA = 10000.0
NORM_EPS = 1e-6
NEG_INF = -1e30
LOG2_E = 1.4426950408889634

DIL_HEADS = 8
DIL_CONFIGS = ((128, 1), (512, 4), (2048, 16))
NA_HEADS = 8
GRID_W = 64
NA_ROWS = 8
NA_COLS = 16

SWA_Q_HEADS = 8
SWA_KV_HEADS = 2
SWA_HALF = 128
MLA_HEADS = 8
MLA_Q_RANK = 384
MLA_KV_RANK = 256
MLA_NOPE = 64
MLA_ROPE = 32
MLA_V = 64

LANES = 128
SUBLANES = 8
VMEM_LIMIT = 56 * 1024 * 1024

FFN_TM = 512
FFN_TF = 256
FFN_SUB = 2
PROJ_SUB = 2
MLA_TQ = 256
MLA_SUB = 8
MLA_TK = 1024
DIL_TQ = 128
DIL_UNROLL = 16
DIL_SKEW = 2
DIL_MERGE_ROWS = 512
SWA_TQ = 128
SWA_SKEW = 4
NA_GROUP = 1
NA_UNROLL = 32
NA_SKEW = 3

BF16 = jnp.bfloat16
F32 = jnp.float32


def _params(*sem):
    return pltpu.CompilerParams(dimension_semantics=sem, vmem_limit_bytes=VMEM_LIMIT)


def _rms(x, g):
    ms = jnp.mean(x * x, axis=-1, keepdims=True)
    return x * lax.rsqrt(ms + NORM_EPS) * g


def _dot(a, b):
    return jnp.dot(a, b, preferred_element_type=F32)


def _dot_nt(a, b):
    return lax.dot_general(a, b, (((1,), (1,)), ((), ())), preferred_element_type=F32)


def _lane_is_first_head(shape):
    return lax.broadcasted_iota(jnp.int32, shape, len(shape) - 1) < HEAD_DIM


def _rope_block(x, cos, sin_signed, first_half, up, down):
    partner = jnp.where(first_half, pltpu.roll(x, up, 1), pltpu.roll(x, down, 1))
    return x * cos + partner * sin_signed


def _ffn_kernel(*refs, mix, final):
    if mix:
        x_ref, oc_ref, od_ref, wa_ref, wb_ref, g_ref, w1_ref, w3_ref, w2_ref, fg_ref, o_ref, a_ref = refs
    else:
        x_ref, g_ref, w1_ref, w3_ref, w2_ref, fg_ref, o_ref, a_ref = refs
    tm = x_ref.shape[0] // FFN_SUB
    for t in range(FFN_SUB):
        rows = slice(t * tm, (t + 1) * tm)
        x = x_ref[rows, :]
        if mix:
            x = x + _dot(oc_ref[rows, :], wa_ref[...]) + _dot(od_ref[rows, :], wb_ref[...])
        hn = _rms(x, g_ref[...]).astype(BF16)
        for c in range(a_ref.shape[1] // FFN_TF):
            cols = slice(c * FFN_TF, (c + 1) * FFN_TF)
            h1 = _dot(hn, w1_ref[:, cols])
            h3 = _dot(hn, w3_ref[:, cols])
            a_ref[rows, cols] = (h1 * jax.nn.sigmoid(h1) * h3).astype(BF16)
        y = x + 0.5 * _dot(a_ref[rows, :], w2_ref[...])
        if final:
            y = _rms(y, fg_ref[...])
        o_ref[rows, :] = y


def _ffn(x2, g, w1, w3, w2, layer, fg, final, mix=None):
    m, d = x2.shape
    f = w1.shape[2]
    assert f % FFN_TF == 0
    tm = FFN_TM * FFN_SUB
    const = lambda i: (0, 0)
    of_layer = lambda i: (layer, 0, 0)
    resident = dict(pipeline_mode=pl.Buffered(1))
    mix_args, mix_specs = (), []
    if mix is not None:
        oc, od, w_out = mix
        half_w = w_out.shape[0] // 2
        mix_args = (oc, od, w_out[:half_w], w_out[half_w:])
        mix_specs = [pl.BlockSpec((tm, half_w), lambda i: (i, 0))] * 2 + [pl.BlockSpec((half_w, d), const, **resident)] * 2
    return pl.pallas_call(
        functools.partial(_ffn_kernel, mix=mix is not None, final=final),
        grid=(m // tm,),
        in_specs=[pl.BlockSpec((tm, d), lambda i: (i, 0))] + mix_specs + [
            pl.BlockSpec((1, d), const),
            pl.BlockSpec((None, d, f), of_layer, **resident),
            pl.BlockSpec((None, d, f), of_layer, **resident),
            pl.BlockSpec((None, f, d), of_layer, **resident),
            pl.BlockSpec((1, d), const),
        ],
        out_specs=pl.BlockSpec((tm, d), lambda i: (i, 0)),
        out_shape=jax.ShapeDtypeStruct((m, d), F32),
        scratch_shapes=[pltpu.VMEM((tm, f), BF16)],
        compiler_params=_params("parallel"),
        name="ffn",
    )(x2, *mix_args, g, w1, w3, w2, fg)


def _ffn_weights(w1, w3, w2):
    return w1.astype(BF16), w3.astype(BF16), w2.astype(BF16)


_EVEN_DIL_BLOCKS = 12
_EVEN_Q_BLOCKS = tuple(range(0, 4)) + tuple(range(12, 16))
_EVEN_ROPE_BLOCKS = 8


def _even_proj_kernel(x_ref, g_ref, w_ref, cos_ref, sin_ref, dil_ref, na_ref, *, n_blocks):
    tm = x_ref.shape[0] // PROJ_SUB
    for t in range(PROJ_SUB):
        rows = slice(t * tm, (t + 1) * tm)
        _even_proj_rows(x_ref.at[rows], g_ref, w_ref, cos_ref.at[rows], sin_ref.at[rows],
                        dil_ref.at[rows], na_ref.at[rows], n_blocks=n_blocks)


def _even_proj_rows(x_ref, g_ref, w_ref, cos_ref, sin_ref, dil_ref, na_ref, *, n_blocks):
    hn = _rms(x_ref[...], g_ref[...]).astype(BF16)
    cos = cos_ref[...]
    sin = sin_ref[...]
    lane = lax.broadcasted_iota(jnp.int32, cos.shape, 1)
    first_half = (lane % HEAD_DIM) < HEAD_DIM // 2
    scale = HEAD_DIM ** -0.5 * LOG2_E
    for j2 in range(0, n_blocks, 2):
        wide = _dot(hn, w_ref[:, j2 * LANES:(j2 + 2) * LANES])
        for j in (j2, j2 + 1):
            blk = wide[:, (j - j2) * LANES:(j - j2 + 1) * LANES]
            if j < _EVEN_ROPE_BLOCKS:
                blk = _rope_block(blk, cos, sin, first_half, LANES - HEAD_DIM // 2, HEAD_DIM // 2)
            if j in _EVEN_Q_BLOCKS:
                blk = blk * scale
            if j < _EVEN_DIL_BLOCKS:
                dil_ref[:, j * LANES:(j + 1) * LANES] = blk
            else:
                jn = j - _EVEN_DIL_BLOCKS
                na_ref[:, jn * LANES:(jn + 1) * LANES] = blk.astype(BF16)


def _even_proj(h2, g, w_in, cos, sin, seq):
    m, d = h2.shape
    n_out = w_in.shape[1]
    n_dil = _EVEN_DIL_BLOCKS * LANES
    tm = FFN_TM * PROJ_SUB
    per_seq = seq // tm
    return pl.pallas_call(
        functools.partial(_even_proj_kernel, n_blocks=n_out // LANES),
        grid=(m // tm,),
        in_specs=[
            pl.BlockSpec((tm, d), lambda i: (i, 0)),
            pl.BlockSpec((1, d), lambda i: (0, 0)),
            pl.BlockSpec((d, n_out), lambda i: (0, 0)),
            pl.BlockSpec((tm, LANES), lambda i: (i % per_seq, 0)),
            pl.BlockSpec((tm, LANES), lambda i: (i % per_seq, 0)),
        ],
        out_specs=[pl.BlockSpec((tm, n_dil), lambda i: (i, 0)),
                   pl.BlockSpec((tm, n_out - n_dil), lambda i: (i, 0))],
        out_shape=[jax.ShapeDtypeStruct((m, n_dil), F32),
                   jax.ShapeDtypeStruct((m, n_out - n_dil), BF16)],
        compiler_params=_params("parallel"),
        name="even_proj",
    )(h2, g, w_in, cos, sin)


def _pair_scores(q, k):
    first = _lane_is_first_head(q.shape)
    zero = jnp.zeros_like(q)
    q2 = jnp.concatenate([jnp.where(first, q, zero), jnp.where(first, zero, q)], axis=0)
    return _dot_nt(q2, k)


def _pair_merge(x2, tq):
    first = _lane_is_first_head((tq, LANES))
    return jnp.where(first, x2[:tq], x2[tq:])


def _band_bias(tq, half, win):
    row = np.arange(2 * tq)[:, None] % tq
    col = np.arange(win)[None, :]
    kinds = [np.where(np.abs(col - row - k * half) <= half, 0.0, NEG_INF) for k in range(3)]
    return jnp.asarray(np.stack(kinds), F32)


def _band_window(q0, tq, half, length, win):
    ks = pl.multiple_of(jnp.clip(q0 - half, 0, length - win), half)
    return ks, (q0 - ks) // half


def _dilated_kernel(q_ref, k_ref, v_ref, *refs, seq, configs, tiles):
    n = len(configs)
    bias_refs, o_ref = refs[:n], refs[n]
    qs_ref, ks_ref, vs_ref = refs[n + 1:n + 4]
    ob_refs, mb_refs, lb_refs = refs[n + 4:2 * n + 4], refs[2 * n + 4:3 * n + 4], refs[3 * n + 4:4 * n + 4]
    stage_refs = refs[4 * n + 4:]
    copies = {1: (q_ref, k_ref, v_ref)}
    for c, ((half, dil), (tq, win)) in enumerate(zip(configs, tiles)):
        length = seq // dil
        blocks = length // tq
        base_dil = max(d for d in copies if dil % d == 0)
        step = dil // base_dil
        keep = dil > 1 and any(d2 > dil and d2 % dil == 0 for _, d2 in configs)
        if keep:
            copies[dil] = stage_refs
        for a, (src, dst) in enumerate(zip(copies[base_dil], (qs_ref, ks_ref, vs_ref))):
            for r in range(dil):
                start = (r % base_dil) * (seq // base_dil) + r // base_dil
                rows = pl.ds(start, length, stride=step) if step > 1 else pl.ds(start, length)
                x = src[rows, :]
                dst[r * length:(r + 1) * length, :] = x.astype(BF16)
                if keep:
                    stage_refs[a][r * length:(r + 1) * length, :] = x

        def body(i, carry, c=c, half=half, dil=dil, tq=tq, win=win, length=length, blocks=blocks):
            geo = []
            for g in range(DIL_UNROLL):
                n = i * DIL_UNROLL + g
                r = n // blocks
                q0 = pl.multiple_of((n % blocks) * tq, tq)
                ks, kind = _band_window(q0, tq, half, length, win)
                geo.append((r, q0, pl.multiple_of(r * length, tq) + ks, kind))
            def score(u):
                r, q0, kb, kind = geo[u]
                q = qs_ref[pl.ds(pl.multiple_of(r * length, tq) + q0, tq), :]
                return _pair_scores(q, ks_ref[pl.ds(kb, win), :]) + bias_refs[c][kind]

            ss = {u: score(u) for u in range(min(DIL_SKEW, DIL_UNROLL))}
            for u in range(DIL_UNROLL):
                if u + DIL_SKEW < DIL_UNROLL:
                    ss[u + DIL_SKEW] = score(u + DIL_SKEW)
                s = ss.pop(u)
                r, q0, kb, _ = geo[u]
                m = jnp.max(s, axis=-1, keepdims=True)
                p = jnp.exp2(s - m)
                l = jnp.sum(p, axis=-1, keepdims=True)
                o2 = _dot(p.astype(BF16), vs_ref[pl.ds(kb, win), :])
                rows = pl.ds(r + dil * q0, tq, stride=dil) if dil > 1 else pl.ds(q0, tq)
                ob_refs[c][rows, :] = _pair_merge(o2, tq)
                mb_refs[c][rows, :] = _pair_merge(jnp.broadcast_to(m, o2.shape), tq)
                lb_refs[c][rows, :] = _pair_merge(jnp.broadcast_to(l, o2.shape), tq)
            return carry

        lax.fori_loop(0, seq // (tq * DIL_UNROLL), body, 0)

    chunk = DIL_MERGE_ROWS

    def merge(i, carry):
        rows = pl.ds(pl.multiple_of(i * chunk, chunk), chunk)
        ms = [mb[rows, :] for mb in mb_refs]
        mx = functools.reduce(jnp.maximum, ms)
        ws = [jnp.exp2(m - mx) for m in ms]
        num = sum(w * ob[rows, :] for w, ob in zip(ws, ob_refs))
        den = sum(w * lb[rows, :] for w, lb in zip(ws, lb_refs))
        o_ref[rows, :] = (num / den).astype(o_ref.dtype)
        return carry

    lax.fori_loop(0, seq // chunk, merge, 0)


def _dilated(dil_proj, batch, seq):
    n_pairs = DIL_HEADS // 2
    configs = tuple((window // 2 // dil, dil) for window, dil in DIL_CONFIGS)
    tiles = []
    for half, dil in configs:
        tq = min(DIL_TQ, seq // dil)
        tiles.append((tq, min(tq + 2 * half, seq // dil)))
        assert seq % (dil * tq) == 0 and tq % half == 0
    biases = [_band_bias(tq, half, win) for (half, _), (tq, win) in zip(configs, tiles)]

    def spec(base):
        return pl.BlockSpec((None, seq, LANES), lambda b, p: (b, 0, base + p))

    return pl.pallas_call(
        functools.partial(_dilated_kernel, seq=seq, configs=configs, tiles=tuple(tiles)),
        grid=(batch, n_pairs),
        in_specs=[spec(0), spec(4), spec(8)] + [pl.BlockSpec(b.shape, lambda b_, p: (0, 0, 0)) for b in biases],
        out_specs=pl.BlockSpec((None, seq, LANES), lambda b, p: (b, 0, p)),
        out_shape=jax.ShapeDtypeStruct((batch, seq, n_pairs * LANES), BF16),
        scratch_shapes=[pltpu.VMEM((seq, LANES), BF16)] * 3 + [pltpu.VMEM((seq, LANES), F32)] * (3 * len(configs) + 3),
        compiler_params=_params("parallel", "parallel"),
        name="dilated",
    )(dil_proj, dil_proj, dil_proj, *biases)


def _na_group_geometry(rows, kr):
    n_groups = rows // NA_GROUP
    wr = -(-(kr + NA_GROUP - 1) * GRID_W // LANES) * LANES // GRID_W
    g = np.arange(n_groups)
    start = np.minimum(np.clip(g * NA_GROUP - kr // 2, 0, rows - kr), rows - wr)
    return n_groups, wr, start


def _na_kernel(start_ref, kind_ref, q_ref, k_ref, v_ref, bias_ref, o_ref, *, rows, kr):
    n_groups, wr, _ = _na_group_geometry(rows, kr)
    tq = NA_GROUP * GRID_W

    def body(i, carry):
        geo = []
        for u in range(NA_UNROLL):
            g = i * NA_UNROLL + u
            geo.append((pl.multiple_of(g * tq, tq), pl.multiple_of(start_ref[g] * GRID_W, GRID_W), kind_ref[g]))
        def score(u):
            q0, k0, kind = geo[u]
            return _pair_scores(q_ref[pl.ds(q0, tq), :], k_ref[pl.ds(k0, wr * GRID_W), :]) + bias_ref[kind]

        ss = {u: score(u) for u in range(min(NA_SKEW, NA_UNROLL))}
        for u in range(NA_UNROLL):
            if u + NA_SKEW < NA_UNROLL:
                ss[u + NA_SKEW] = score(u + NA_SKEW)
            s = ss.pop(u)
            q0, k0, _ = geo[u]
            m = jnp.max(s, axis=-1, keepdims=True)
            p = jnp.exp2(s - m)
            l = jnp.sum(p, axis=-1, keepdims=True)
            o2 = _dot(p.astype(BF16), v_ref[pl.ds(k0, wr * GRID_W), :])
            o_ref[pl.ds(q0, tq), :] = _pair_merge(o2 / l, tq).astype(o_ref.dtype)
        return carry

    lax.fori_loop(0, n_groups // NA_UNROLL, body, 0)


def _na_bias_table(rpb, rows, kr):
    n_groups, wr, start = _na_group_geometry(rows, kr)
    qcol = np.arange(GRID_W)
    kcol = np.arange(GRID_W)
    wstart = np.clip(qcol - NA_COLS // 2, 0, GRID_W - NA_COLS)
    col_ok = (kcol[None, :] >= wstart[:, None]) & (kcol[None, :] < wstart[:, None] + NA_COLS)
    dc = np.clip(kcol[None, :] - qcol[:, None] + NA_COLS - 1, 0, 2 * NA_COLS - 2)

    def geometry(g):
        qrow = g * NA_GROUP + np.arange(NA_GROUP)
        krow = start[g] + np.arange(wr)
        rs = np.clip(qrow - kr // 2, 0, rows - kr)
        row_ok = (krow[None, :] >= rs[:, None]) & (krow[None, :] < rs[:, None] + kr)
        dr = np.where(row_ok, krow[None, :] - qrow[:, None] + NA_ROWS - 1, -1)
        return dr

    patterns, kind_of = [], []
    for g in range(n_groups):
        dr = geometry(g)
        match = [k for k, p in enumerate(patterns) if np.array_equal(p, dr)]
        if not match:
            patterns.append(dr)
        kind_of.append(match[0] if match else len(patterns) - 1)
    onehot = (dc[None] == np.arange(2 * NA_COLS - 1)[:, None, None]) & col_ok[None]
    planes = jnp.einsum("hrd,dqk->hrqk", rpb * LOG2_E, jnp.asarray(onehot, F32), precision=lax.Precision.HIGHEST)
    planes = planes + jnp.asarray(np.where(col_ok, 0.0, NEG_INF), F32)
    masked = jnp.full(planes.shape[:1] + planes.shape[2:], NEG_INF, F32)
    slabs = []
    for dr in patterns:
        rows_i = []
        for i in range(NA_GROUP):
            cols_j = [planes[:, int(dr[i, j])] if dr[i, j] >= 0 else masked for j in range(wr)]
            rows_i.append(jnp.concatenate(cols_j, axis=-1))
        slab = jnp.stack(rows_i, axis=1)
        slabs.append(slab.reshape(NA_HEADS // 2, 2 * NA_GROUP * GRID_W, wr * GRID_W))
    table = jnp.stack(slabs)
    return table, jnp.asarray(start, jnp.int32), jnp.asarray(kind_of, jnp.int32)


def _neighbourhood(proj, rpb, batch, seq):
    rows = seq // GRID_W
    kr = min(NA_ROWS, rows)
    assert rows % (NA_GROUP * NA_UNROLL) == 0 and rows >= kr + NA_GROUP
    table, start, kind_of = _na_bias_table(rpb.astype(F32), rows, kr)
    n_pairs = NA_HEADS // 2

    def spec(base):
        return pl.BlockSpec((None, seq, LANES), lambda b, p: (b, 0, base + p))

    smem = pl.BlockSpec(memory_space=pltpu.SMEM)
    return pl.pallas_call(
        functools.partial(_na_kernel, rows=rows, kr=kr),
        grid=(batch, n_pairs),
        in_specs=[smem, smem, spec(0), spec(4), spec(8),
                  pl.BlockSpec((table.shape[0], None) + table.shape[2:], lambda b, p: (0, p, 0, 0))],
        out_specs=pl.BlockSpec((None, seq, LANES), lambda b, p: (b, 0, p)),
        out_shape=jax.ShapeDtypeStruct((batch, seq, n_pairs * LANES), BF16),
        compiler_params=_params("parallel", "parallel"),
        name="neighbourhood",
    )(start, kind_of, proj, proj, proj, table)


_ODD_W_COLS = 14 * LANES
_ODD_OUT_COLS = 24 * LANES


def _odd_proj_kernel(x_ref, g_ref, w_ref, qn_ref, wqb_ref, kvn_ref, wkb_ref, wvbt_ref, wvst_ref,
                     cos_ref, sin_ref, cosr_ref, sinr_ref, o_ref, vt_ref, vst_ref):
    tm = x_ref.shape[0] // PROJ_SUB
    for t in range(PROJ_SUB):
        rows = slice(t * tm, (t + 1) * tm)
        _odd_proj_rows(x_ref.at[rows], g_ref, w_ref, qn_ref, wqb_ref, kvn_ref, wkb_ref, wvbt_ref, wvst_ref,
                       cos_ref.at[rows], sin_ref.at[rows], cosr_ref.at[rows], sinr_ref.at[rows],
                       o_ref.at[rows], vt_ref.at[:, rows], vst_ref.at[:, rows])


def _odd_proj_rows(x_ref, g_ref, w_ref, qn_ref, wqb_ref, kvn_ref, wkb_ref, wvbt_ref, wvst_ref,
                   cos_ref, sin_ref, cosr_ref, sinr_ref, o_ref, vt_ref, vst_ref):
    hn = _rms(x_ref[...], g_ref[...]).astype(BF16)
    cos, sin = cos_ref[...], sin_ref[...]
    cosr, sinr = cosr_ref[...], sinr_ref[...]
    lane = lax.broadcasted_iota(jnp.int32, cos.shape, 1)
    first_half = (lane % HEAD_DIM) < HEAD_DIM // 2
    first_half_r = lane < MLA_NOPE + MLA_ROPE // 2
    swa_scale = HEAD_DIM ** -0.5 * LOG2_E
    mla_scale = (MLA_NOPE + MLA_ROPE) ** -0.5 * LOG2_E

    def col(j, n=1):
        return slice(j * LANES, (j + n) * LANES)

    def rope(blk):
        return _rope_block(blk, cos, sin, first_half, LANES - HEAD_DIM // 2, HEAD_DIM // 2)

    def rope_r(blk):
        return _rope_block(blk, cosr, sinr, first_half_r, LANES - MLA_ROPE // 2, MLA_ROPE // 2)

    def halves(wide):
        return wide[:, :LANES], wide[:, LANES:]

    latent = _dot(hn, w_ref[:, col(8, 6)])
    for j in range(0, 6, 2):
        for jj, blk in zip((j, j + 1), halves(_dot(hn, w_ref[:, col(j, 2)]))):
            blk = rope(blk) * swa_scale if jj < 4 else rope(blk)
            o_ref[:, col(jj)] = blk.astype(BF16)

    q_an = _rms(latent[:, :3 * LANES], qn_ref[...]).astype(BF16)
    kv_an = _rms(latent[:, 3 * LANES:5 * LANES], kvn_ref[...]).astype(BF16)
    k_pe = rope_r(latent[:, 5 * LANES:])
    for h in range(0, MLA_HEADS, 2):
        for hh, qh in zip((h, h + 1), halves(_dot(q_an, wqb_ref[:, col(h, 2)]))):
            o_ref[:, col(8 + hh)] = (rope_r(qh) * mla_scale).astype(BF16)
    for h in range(0, MLA_HEADS, 2):
        for hh, kh in zip((h, h + 1), halves(_dot(kv_an, wkb_ref[:, col(h, 2)]))):
            o_ref[:, col(16 + hh)] = (kh + k_pe).astype(BF16)
    o_ref[:, col(6, 2)] = jnp.zeros((o_ref.shape[0], 2 * LANES), BF16)
    vst_ref[...] = _dot_nt(wvst_ref[...], hn).astype(BF16)
    vt_ref[...] = _dot_nt(wvbt_ref[...], kv_an).astype(BF16)


def _odd_proj(h2, g, w, qn, wqb, kvn, wkb, wvbt, wvst, cos, sin, cosr, sinr, seq):
    m, d = h2.shape
    tm = FFN_TM * PROJ_SUB
    per_seq = seq // tm
    n_v = wvbt.shape[0]
    full = lambda a: pl.BlockSpec(a.shape, lambda i: (0, 0))
    tab = pl.BlockSpec((tm, LANES), lambda i: (i % per_seq, 0))
    return pl.pallas_call(
        _odd_proj_kernel,
        grid=(m // tm,),
        in_specs=[pl.BlockSpec((tm, d), lambda i: (i, 0)), full(g), full(w), full(qn), full(wqb),
                  full(kvn), full(wkb), full(wvbt), full(wvst), tab, tab, tab, tab],
        out_specs=[pl.BlockSpec((tm, _ODD_OUT_COLS), lambda i: (i, 0)),
                   pl.BlockSpec((None, n_v, tm), lambda i: (i // per_seq, 0, i % per_seq)),
                   pl.BlockSpec((None, wvst.shape[0], tm), lambda i: (i // per_seq, 0, i % per_seq))],
        out_shape=[jax.ShapeDtypeStruct((m, _ODD_OUT_COLS), BF16),
                   jax.ShapeDtypeStruct((m // seq, n_v, seq), BF16),
                   jax.ShapeDtypeStruct((m // seq, wvst.shape[0], seq), BF16)],
        compiler_params=_params("parallel"),
        name="odd_proj",
    )(h2, g, w, qn, wqb, kvn, wkb, wvbt, wvst, cos, sin, cosr, sinr)


def _odd_weights(w_in, w_qb, w_kvb):
    d = w_in.shape[0]
    c = np.cumsum([0, SWA_Q_HEADS * HEAD_DIM, SWA_KV_HEADS * HEAD_DIM, SWA_KV_HEADS * HEAD_DIM,
                   MLA_Q_RANK, MLA_KV_RANK, MLA_ROPE])

    def dup(w):
        w = w.reshape(d, SWA_KV_HEADS, 1, HEAD_DIM)
        return jnp.broadcast_to(w, (d, SWA_KV_HEADS, 2, HEAD_DIM)).reshape(d, SWA_KV_HEADS * LANES)

    w_kpe = jnp.pad(w_in[:, c[5]:c[6]], ((0, 0), (MLA_NOPE, LANES - MLA_NOPE - MLA_ROPE)))
    w = jnp.concatenate([w_in[:, c[0]:c[1]], dup(w_in[:, c[1]:c[2]]), dup(w_in[:, c[2]:c[3]]),
                         w_in[:, c[3]:c[4]], w_in[:, c[4]:c[5]], w_kpe], axis=1).astype(BF16)
    qk = MLA_NOPE + MLA_ROPE
    wqb = jnp.pad(w_qb.reshape(MLA_Q_RANK, MLA_HEADS, qk), ((0, 0), (0, 0), (0, LANES - qk)))
    wqb = wqb.reshape(MLA_Q_RANK, MLA_HEADS * LANES).astype(BF16)
    kv = w_kvb.reshape(MLA_KV_RANK, MLA_HEADS, MLA_NOPE + MLA_V)
    wkb = jnp.pad(kv[:, :, :MLA_NOPE], ((0, 0), (0, 0), (0, LANES - MLA_NOPE)))
    wkb = wkb.reshape(MLA_KV_RANK, MLA_HEADS * LANES).astype(BF16)
    wvbt = kv[:, :, MLA_NOPE:].reshape(MLA_KV_RANK, MLA_HEADS * MLA_V).T.astype(BF16)
    wvst = w_in[:, c[2]:c[3]].T.astype(BF16)
    return w, wqb, wkb, wvbt, wvst


def _swa_kernel(sink_ref, q_ref, k_ref, vt_ref, bias_ref, o_ref, *, seq, half, tq, win):
    pair = pl.program_id(1)
    first = lax.broadcasted_iota(jnp.int32, (1, 2 * tq), 1) < tq
    sink = jnp.where(first, sink_ref[2 * pair], sink_ref[2 * pair + 1]) * LOG2_E
    blocks = []
    for n in range(seq // tq):
        q0 = n * tq
        ks = min(max(q0 - half, 0), seq - win)
        blocks.append((q0, ks, (q0 - ks) // half))

    def score(u):
        q0, ks, kind = blocks[u]
        q = q_ref[q0:q0 + tq, :]
        head_a = _lane_is_first_head(q.shape)
        zero = jnp.zeros_like(q)
        q2 = jnp.concatenate([jnp.where(head_a, q, zero), jnp.where(head_a, zero, q)], axis=0)
        return _dot_nt(k_ref[ks:ks + win, :], q2) + bias_ref[kind]

    ss = {u: score(u) for u in range(min(SWA_SKEW, len(blocks)))}
    for u, (q0, ks, _) in enumerate(blocks):
        if u + SWA_SKEW < len(blocks):
            ss[u + SWA_SKEW] = score(u + SWA_SKEW)
        st = ss.pop(u)
        m = jnp.maximum(jnp.max(st, axis=0, keepdims=True), sink)
        pt = jnp.exp2(st - m)
        l = jnp.sum(pt, axis=0, keepdims=True) + jnp.exp2(sink - m)
        ot = _dot(vt_ref[:, ks:ks + win], pt.astype(BF16)) / l
        both = jnp.concatenate([ot[:, :tq], ot[:, tq:]], axis=0)
        o_ref[q0:q0 + tq, :] = both.T.astype(o_ref.dtype)


def _swa_bias(tq, half, win):
    key = np.arange(win)[:, None]
    qry = np.arange(2 * tq)[None, :] % tq
    kinds = [np.where(np.abs(key - qry - k * half) <= half, 0.0, NEG_INF) for k in range(3)]
    return jnp.asarray(np.stack(kinds), F32)


def _swa(proj, vt, sink, batch, seq):
    tq = SWA_TQ
    win = tq + 2 * SWA_HALF
    assert seq % tq == 0 and tq % SWA_HALF == 0 and seq >= win and tq == LANES
    n_pairs = SWA_Q_HEADS // 2
    pairs_per_kv = n_pairs // SWA_KV_HEADS
    bias = _swa_bias(tq, SWA_HALF, win)
    return pl.pallas_call(
        functools.partial(_swa_kernel, seq=seq, half=SWA_HALF, tq=tq, win=win),
        grid=(batch, n_pairs),
        in_specs=[pl.BlockSpec(memory_space=pltpu.SMEM),
                  pl.BlockSpec((None, seq, LANES), lambda b, p: (b, 0, p)),
                  pl.BlockSpec((None, seq, LANES), lambda b, p: (b, 0, 4 + p // pairs_per_kv)),
                  pl.BlockSpec((None, HEAD_DIM, seq), lambda b, p: (b, p // pairs_per_kv, 0)),
                  pl.BlockSpec(bias.shape, lambda b, p: (0, 0, 0))],
        out_specs=pl.BlockSpec((None, seq, LANES), lambda b, p: (b, 0, p)),
        out_shape=jax.ShapeDtypeStruct((batch, seq, n_pairs * LANES), BF16),
        compiler_params=_params("parallel", "parallel"),
        name="swa",
    )(sink, proj, proj, vt, bias)


def _mla_kernel(q_ref, k_ref, vt_ref, o_ref, s_ref, p_ref, *, seq, tk):
    tq = MLA_TQ
    chunks = [slice(c * tk, (c + 1) * tk) for c in range(seq // tk)]
    n_units = 2 * (q_ref.shape[1] // tq)

    def fold(x):
        return x.reshape(tk // SUBLANES, SUBLANES, tq)

    def scores(u):
        t, h = divmod(u, 2)
        q = q_ref[0, t * tq:(t + 1) * tq, h * LANES:(h + 1) * LANES]
        m_run = jnp.full((SUBLANES, tq), NEG_INF, F32)
        for ck in chunks:
            st = _dot_nt(k_ref[0, ck, h * LANES:(h + 1) * LANES], q)
            s_ref[u % 2, ck, :] = st
            m_run = jnp.maximum(m_run, jnp.max(fold(st), axis=0))
        return jnp.max(m_run, axis=0, keepdims=True)

    def probs(u, m):
        l_run = jnp.zeros((SUBLANES, tq), F32)
        for ck in chunks:
            pt = jnp.exp2(s_ref[u % 2, ck, :] - m)
            l_run = l_run + jnp.sum(fold(pt), axis=0)
            p_ref[u % 2, ck, :] = pt.astype(BF16)
        return jnp.sum(l_run, axis=0, keepdims=True)

    m = scores(0)
    out_a = None
    for u in range(n_units):
        m_next = scores(u + 1) if u + 1 < n_units else None
        l = probs(u, m)
        h = u % 2
        out_t = _dot(vt_ref[h * MLA_V:(h + 1) * MLA_V, :], p_ref[u % 2]) / l
        if h == 0:
            out_a = out_t
        else:
            t = u // 2
            pair = jnp.concatenate([out_a, out_t], axis=0)
            o_ref[0, t * tq:(t + 1) * tq, :] = pair.T.astype(o_ref.dtype)
        m = m_next


def _mla(proj, vt, batch, seq):
    tq = MLA_TQ * MLA_SUB
    assert seq % tq == 0 and seq % MLA_TK == 0 and 2 * MLA_V == LANES
    n_pairs = MLA_HEADS // 2
    return pl.pallas_call(
        functools.partial(_mla_kernel, seq=seq, tk=MLA_TK),
        grid=(batch, n_pairs, seq // tq),
        in_specs=[pl.BlockSpec((1, tq, 2 * LANES), lambda b, p, i: (b, i, 4 + p)),
                  pl.BlockSpec((1, seq, 2 * LANES), lambda b, p, i: (b, 0, 8 + p)),
                  pl.BlockSpec((None, 2 * MLA_V, seq), lambda b, p, i: (b, p, 0))],
        out_specs=pl.BlockSpec((1, tq, LANES), lambda b, p, i: (b, i, p)),
        out_shape=jax.ShapeDtypeStruct((batch, seq, n_pairs * LANES), BF16),
        scratch_shapes=[pltpu.VMEM((2, seq, MLA_TQ), F32), pltpu.VMEM((2, seq, MLA_TQ), BF16)],
        compiler_params=_params("parallel", "parallel", "parallel"),
        name="mla",
    )(proj, proj, vt)


def _rope_tables(seq):
    pos = jnp.arange(seq, dtype=F32)

    def tables(dim):
        inv_freq = ROPE_THETA ** (-jnp.arange(0, dim, 2, dtype=F32) / dim)
        ang = pos[:, None] * inv_freq[None, :]
        return jnp.cos(ang), jnp.sin(ang)

    cos, sin = tables(HEAD_DIM)
    reps = LANES // HEAD_DIM
    cos_t = jnp.tile(jnp.concatenate([cos, cos], axis=1), (1, reps))
    sin_t = jnp.tile(jnp.concatenate([-sin, sin], axis=1), (1, reps))
    cos_r, sin_r = tables(MLA_ROPE)
    pad_l, pad_r = MLA_NOPE, LANES - MLA_NOPE - MLA_ROPE
    cos_rt = jnp.pad(jnp.concatenate([cos_r, cos_r], axis=1), ((0, 0), (pad_l, pad_r)), constant_values=1.0)
    sin_rt = jnp.pad(jnp.concatenate([-sin_r, sin_r], axis=1), ((0, 0), (pad_l, pad_r)))
    return cos_t, sin_t, cos_rt, sin_rt


def kernel(x, ffn1_norm, ffn1_w1, ffn1_w3, ffn1_w2, mix_norm, ffn2_norm, ffn2_w1, ffn2_w3, ffn2_w2, even_w_in, even_w_out, na_rel_bias, odd_w_in, odd_w_out, swa_sink, mla_q_norm, mla_w_qb, mla_kv_norm, mla_w_kvb, final_norm):
    batch, seq, d = x.shape
    depth = ffn1_norm.shape[0]
    assert d == D_MODEL and seq % (FFN_TM * PROJ_SUB) == 0 and (batch * seq) % (FFN_TM * FFN_SUB) == 0
    assert seq % (GRID_W * NA_ROWS) == 0
    cos_t, sin_t, cos_rt, sin_rt = _rope_tables(seq)
    h = x.reshape(batch * seq, d)
    row = lambda v: v.reshape(1, -1).astype(F32)
    fg = row(final_norm)

    ffn1_w = _ffn_weights(ffn1_w1, ffn1_w3, ffn1_w2)
    ffn2_w = _ffn_weights(ffn2_w1, ffn2_w3, ffn2_w2)
    for i in range(depth):
        j = i // 2
        h = _ffn(h, row(ffn1_norm[i]), *ffn1_w, i, fg, False)
        if i % 2 == 0:
            dil_proj, na_proj = _even_proj(h, row(mix_norm[i]), even_w_in[j].astype(BF16), cos_t, sin_t, seq)
            oa = _dilated(dil_proj.reshape(batch, seq, -1), batch, seq).reshape(batch * seq, -1)
            on = _neighbourhood(na_proj.reshape(batch, seq, -1), na_rel_bias[j], batch, seq).reshape(batch * seq, -1)
            mix = (oa, on, even_w_out[j].astype(BF16))
        else:
            w, wqb, wkb, wvbt, wvst = _odd_weights(odd_w_in[j], mla_w_qb[j], mla_w_kvb[j])
            proj, vt, vst = _odd_proj(h, row(mix_norm[i]), w, row(mla_q_norm[j]), wqb, row(mla_kv_norm[j]), wkb, wvbt,
                                      wvst, cos_t, sin_t, cos_rt, sin_rt, seq)
            proj = proj.reshape(batch, seq, -1)
            oc = _swa(proj, vst, swa_sink[j].astype(F32), batch, seq).reshape(batch * seq, -1)
            od = _mla(proj, vt, batch, seq).reshape(batch * seq, -1)
            mix = (oc, od, odd_w_out[j].astype(BF16))
        h = _ffn(h, row(ffn2_norm[i]), *ffn2_w, i, fg, i == depth - 1, mix)
    return h.reshape(batch, seq, d)
```

```python
import functools

import jax
import jax.numpy as jnp
import numpy as np
from jax import lax
from jax.experimental import pallas as pl
from jax.experimental.pallas import tpu as pltpu

D_MODEL = 1024
HEAD_DIM = 64
ROPE_THETA = 10000.0
NORM_EPS = 1e-6
NEG_INF = -1e30
LOG2_E = 1.4426950408889634

DIL_HEADS = 8
DIL_CONFIGS = ((128, 1), (512, 4), (2048, 16))
NA_HEADS = 8
GRID_W = 64
NA_ROWS = 8
NA_COLS = 16

SWA_Q_HEADS = 8
SWA_KV_HEADS = 2
SWA_HALF = 128
MLA_HEADS = 8
MLA_Q_RANK = 384
MLA_KV_RANK = 256
MLA_NOPE = 64
MLA_ROPE = 32
MLA_V = 64

LANES = 128
SUBLANES = 8
VMEM_LIMIT = 56 * 1024 * 1024

FFN_TM = 512
FFN_TF = 256
FFN_SUB = 2
PROJ_SUB = 2
MLA_TQ = 256
MLA_SUB = 8
MLA_TK = 1024
DIL_TQ = 128
DIL_UNROLL = 16
DIL_SKEW = 2
DIL_MERGE_ROWS = 512
SWA_TQ = 128
SWA_SKEW = 4
NA_GROUP = 1
NA_UNROLL = 32
NA_SKEW = 3

BF16 = jnp.bfloat16
F32 = jnp.float32


def _params(*sem):
    return pltpu.CompilerParams(dimension_semantics=sem, vmem_limit_bytes=VMEM_LIMIT)


def _rms(x, g):
    ms = jnp.mean(x * x, axis=-1, keepdims=True)
    return x * lax.rsqrt(ms + NORM_EPS) * g


def _dot(a, b):
    return jnp.dot(a, b, preferred_element_type=F32)


def _dot_nt(a, b):
    return lax.dot_general(a, b, (((1,), (1,)), ((), ())), preferred_element_type=F32)


def _lane_is_first_head(shape):
    return lax.broadcasted_iota(jnp.int32, shape, len(shape) - 1) < HEAD_DIM


def _rope_block(x, cos, sin_signed, first_half, up, down):
    partner = jnp.where(first_half, pltpu.roll(x, up, 1), pltpu.roll(x, down, 1))
    return x * cos + partner * sin_signed


def _ffn_kernel(*refs, mix, final):
    if mix:
        x_ref, oc_ref, od_ref, wa_ref, wb_ref, g_ref, w1_ref, w3_ref, w2_ref, fg_ref, o_ref, a_ref = refs
    else:
        x_ref, g_ref, w1_ref, w3_ref, w2_ref, fg_ref, o_ref, a_ref = refs
    tm = x_ref.shape[0] // FFN_SUB
    for t in range(FFN_SUB):
        rows = slice(t * tm, (t + 1) * tm)
        x = x_ref[rows, :]
        if mix:
            x = x + _dot(oc_ref[rows, :], wa_ref[...]) + _dot(od_ref[rows, :], wb_ref[...])
        hn = _rms(x, g_ref[...]).astype(BF16)
        for c in range(a_ref.shape[1] // FFN_TF):
            cols = slice(c * FFN_TF, (c + 1) * FFN_TF)
            h1 = _dot(hn, w1_ref[:, cols])
            h3 = _dot(hn, w3_ref[:, cols])
            a_ref[rows, cols] = (h1 * jax.nn.sigmoid(h1) * h3).astype(BF16)
        y = x + 0.5 * _dot(a_ref[rows, :], w2_ref[...])
        if final:
            y = _rms(y, fg_ref[...])
        o_ref[rows, :] = y


def _ffn(x2, g, w1, w3, w2, layer, fg, final, mix=None):
    m, d = x2.shape
    f = w1.shape[2]
    assert f % FFN_TF == 0
    tm = FFN_TM * FFN_SUB
    const = lambda i: (0, 0)
    of_layer = lambda i: (layer, 0, 0)
    resident = dict(pipeline_mode=pl.Buffered(1))
    mix_args, mix_specs = (), []
    if mix is not None:
        oc, od, w_out = mix
        half_w = w_out.shape[0] // 2
        mix_args = (oc, od, w_out[:half_w], w_out[half_w:])
        mix_specs = [pl.BlockSpec((tm, half_w), lambda i: (i, 0))] * 2 + [pl.BlockSpec((half_w, d), const, **resident)] * 2
    return pl.pallas_call(
        functools.partial(_ffn_kernel, mix=mix is not None, final=final),
        grid=(m // tm,),
        in_specs=[pl.BlockSpec((tm, d), lambda i: (i, 0))] + mix_specs + [
            pl.BlockSpec((1, d), const),
            pl.BlockSpec((None, d, f), of_layer, **resident),
            pl.BlockSpec((None, d, f), of_layer, **resident),
            pl.BlockSpec((None, f, d), of_layer, **resident),
            pl.BlockSpec((1, d), const),
        ],
        out_specs=pl.BlockSpec((tm, d), lambda i: (i, 0)),
        out_shape=jax.ShapeDtypeStruct((m, d), F32),
        scratch_shapes=[pltpu.VMEM((tm, f), BF16)],
        compiler_params=_params("parallel"),
        name="ffn",
    )(x2, *mix_args, g, w1, w3, w2, fg)


def _ffn_weights(w1, w3, w2):
    return w1.astype(BF16), w3.astype(BF16), w2.astype(BF16)


_EVEN_DIL_BLOCKS = 12
_EVEN_Q_BLOCKS = tuple(range(0, 4)) + tuple(range(12, 16))
_EVEN_ROPE_BLOCKS = 8


def _even_proj_kernel(x_ref, g_ref, w_ref, cos_ref, sin_ref, dil_ref, na_ref, *, n_blocks):
    tm = x_ref.shape[0] // PROJ_SUB
    for t in range(PROJ_SUB):
        rows = slice(t * tm, (t + 1) * tm)
        _even_proj_rows(x_ref.at[rows], g_ref, w_ref, cos_ref.at[rows], sin_ref.at[rows],
                        dil_ref.at[rows], na_ref.at[rows], n_blocks=n_blocks)


def _even_proj_rows(x_ref, g_ref, w_ref, cos_ref, sin_ref, dil_ref, na_ref, *, n_blocks):
    hn = _rms(x_ref[...], g_ref[...]).astype(BF16)
    cos = cos_ref[...]
    sin = sin_ref[...]
    lane = lax.broadcasted_iota(jnp.int32, cos.shape, 1)
    first_half = (lane % HEAD_DIM) < HEAD_DIM // 2
    scale = HEAD_DIM ** -0.5 * LOG2_E
    for j2 in range(0, n_blocks, 2):
        wide = _dot(hn, w_ref[:, j2 * LANES:(j2 + 2) * LANES])
        for j in (j2, j2 + 1):
            blk = wide[:, (j - j2) * LANES:(j - j2 + 1) * LANES]
            if j < _EVEN_ROPE_BLOCKS:
                blk = _rope_block(blk, cos, sin, first_half, LANES - HEAD_DIM // 2, HEAD_DIM // 2)
            if j in _EVEN_Q_BLOCKS:
                blk = blk * scale
            if j < _EVEN_DIL_BLOCKS:
                dil_ref[:, j * LANES:(j + 1) * LANES] = blk
            else:
                jn = j - _EVEN_DIL_BLOCKS
                na_ref[:, jn * LANES:(jn + 1) * LANES] = blk.astype(BF16)


def _even_proj(h2, g, w_in, cos, sin, seq):
    m, d = h2.shape
    n_out = w_in.shape[1]
    n_dil = _EVEN_DIL_BLOCKS * LANES
    tm = FFN_TM * PROJ_SUB
    per_seq = seq // tm
    return pl.pallas_call(
        functools.partial(_even_proj_kernel, n_blocks=n_out // LANES),
        grid=(m // tm,),
        in_specs=[
            pl.BlockSpec((tm, d), lambda i: (i, 0)),
            pl.BlockSpec((1, d), lambda i: (0, 0)),
            pl.BlockSpec((d, n_out), lambda i: (0, 0)),
            pl.BlockSpec((tm, LANES), lambda i: (i % per_seq, 0)),
            pl.BlockSpec((tm, LANES), lambda i: (i % per_seq, 0)),
        ],
        out_specs=[pl.BlockSpec((tm, n_dil), lambda i: (i, 0)),
                   pl.BlockSpec((tm, n_out - n_dil), lambda i: (i, 0))],
        out_shape=[jax.ShapeDtypeStruct((m, n_dil), F32),
                   jax.ShapeDtypeStruct((m, n_out - n_dil), BF16)],
        compiler_params=_params("parallel"),
        name="even_proj",
    )(h2, g, w_in, cos, sin)


def _pair_scores(q, k):
    first = _lane_is_first_head(q.shape)
    zero = jnp.zeros_like(q)
    q2 = jnp.concatenate([jnp.where(first, q, zero), jnp.where(first, zero, q)], axis=0)
    return _dot_nt(q2, k)


def _pair_merge(x2, tq):
    first = _lane_is_first_head((tq, LANES))
    return jnp.where(first, x2[:tq], x2[tq:])


def _band_bias(tq, half, win):
    row = np.arange(2 * tq)[:, None] % tq
    col = np.arange(win)[None, :]
    kinds = [np.where(np.abs(col - row - k * half) <= half, 0.0, NEG_INF) for k in range(3)]
    return jnp.asarray(np.stack(kinds), F32)


def _band_window(q0, tq, half, length, win):
    ks = pl.multiple_of(jnp.clip(q0 - half, 0, length - win), half)
    return ks, (q0 - ks) // half


def _dilated_kernel(q_ref, k_ref, v_ref, *refs, seq, configs, tiles):
    n = len(configs)
    bias_refs, o_ref = refs[:n], refs[n]
    qs_ref, ks_ref, vs_ref = refs[n + 1:n + 4]
    ob_refs, mb_refs, lb_refs = refs[n + 4:2 * n + 4], refs[2 * n + 4:3 * n + 4], refs[3 * n + 4:4 * n + 4]
    stage_refs = refs[4 * n + 4:]
    copies = {1: (q_ref, k_ref, v_ref)}
    for c, ((half, dil), (tq, win)) in enumerate(zip(configs, tiles)):
        length = seq // dil
        blocks = length // tq
        base_dil = max(d for d in copies if dil % d == 0)
        step = dil // base_dil
        keep = dil > 1 and any(d2 > dil and d2 % dil == 0 for _, d2 in configs)
        if keep:
            copies[dil] = stage_refs
        for a, (src, dst) in enumerate(zip(copies[base_dil], (qs_ref, ks_ref, vs_ref))):
            for r in range(dil):
                start = (r % base_dil) * (seq // base_dil) + r // base_dil
                rows = pl.ds(start, length, stride=step) if step > 1 else pl.ds(start, length)
                x = src[rows, :]
                dst[r * length:(r + 1) * length, :] = x.astype(BF16)
                if keep:
                    stage_refs[a][r * length:(r + 1) * length, :] = x

        def body(i, carry, c=c, half=half, dil=dil, tq=tq, win=win, length=length, blocks=blocks):
            geo = []
            for g in range(DIL_UNROLL):
                n = i * DIL_UNROLL + g
                r = n // blocks
                q0 = pl.multiple_of((n % blocks) * tq, tq)
                ks, kind = _band_window(q0, tq, half, length, win)
                geo.append((r, q0, pl.multiple_of(r * length, tq) + ks, kind))
            def score(u):
                r, q0, kb, kind = geo[u]
                q = qs_ref[pl.ds(pl.multiple_of(r * length, tq) + q0, tq), :]
                return _pair_scores(q, ks_ref[pl.ds(kb, win), :]) + bias_refs[c][kind]

            ss = {u: score(u) for u in range(min(DIL_SKEW, DIL_UNROLL))}
            for u in range(DIL_UNROLL):
                if u + DIL_SKEW < DIL_UNROLL:
                    ss[u + DIL_SKEW] = score(u + DIL_SKEW)
                s = ss.pop(u)
                r, q0, kb, _ = geo[u]
                m = jnp.max(s, axis=-1, keepdims=True)
                p = jnp.exp2(s - m)
                l = jnp.sum(p, axis=-1, keepdims=True)
                o2 = _dot(p.astype(BF16), vs_ref[pl.ds(kb, win), :])
                rows = pl.ds(r + dil * q0, tq, stride=dil) if dil > 1 else pl.ds(q0, tq)
                ob_refs[c][rows, :] = _pair_merge(o2, tq)
                mb_refs[c][rows, :] = _pair_merge(jnp.broadcast_to(m, o2.shape), tq)
                lb_refs[c][rows, :] = _pair_merge(jnp.broadcast_to(l, o2.shape), tq)
            return carry

        lax.fori_loop(0, seq // (tq * DIL_UNROLL), body, 0)

    chunk = DIL_MERGE_ROWS

    def merge(i, carry):
        rows = pl.ds(pl.multiple_of(i * chunk, chunk), chunk)
        ms = [mb[rows, :] for mb in mb_refs]
        mx = functools.reduce(jnp.maximum, ms)
        ws = [jnp.exp2(m - mx) for m in ms]
        num = sum(w * ob[rows, :] for w, ob in zip(ws, ob_refs))
        den = sum(w * lb[rows, :] for w, lb in zip(ws, lb_refs))
        o_ref[rows, :] = (num / den).astype(o_ref.dtype)
        return carry

    lax.fori_loop(0, seq // chunk, merge, 0)


def _dilated(dil_proj, batch, seq):
    n_pairs = DIL_HEADS // 2
    configs = tuple((window // 2 // dil, dil) for window, dil in DIL_CONFIGS)
    tiles = []
    for half, dil in configs:
        tq = min(DIL_TQ, seq // dil)
        tiles.append((tq, min(tq + 2 * half, seq // dil)))
        assert seq % (dil * tq) == 0 and tq % half == 0
    biases = [_band_bias(tq, half, win) for (half, _), (tq, win) in zip(configs, tiles)]

    def spec(base):
        return pl.BlockSpec((None, seq, LANES), lambda b, p: (b, 0, base + p))

    return pl.pallas_call(
        functools.partial(_dilated_kernel, seq=seq, configs=configs, tiles=tuple(tiles)),
        grid=(batch, n_pairs),
        in_specs=[spec(0), spec(4), spec(8)] + [pl.BlockSpec(b.shape, lambda b_, p: (0, 0, 0)) for b in biases],
        out_specs=pl.BlockSpec((None, seq, LANES), lambda b, p: (b, 0, p)),
        out_shape=jax.ShapeDtypeStruct((batch, seq, n_pairs * LANES), BF16),
        scratch_shapes=[pltpu.VMEM((seq, LANES), BF16)] * 3 + [pltpu.VMEM((seq, LANES), F32)] * (3 * len(configs) + 3),
        compiler_params=_params("parallel", "parallel"),
        name="dilated",
    )(dil_proj, dil_proj, dil_proj, *biases)


def _na_group_geometry(rows, kr):
    n_groups = rows // NA_GROUP
    wr = -(-(kr + NA_GROUP - 1) * GRID_W // LANES) * LANES // GRID_W
    g = np.arange(n_groups)
    start = np.minimum(np.clip(g * NA_GROUP - kr // 2, 0, rows - kr), rows - wr)
    return n_groups, wr, start


def _na_kernel(start_ref, kind_ref, q_ref, k_ref, v_ref, bias_ref, o_ref, *, rows, kr):
    n_groups, wr, _ = _na_group_geometry(rows, kr)
    tq = NA_GROUP * GRID_W

    def body(i, carry):
        geo = []
        for u in range(NA_UNROLL):
            g = i * NA_UNROLL + u
            geo.append((pl.multiple_of(g * tq, tq), pl.multiple_of(start_ref[g] * GRID_W, GRID_W), kind_ref[g]))
        def score(u):
            q0, k0, kind = geo[u]
            return _pair_scores(q_ref[pl.ds(q0, tq), :], k_ref[pl.ds(k0, wr * GRID_W), :]) + bias_ref[kind]

        ss = {u: score(u) for u in range(min(NA_SKEW, NA_UNROLL))}
        for u in range(NA_UNROLL):
            if u + NA_SKEW < NA_UNROLL:
                ss[u + NA_SKEW] = score(u + NA_SKEW)
            s = ss.pop(u)
            q0, k0, _ = geo[u]
            m = jnp.max(s, axis=-1, keepdims=True)
            p = jnp.exp2(s - m)
            l = jnp.sum(p, axis=-1, keepdims=True)
            o2 = _dot(p.astype(BF16), v_ref[pl.ds(k0, wr * GRID_W), :])
            o_ref[pl.ds(q0, tq), :] = _pair_merge(o2 / l, tq).astype(o_ref.dtype)
        return carry

    lax.fori_loop(0, n_groups // NA_UNROLL, body, 0)


def _na_bias_table(rpb, rows, kr):
    n_groups, wr, start = _na_group_geometry(rows, kr)
    qcol = np.arange(GRID_W)
    kcol = np.arange(GRID_W)
    wstart = np.clip(qcol - NA_COLS // 2, 0, GRID_W - NA_COLS)
    col_ok = (kcol[None, :] >= wstart[:, None]) & (kcol[None, :] < wstart[:, None] + NA_COLS)
    dc = np.clip(kcol[None, :] - qcol[:, None] + NA_COLS - 1, 0, 2 * NA_COLS - 2)

    def geometry(g):
        qrow = g * NA_GROUP + np.arange(NA_GROUP)
        krow = start[g] + np.arange(wr)
        rs = np.clip(qrow - kr // 2, 0, rows - kr)
        row_ok = (krow[None, :] >= rs[:, None]) & (krow[None, :] < rs[:, None] + kr)
        dr = np.where(row_ok, krow[None, :] - qrow[:, None] + NA_ROWS - 1, -1)
        return dr

    patterns, kind_of = [], []
    for g in range(n_groups):
        dr = geometry(g)
        match = [k for k, p in enumerate(patterns) if np.array_equal(p, dr)]
        if not match:
            patterns.append(dr)
        kind_of.append(match[0] if match else len(patterns) - 1)
    onehot = (dc[None] == np.arange(2 * NA_COLS - 1)[:, None, None]) & col_ok[None]
    planes = jnp.einsum("hrd,dqk->hrqk", rpb * LOG2_E, jnp.asarray(onehot, F32), precision=lax.Precision.HIGHEST)
    planes = planes + jnp.asarray(np.where(col_ok, 0.0, NEG_INF), F32)
    masked = jnp.full(planes.shape[:1] + planes.shape[2:], NEG_INF, F32)
    slabs = []
    for dr in patterns:
        rows_i = []
        for i in range(NA_GROUP):
            cols_j = [planes[:, int(dr[i, j])] if dr[i, j] >= 0 else masked for j in range(wr)]
            rows_i.append(jnp.concatenate(cols_j, axis=-1))
        slab = jnp.stack(rows_i, axis=1)
        slabs.append(slab.reshape(NA_HEADS // 2, 2 * NA_GROUP * GRID_W, wr * GRID_W))
    table = jnp.stack(slabs)
    return table, jnp.asarray(start, jnp.int32), jnp.asarray(kind_of, jnp.int32)


def _neighbourhood(proj, rpb, batch, seq):
    rows = seq // GRID_W
    kr = min(NA_ROWS, rows)
    assert rows % (NA_GROUP * NA_UNROLL) == 0 and rows >= kr + NA_GROUP
    table, start, kind_of = _na_bias_table(rpb.astype(F32), rows, kr)
    n_pairs = NA_HEADS // 2

    def spec(base):
        return pl.BlockSpec((None, seq, LANES), lambda b, p: (b, 0, base + p))

    smem = pl.BlockSpec(memory_space=pltpu.SMEM)
    return pl.pallas_call(
        functools.partial(_na_kernel, rows=rows, kr=kr),
        grid=(batch, n_pairs),
        in_specs=[smem, smem, spec(0), spec(4), spec(8),
                  pl.BlockSpec((table.shape[0], None) + table.shape[2:], lambda b, p: (0, p, 0, 0))],
        out_specs=pl.BlockSpec((None, seq, LANES), lambda b, p: (b, 0, p)),
        out_shape=jax.ShapeDtypeStruct((batch, seq, n_pairs * LANES), BF16),
        compiler_params=_params("parallel", "parallel"),
        name="neighbourhood",
    )(start, kind_of, proj, proj, proj, table)


_ODD_W_COLS = 14 * LANES
_ODD_OUT_COLS = 24 * LANES


def _odd_proj_kernel(x_ref, g_ref, w_ref, qn_ref, wqb_ref, kvn_ref, wkb_ref, wvbt_ref, wvst_ref,
                     cos_ref, sin_ref, cosr_ref, sinr_ref, o_ref, vt_ref, vst_ref):
    tm = x_ref.shape[0] // PROJ_SUB
    for t in range(PROJ_SUB):
        rows = slice(t * tm, (t + 1) * tm)
        _odd_proj_rows(x_ref.at[rows], g_ref, w_ref, qn_ref, wqb_ref, kvn_ref, wkb_ref, wvbt_ref, wvst_ref,
                       cos_ref.at[rows], sin_ref.at[rows], cosr_ref.at[rows], sinr_ref.at[rows],
                       o_ref.at[rows], vt_ref.at[:, rows], vst_ref.at[:, rows])


def _odd_proj_rows(x_ref, g_ref, w_ref, qn_ref, wqb_ref, kvn_ref, wkb_ref, wvbt_ref, wvst_ref,
                   cos_ref, sin_ref, cosr_ref, sinr_ref, o_ref, vt_ref, vst_ref):
    hn = _rms(x_ref[...], g_ref[...]).astype(BF16)
    cos, sin = cos_ref[...], sin_ref[...]
    cosr, sinr = cosr_ref[...], sinr_ref[...]
    lane = lax.broadcasted_iota(jnp.int32, cos.shape, 1)
    first_half = (lane % HEAD_DIM) < HEAD_DIM // 2
    first_half_r = lane < MLA_NOPE + MLA_ROPE // 2
    swa_scale = HEAD_DIM ** -0.5 * LOG2_E
    mla_scale = (MLA_NOPE + MLA_ROPE) ** -0.5 * LOG2_E

    def col(j, n=1):
        return slice(j * LANES, (j + n) * LANES)

    def rope(blk):
        return _rope_block(blk, cos, sin, first_half, LANES - HEAD_DIM // 2, HEAD_DIM // 2)

    def rope_r(blk):
        return _rope_block(blk, cosr, sinr, first_half_r, LANES - MLA_ROPE // 2, MLA_ROPE // 2)

    def halves(wide):
        return wide[:, :LANES], wide[:, LANES:]

    latent = _dot(hn, w_ref[:, col(8, 6)])
    for j in range(0, 6, 2):
        for jj, blk in zip((j, j + 1), halves(_dot(hn, w_ref[:, col(j, 2)]))):
            blk = rope(blk) * swa_scale if jj < 4 else rope(blk)
            o_ref[:, col(jj)] = blk.astype(BF16)

    q_an = _rms(latent[:, :3 * LANES], qn_ref[...]).astype(BF16)
    kv_an = _rms(latent[:, 3 * LANES:5 * LANES], kvn_ref[...]).astype(BF16)
    k_pe = rope_r(latent[:, 5 * LANES:])
    for h in range(0, MLA_HEADS, 2):
        for hh, qh in zip((h, h + 1), halves(_dot(q_an, wqb_ref[:, col(h, 2)]))):
            o_ref[:, col(8 + hh)] = (rope_r(qh) * mla_scale).astype(BF16)
    for h in range(0, MLA_HEADS, 2):
        for hh, kh in zip((h, h + 1), halves(_dot(kv_an, wkb_ref[:, col(h, 2)]))):
            o_ref[:, col(16 + hh)] = (kh + k_pe).astype(BF16)
    o_ref[:, col(6, 2)] = jnp.zeros((o_ref.shape[0], 2 * LANES), BF16)
    vst_ref[...] = _dot(hn, wvst_ref[...]).T.astype(BF16)
    vt_ref[...] = _dot_nt(wvbt_ref[...], kv_an).astype(BF16)


def _odd_proj(h2, g, w, qn, wqb, kvn, wkb, wvbt, wvst, cos, sin, cosr, sinr, seq):
    m, d = h2.shape
    tm = FFN_TM * PROJ_SUB
    per_seq = seq // tm
    n_v = wvbt.shape[0]
    full = lambda a: pl.BlockSpec(a.shape, lambda i: (0, 0))
    tab = pl.BlockSpec((tm, LANES), lambda i: (i % per_seq, 0))
    return pl.pallas_call(
        _odd_proj_kernel,
        grid=(m // tm,),
        in_specs=[pl.BlockSpec((tm, d), lambda i: (i, 0)), full(g), full(w), full(qn), full(wqb),
                  full(kvn), full(wkb), full(wvbt), full(wvst), tab, tab, tab, tab],
        out_specs=[pl.BlockSpec((tm, _ODD_OUT_COLS), lambda i: (i, 0)),
                   pl.BlockSpec((None, n_v, tm), lambda i: (i // per_seq, 0, i % per_seq)),
                   pl.BlockSpec((None, wvst.shape[1], tm), lambda i: (i // per_seq, 0, i % per_seq))],
        out_shape=[jax.ShapeDtypeStruct((m, _ODD_OUT_COLS), BF16),
                   jax.ShapeDtypeStruct((m // seq, n_v, seq), BF16),
                   jax.ShapeDtypeStruct((m // seq, wvst.shape[1], seq), BF16)],
        compiler_params=_params("parallel"),
        name="odd_proj",
    )(h2, g, w, qn, wqb, kvn, wkb, wvbt, wvst, cos, sin, cosr, sinr)


def _odd_weights(w_in, w_qb, w_kvb):
    d = w_in.shape[0]
    c = np.cumsum([0, SWA_Q_HEADS * HEAD_DIM, SWA_KV_HEADS * HEAD_DIM, SWA_KV_HEADS * HEAD_DIM,
                   MLA_Q_RANK, MLA_KV_RANK, MLA_ROPE])

    def dup(w):
        w = w.reshape(d, SWA_KV_HEADS, 1, HEAD_DIM)
        return jnp.broadcast_to(w, (d, SWA_KV_HEADS, 2, HEAD_DIM)).reshape(d, SWA_KV_HEADS * LANES)

    w_kpe = jnp.pad(w_in[:, c[5]:c[6]], ((0, 0), (MLA_NOPE, LANES - MLA_NOPE - MLA_ROPE)))
    w = jnp.concatenate([w_in[:, c[0]:c[1]], dup(w_in[:, c[1]:c[2]]), dup(w_in[:, c[2]:c[3]]),
                         w_in[:, c[3]:c[4]], w_in[:, c[4]:c[5]], w_kpe], axis=1).astype(BF16)
    qk = MLA_NOPE + MLA_ROPE
    wqb = jnp.pad(w_qb.reshape(MLA_Q_RANK, MLA_HEADS, qk), ((0, 0), (0, 0), (0, LANES - qk)))
    wqb = wqb.reshape(MLA_Q_RANK, MLA_HEADS * LANES).astype(BF16)
    kv = w_kvb.reshape(MLA_KV_RANK, MLA_HEADS, MLA_NOPE + MLA_V)
    wkb = jnp.pad(kv[:, :, :MLA_NOPE], ((0, 0), (0, 0), (0, LANES - MLA_NOPE)))
    wkb = wkb.reshape(MLA_KV_RANK, MLA_HEADS * LANES).astype(BF16)
    wvbt = kv[:, :, MLA_NOPE:].reshape(MLA_KV_RANK, MLA_HEADS * MLA_V).T.astype(BF16)
    wvst = w_in[:, c[2]:c[3]].astype(BF16)
    return w, wqb, wkb, wvbt, wvst


def _swa_kernel(sink_ref, q_ref, k_ref, vt_ref, bias_ref, o_ref, *, seq, half, tq, win):
    pair = pl.program_id(1)
    first = lax.broadcasted_iota(jnp.int32, (1, 2 * tq), 1) < tq
    sink = jnp.where(first, sink_ref[2 * pair], sink_ref[2 * pair + 1]) * LOG2_E
    blocks = []
    for n in range(seq // tq):
        q0 = n * tq
        ks = min(max(q0 - half, 0), seq - win)
        blocks.append((q0, ks, (q0 - ks) // half))

    def score(u):
        q0, ks, kind = blocks[u]
        q = q_ref[q0:q0 + tq, :]
        head_a = _lane_is_first_head(q.shape)
        zero = jnp.zeros_like(q)
        q2 = jnp.concatenate([jnp.where(head_a, q, zero), jnp.where(head_a, zero, q)], axis=0)
        return _dot_nt(k_ref[ks:ks + win, :], q2) + bias_ref[kind]

    ss = {u: score(u) for u in range(min(SWA_SKEW, len(blocks)))}
    for u, (q0, ks, _) in enumerate(blocks):
        if u + SWA_SKEW < len(blocks):
            ss[u + SWA_SKEW] = score(u + SWA_SKEW)
        st = ss.pop(u)
        m = jnp.maximum(jnp.max(st, axis=0, keepdims=True), sink)
        pt = jnp.exp2(st - m)
        l = jnp.sum(pt, axis=0, keepdims=True) + jnp.exp2(sink - m)
        ot = _dot(vt_ref[:, ks:ks + win], pt.astype(BF16)) / l
        both = jnp.concatenate([ot[:, :tq], ot[:, tq:]], axis=0)
        o_ref[q0:q0 + tq, :] = both.T.astype(o_ref.dtype)


def _swa_bias(tq, half, win):
    key = np.arange(win)[:, None]
    qry = np.arange(2 * tq)[None, :] % tq
    kinds = [np.where(np.abs(key - qry - k * half) <= half, 0.0, NEG_INF) for k in range(3)]
    return jnp.asarray(np.stack(kinds), F32)


def _swa(proj, vt, sink, batch, seq):
    tq = SWA_TQ
    win = tq + 2 * SWA_HALF
    assert seq % tq == 0 and tq % SWA_HALF == 0 and seq >= win and tq == LANES
    n_pairs = SWA_Q_HEADS // 2
    pairs_per_kv = n_pairs // SWA_KV_HEADS
    bias = _swa_bias(tq, SWA_HALF, win)
    return pl.pallas_call(
        functools.partial(_swa_kernel, seq=seq, half=SWA_HALF, tq=tq, win=win),
        grid=(batch, n_pairs),
        in_specs=[pl.BlockSpec(memory_space=pltpu.SMEM),
                  pl.BlockSpec((None, seq, LANES), lambda b, p: (b, 0, p)),
                  pl.BlockSpec((None, seq, LANES), lambda b, p: (b, 0, 4 + p // pairs_per_kv)),
                  pl.BlockSpec((None, HEAD_DIM, seq), lambda b, p: (b, p // pairs_per_kv, 0)),
                  pl.BlockSpec(bias.shape, lambda b, p: (0, 0, 0))],
        out_specs=pl.BlockSpec((None, seq, LANES), lambda b, p: (b, 0, p)),
        out_shape=jax.ShapeDtypeStruct((batch, seq, n_pairs * LANES), BF16),
        compiler_params=_params("parallel", "parallel"),
        name="swa",
    )(sink, proj, proj, vt, bias)


def _mla_kernel(q_ref, k_ref, vt_ref, o_ref, s_ref, p_ref, *, seq, tk):
    tq = MLA_TQ
    chunks = [slice(c * tk, (c + 1) * tk) for c in range(seq // tk)]
    n_units = 2 * (q_ref.shape[1] // tq)

    def fold(x):
        return x.reshape(tk // SUBLANES, SUBLANES, tq)

    def scores(u):
        t, h = divmod(u, 2)
        q = q_ref[0, t * tq:(t + 1) * tq, h * LANES:(h + 1) * LANES]
        m_run = jnp.full((SUBLANES, tq), NEG_INF, F32)
        for ck in chunks:
            st = _dot_nt(k_ref[0, ck, h * LANES:(h + 1) * LANES], q)
            s_ref[u % 2, ck, :] = st
            m_run = jnp.maximum(m_run, jnp.max(fold(st), axis=0))
        return jnp.max(m_run, axis=0, keepdims=True)

    def probs(u, m):
        l_run = jnp.zeros((SUBLANES, tq), F32)
        for ck in chunks:
            pt = jnp.exp2(s_ref[u % 2, ck, :] - m)
            l_run = l_run + jnp.sum(fold(pt), axis=0)
            p_ref[u % 2, ck, :] = pt.astype(BF16)
        return jnp.sum(l_run, axis=0, keepdims=True)

    m = scores(0)
    out_a = None
    for u in range(n_units):
        m_next = scores(u + 1) if u + 1 < n_units else None
        l = probs(u, m)
        h = u % 2
        out_t = _dot(vt_ref[h * MLA_V:(h + 1) * MLA_V, :], p_ref[u % 2]) / l
        if h == 0:
            out_a = out_t
        else:
            t = u // 2
            pair = jnp.concatenate([out_a, out_t], axis=0)
            o_ref[0, t * tq:(t + 1) * tq, :] = pair.T.astype(o_ref.dtype)
        m = m_next


def _mla(proj, vt, batch, seq):
    tq = MLA_TQ * MLA_SUB
    assert seq % tq == 0 and seq % MLA_TK == 0 and 2 * MLA_V == LANES
    n_pairs = MLA_HEADS // 2
    return pl.pallas_call(
        functools.partial(_mla_kernel, seq=seq, tk=MLA_TK),
        grid=(batch, n_pairs, seq // tq),
        in_specs=[pl.BlockSpec((1, tq, 2 * LANES), lambda b, p, i: (b, i, 4 + p)),
                  pl.BlockSpec((1, seq, 2 * LANES), lambda b, p, i: (b, 0, 8 + p)),
                  pl.BlockSpec((None, 2 * MLA_V, seq), lambda b, p, i: (b, p, 0))],
        out_specs=pl.BlockSpec((1, tq, LANES), lambda b, p, i: (b, i, p)),
        out_shape=jax.ShapeDtypeStruct((batch, seq, n_pairs * LANES), BF16),
        scratch_shapes=[pltpu.VMEM((2, seq, MLA_TQ), F32), pltpu.VMEM((2, seq, MLA_TQ), BF16)],
        compiler_params=_params("parallel", "parallel", "parallel"),
        name="mla",
    )(proj, proj, vt)


def _rope_tables(seq):
    pos = jnp.arange(seq, dtype=F32)

    def tables(dim):
        inv_freq = ROPE_THETA ** (-jnp.arange(0, dim, 2, dtype=F32) / dim)
        ang = pos[:, None] * inv_freq[None, :]
        return jnp.cos(ang), jnp.sin(ang)

    cos, sin = tables(HEAD_DIM)
    reps = LANES // HEAD_DIM
    cos_t = jnp.tile(jnp.concatenate([cos, cos], axis=1), (1, reps))
    sin_t = jnp.tile(jnp.concatenate([-sin, sin], axis=1), (1, reps))
    cos_r, sin_r = tables(MLA_ROPE)
    pad_l, pad_r = MLA_NOPE, LANES - MLA_NOPE - MLA_ROPE
    cos_rt = jnp.pad(jnp.concatenate([cos_r, cos_r], axis=1), ((0, 0), (pad_l, pad_r)), constant_values=1.0)
    sin_rt = jnp.pad(jnp.concatenate([-sin_r, sin_r], axis=1), ((0, 0), (pad_l, pad_r)))
    return cos_t, sin_t, cos_rt, sin_rt


def kernel(x, ffn1_norm, ffn1_w1, ffn1_w3, ffn1_w2, mix_norm, ffn2_norm, ffn2_w1, ffn2_w3, ffn2_w2, even_w_in, even_w_out, na_rel_bias, odd_w_in, odd_w_out, swa_sink, mla_q_norm, mla_w_qb, mla_kv_norm, mla_w_kvb, final_norm):
    batch, seq, d = x.shape
    depth = ffn1_norm.shape[0]
    assert d == D_MODEL and seq % (FFN_TM * PROJ_SUB) == 0 and (batch * seq) % (FFN_TM * FFN_SUB) == 0
    assert seq % (GRID_W * NA_ROWS) == 0
    cos_t, sin_t, cos_rt, sin_rt = _rope_tables(seq)
    h = x.reshape(batch * seq, d)
    row = lambda v: v.reshape(1, -1).astype(F32)
    fg = row(final_norm)

    ffn1_w = _ffn_weights(ffn1_w1, ffn1_w3, ffn1_w2)
    ffn2_w = _ffn_weights(ffn2_w1, ffn2_w3, ffn2_w2)
    for i in range(depth):
        j = i // 2
        h = _ffn(h, row(ffn1_norm[i]), *ffn1_w, i, fg, False)
        if i % 2 == 0:
            dil_proj, na_proj = _even_proj(h, row(mix_norm[i]), even_w_in[j].astype(BF16), cos_t, sin_t, seq)
            oa = _dilated(dil_proj.reshape(batch, seq, -1), batch, seq).reshape(batch * seq, -1)
            on = _neighbourhood(na_proj.reshape(batch, seq, -1), na_rel_bias[j], batch, seq).reshape(batch * seq, -1)
            mix = (oa, on, even_w_out[j].astype(BF16))
        else:
            w, wqb, wkb, wvbt, wvst = _odd_weights(odd_w_in[j], mla_w_qb[j], mla_w_kvb[j])
            proj, vt, vst = _odd_proj(h, row(mix_norm[i]), w, row(mla_q_norm[j]), wqb, row(mla_kv_norm[j]), wkb, wvbt,
                                      wvst, cos_t, sin_t, cos_rt, sin_rt, seq)
            proj = proj.reshape(batch, seq, -1)
            oc = _swa(proj, vst, swa_sink[j].astype(F32), batch, seq).reshape(batch * seq, -1)
            od = _mla(proj, vt, batch, seq).reshape(batch * seq, -1)
            mix = (oc, od, odd_w_out[j].astype(BF16))
        h = _ffn(h, row(ffn2_norm[i]), *ffn2_w, i, fg, i == depth - 1, mix)
    return h.reshape(batch, seq, d)
```
